```python
import math
import jax, jax.numpy as jnp
from jax import lax
import numpy as np

D_MODEL = 2048
BATCH = 8
SEQ = 2048
DEPTH = 1

GLA_HEADS = 4
GLA_DK = D_MODEL // 8
GLA_DV = D_MODEL // 4
GLA_KEY = GLA_HEADS * GLA_DK
GLA_VAL = GLA_HEADS * GLA_DV
GLA_GATE_RANK = 16
GLA_TAU = 16.0
GLA_CHUNK = 64
HY_WIDTH = D_MODEL
HY_ORDER = 2
HY_SHORT = 3
HY_EMB = 33
HY_FILTER_HIDDEN = 64
HY_FAST_DECAY = 0.3
HY_SLOW_DECAY = 1.5
HY_DECAY_TARGET = 1e-2
HY_FILTER_OUT = HY_ORDER * 2 * HY_WIDTH
N_BRANCH = 2
D_FF = 4 * D_MODEL
LN_EPS = 1e-5
DEEPNORM_ALPHA = (2 * DEPTH) ** 0.25
DEEPNORM_BETA = (8 * DEPTH) ** -0.25
IN_SPLIT_SIZES = (GLA_KEY, GLA_KEY, GLA_VAL, GLA_VAL, GLA_GATE_RANK, GLA_GATE_RANK, 3 * HY_WIDTH, N_BRANCH * D_MODEL)
IN_WIDTH = sum(IN_SPLIT_SIZES)

kernel_name = 'hybrid_gla_hyena_deepnorm_block'


def _layernorm(x, g, b):
    xf = x.astype(jnp.float32)
    mu = jnp.mean(xf, axis=-1, keepdims=True)
    var = jnp.mean(jnp.square(xf - mu), axis=-1, keepdims=True)
    return ((xf - mu) * lax.rsqrt(var + LN_EPS) * g + b).astype(x.dtype)


def _gla_one_direction(q, k, v, log_a):
    B, H, S, DK = q.shape
    DV = v.shape[-1]
    C = GLA_CHUNK
    n = S // C
    q = q.reshape(B, H, n, C, DK)
    k = k.reshape(B, H, n, C, DK)
    v = v.reshape(B, H, n, C, DV)
    b = jnp.cumsum(log_a.reshape(B, H, n, C, DK), axis=3)
    b_last = b[:, :, :, -1, :]
    q_in = q * jnp.exp(b)
    k_in = k * jnp.exp(-b)
    k_end = k * jnp.exp(b_last[:, :, :, None, :] - b)
    lower_tri = jnp.tril(jnp.ones((C, C), dtype=bool))
    scores = jnp.where(lower_tri, jnp.einsum('bhnid,bhnjd->bhnij', q_in, k_in), 0.0)
    o_intra = jnp.einsum('bhnij,bhnje->bhnie', scores, v)

    def step(state, inp):
        q_c, k_c, v_c, bl_c = inp
        o_c = jnp.einsum('bhid,bhde->bhie', q_c, state)
        state = jnp.exp(bl_c)[..., None] * state + jnp.einsum('bhjd,bhje->bhde', k_c, v_c)
        return state, o_c

    xs = (jnp.moveaxis(q_in, 2, 0), jnp.moveaxis(k_end, 2, 0), jnp.moveaxis(v, 2, 0), jnp.moveaxis(b_last, 2, 0))
    _, o_inter = lax.scan(step, jnp.zeros((B, H, DK, DV), jnp.float32), xs)
    o = o_intra + jnp.moveaxis(o_inter, 0, 2)
    return o.reshape(B, H, S, DV)


def _gla_branch(q, k, v, r, af, ab, wa2_f, ba_f, wa2_b, ba_b, norm_g):
    B, L, _ = q.shape
    f32 = jnp.float32

    def heads(t, d):
        return t.astype(f32).reshape(B, L, GLA_HEADS, d).transpose(0, 2, 1, 3)

    qh = heads(q, GLA_DK) * (GLA_DK ** -0.5)
    kh = heads(k, GLA_DK)
    vh = heads(v, GLA_DV)
    la_f = heads(jax.nn.log_sigmoid((af @ wa2_f + ba_f).astype(f32)) / GLA_TAU, GLA_DK)
    la_b = heads(jax.nn.log_sigmoid((ab @ wa2_b + ba_b).astype(f32)) / GLA_TAU, GLA_DK)
    flip = lambda t: jnp.flip(t, axis=2)
    o_f = _gla_one_direction(qh, kh, vh, la_f)
    o_b = flip(_gla_one_direction(flip(qh), flip(kh), flip(vh), flip(la_b)))
    o_b = o_b - jnp.sum(qh * kh, axis=-1, keepdims=True) * vh
    o = (o_f + o_b).transpose(0, 2, 1, 3)
    o = o * lax.rsqrt(jnp.mean(jnp.square(o), axis=-1, keepdims=True) + LN_EPS)
    o = o * norm_g.astype(f32).reshape(GLA_HEADS, GLA_DV)
    o = o * jax.nn.silu(r.astype(f32)).reshape(B, L, GLA_HEADS, GLA_DV)
    return o.reshape(B, L, GLA_VAL).astype(q.dtype)


def _short_conv(u, w, b):
    C = u.shape[-1]
    pad = HY_SHORT // 2
    y = lax.conv_general_dilated(u, w[:, None, :].astype(u.dtype), (1,), [(pad, pad)],
                                 dimension_numbers=('NWC', 'WIO', 'NWC'), feature_group_count=C)
    return y + b


def _hyena_filters(L, w1, b1, w2, b2, w3, b3, w4, b4, freq):
    f32 = jnp.float32
    t = jnp.linspace(0.0, 1.0, L, dtype=f32)[:, None]
    bands = (HY_EMB - 1) // 2
    f = jnp.linspace(1e-4, bands - 1, bands, dtype=f32)
    wpos = 2.0 * math.pi * jnp.arange(L, dtype=f32) / L
    ang = wpos[:, None] * f[None, :]
    emb = jnp.concatenate([t, jnp.cos(ang), -jnp.sin(ang)], axis=-1)
    fr = freq.astype(f32)
    h = jnp.sin(fr * (emb @ w1.astype(f32) + b1.astype(f32)))
    h = jnp.sin(fr * (h @ w2.astype(f32) + b2.astype(f32)))
    h = jnp.sin(fr * (h @ w3.astype(f32) + b3.astype(f32)))
    h = h @ w4.astype(f32) + b4.astype(f32)
    min_decay = math.log(HY_DECAY_TARGET) / HY_SLOW_DECAY
    max_decay = math.log(HY_DECAY_TARGET) / HY_FAST_DECAY
    deltas = jnp.abs(jnp.linspace(min_decay, max_decay, HY_WIDTH, dtype=f32))
    decay = jnp.exp(-t * deltas[None, :])
    return h.reshape(L, HY_ORDER, 2, HY_WIDTH) * decay[:, None, None, :]


def _hyena_branch(u, conv_w, conv_b, w1, b1, w2, b2, w3, b3, w4, b4, freq, skip):
    B, L, _ = u.shape
    out_dtype = u.dtype
    f32 = jnp.float32
    uc = _short_conv(u, conv_w, conv_b).astype(f32)
    v, x1, x2 = jnp.split(uc, 3, axis=-1)
    h = _hyena_filters(L, w1, b1, w2, b2, w3, b3, w4, b4, freq)
    filt = jnp.concatenate([h[:, :, 0], jnp.zeros((1, HY_ORDER, HY_WIDTH), f32), jnp.flip(h[1:, :, 1], axis=0)], axis=0)
    filt_f = jnp.fft.rfft(filt, axis=0)
    z = v
    for o, gate in enumerate((x1, x2)):
        zf = jnp.fft.rfft(z, n=2 * L, axis=1)
        zc = jnp.fft.irfft(zf * filt_f[None, :, o], n=2 * L, axis=1)[:, :L]
        z = gate * (zc + skip[o].astype(f32) * z)
    return z.astype(out_dtype)


def setup_inputs(seed: int = 0) -> dict:
    key = jax.random.key(seed)
    ks = iter(jax.random.split(key, 40))
    D = D_MODEL

    def nrm(shape, scale):
        return jax.random.normal(next(ks), shape, jnp.float32) * scale

    x = nrm((BATCH, SEQ, D), 1.0)
    s_in = D ** -0.5
    w_in = jnp.concatenate([
        nrm((DEPTH, D, GLA_KEY), s_in),
        nrm((DEPTH, D, GLA_KEY), s_in),
        nrm((DEPTH, D, GLA_VAL), s_in * DEEPNORM_BETA),
        nrm((DEPTH, D, GLA_VAL), s_in),
        nrm((DEPTH, D, GLA_GATE_RANK), s_in),
        nrm((DEPTH, D, GLA_GATE_RANK), s_in),
        nrm((DEPTH, D, HY_WIDTH), s_in * DEEPNORM_BETA),
        nrm((DEPTH, D, 2 * HY_WIDTH), s_in),
        nrm((DEPTH, D, N_BRANCH * D), s_in),
    ], axis=-1)
    r_s = GLA_GATE_RANK ** -0.5
    return {
        'x': x,
        'w_in': w_in,
        'gla_wa2_f': nrm((DEPTH, GLA_GATE_RANK, GLA_KEY), r_s),
        'gla_ba_f': nrm((DEPTH, GLA_KEY), 0.1),
        'gla_wa2_b': nrm((DEPTH, GLA_GATE_RANK, GLA_KEY), r_s),
        'gla_ba_b': nrm((DEPTH, GLA_KEY), 0.1),
        'gla_norm_g': 1.0 + nrm((DEPTH, GLA_VAL), 0.02),
        'w_gla_o': nrm((DEPTH, GLA_VAL, D), GLA_VAL ** -0.5 * DEEPNORM_BETA),
        'hy_conv_w': nrm((DEPTH, HY_SHORT, 3 * HY_WIDTH), HY_SHORT ** -0.5),
        'hy_conv_b': nrm((DEPTH, 3 * HY_WIDTH), 0.02),
        'hy_w1': nrm((DEPTH, HY_EMB, HY_FILTER_HIDDEN), HY_EMB ** -0.5),
        'hy_b1': nrm((DEPTH, HY_FILTER_HIDDEN), 0.02),
        'hy_w2': nrm((DEPTH, HY_FILTER_HIDDEN, HY_FILTER_HIDDEN), HY_FILTER_HIDDEN ** -0.5),
        'hy_b2': nrm((DEPTH, HY_FILTER_HIDDEN), 0.02),
        'hy_w3': nrm((DEPTH, HY_FILTER_HIDDEN, HY_FILTER_HIDDEN), HY_FILTER_HIDDEN ** -0.5),
        'hy_b3': nrm((DEPTH, HY_FILTER_HIDDEN), 0.02),
        'hy_w4': nrm((DEPTH, HY_FILTER_HIDDEN, HY_FILTER_OUT), 0.02),
        'hy_b4': nrm((DEPTH, HY_FILTER_OUT), 0.002),
        'hy_freq': 1.0 + nrm((DEPTH, HY_FILTER_HIDDEN), 0.01),
        'hy_skip': nrm((DEPTH, HY_ORDER, HY_WIDTH), 1.0),
        'w_hy_o': nrm((DEPTH, HY_WIDTH, D), HY_WIDTH ** -0.5 * DEEPNORM_BETA),
        'w_out': nrm((DEPTH, D, D), D ** -0.5 * DEEPNORM_BETA),
        'ln1_g': 1.0 + nrm((DEPTH, D), 0.02),
        'ln1_b': nrm((DEPTH, D), 0.02),
        'w_ff1': nrm((DEPTH, D, D_FF), D ** -0.5 * DEEPNORM_BETA),
        'w_ff2': nrm((DEPTH, D_FF, D), D_FF ** -0.5 * DEEPNORM_BETA),
        'ln2_g': 1.0 + nrm((DEPTH, D), 0.02),
        'ln2_b': nrm((DEPTH, D), 0.02),
    }


def reference(x, w_in, gla_wa2_f, gla_ba_f, gla_wa2_b, gla_ba_b, gla_norm_g, w_gla_o,
              hy_conv_w, hy_conv_b, hy_w1, hy_b1, hy_w2, hy_b2, hy_w3, hy_b3, hy_w4, hy_b4,
              hy_freq, hy_skip, w_hy_o, w_out, ln1_g, ln1_b, w_ff1, w_ff2, ln2_g, ln2_b):
    B, L, D = x.shape
    split_idx = [int(i) for i in np.cumsum(IN_SPLIT_SIZES)[:-1]]
    h = x
    for l in range(DEPTH):
        proj = h @ w_in[l]
        q, k, v, r, af, ab, hy_u, gate_logits = jnp.split(proj, split_idx, axis=-1)
        y_gla = _gla_branch(q, k, v, r, af, ab, gla_wa2_f[l], gla_ba_f[l], gla_wa2_b[l], gla_ba_b[l], gla_norm_g[l])
        y_hy = _hyena_branch(hy_u, hy_conv_w[l], hy_conv_b[l], hy_w1[l], hy_b1[l], hy_w2[l], hy_b2[l],
                             hy_w3[l], hy_b3[l], hy_w4[l], hy_b4[l], hy_freq[l], hy_skip[l])
        g = jax.nn.sigmoid(gate_logits.astype(jnp.float32)).astype(h.dtype).reshape(B, L, N_BRANCH, D)
        merged = g[:, :, 0, :] * (y_gla @ w_gla_o[l]) + g[:, :, 1, :] * (y_hy @ w_hy_o[l])
        mix = merged @ w_out[l]
        h = _layernorm(DEEPNORM_ALPHA * h + mix, ln1_g[l], ln1_b[l])
        ff = jnp.square(jax.nn.relu(h @ w_ff1[l])) @ w_ff2[l]
        h = _layernorm(DEEPNORM_ALPHA * h + ff, ln2_g[l], ln2_b[l])
    return h
```

```python
import functools
import math

import jax
import jax.numpy as jnp
import numpy as np
from jax import lax
from jax.experimental import pallas as pl
from jax.experimental.pallas import tpu as pltpu

F32 = jnp.float32
BF16 = jnp.bfloat16

GLA_HEADS = 4
GLA_TAU = 16.0
GLA_CHUNK = 64
HY_FAST_DECAY = 0.3
HY_SLOW_DECAY = 1.5
HY_DECAY_TARGET = 1e-2
LN_EPS = 1e-5

V7X_VMEM_LIMIT_BYTES = 56 * 1024 * 1024
V7X_LANES = 128

NN = (((1,), (0,)), ((), ()))
NT = (((1,), (1,)), ((), ()))
TN = (((0,), (0,)), ((), ()))


def _dot(a, b, dims=NN):
    return lax.dot_general(a.astype(BF16), b.astype(BF16), dims, preferred_element_type=F32)


def _split3(a):
    h1 = a.astype(BF16)
    r1 = a - h1.astype(F32)
    h2 = r1.astype(BF16)
    h3 = (r1 - h2.astype(F32)).astype(BF16)
    return h1, h2, h3


def _dot_f32(a, b, dims=NN):
    a1, a2, _ = _split3(a)
    b1, b2, _ = _split3(b)
    d = lambda x, y: lax.dot_general(x, y, dims, preferred_element_type=F32)
    return (d(a2, b1) + d(a1, b2)) + d(a1, b1)


def _dot_exact_lhs(a_bf16, b, dims=NN):
    b1, b2, b3 = _split3(b)
    d = lambda y: lax.dot_general(a_bf16, y, dims, preferred_element_type=F32)
    return (d(b3) + d(b2)) + d(b1)


def _call(body, *, grid, in_specs, out_specs, out_shape, name, scratch_shapes=()):
    return pl.pallas_call(
        body,
        grid=grid,
        in_specs=in_specs,
        out_specs=out_specs,
        out_shape=out_shape,
        scratch_shapes=scratch_shapes,
        compiler_params=pltpu.CompilerParams(
            dimension_semantics=("arbitrary",) * len(grid),
            vmem_limit_bytes=V7X_VMEM_LIMIT_BYTES,
        ),
        name=name,
    )


def _blk(n, want):
    b = min(n, want)
    while n % b:
        b //= 2
    return b


def _layernorm_rows(y, g, b):
    mu = jnp.mean(y, axis=-1, keepdims=True)
    d = y - mu
    var = jnp.mean(d * d, axis=-1, keepdims=True)
    return d * lax.rsqrt(var + LN_EPS) * g + b


def _mm_body(a_ref, b_ref, o_ref, *, act):
    acc = _dot(a_ref[...], b_ref[...])
    if act == "sigmoid":
        acc = jax.nn.sigmoid(acc)
    elif act == "relu2":
        acc = jnp.square(jnp.maximum(acc, 0.0))
    o_ref[...] = acc.astype(o_ref.dtype)


def _matmul(a, b, *, out_dtype, act=None, name, bm=1024, bn=1024):
    M, K = a.shape
    N = b.shape[1]
    bm, bn = _blk(M, bm), _blk(N, bn)
    return _call(
        functools.partial(_mm_body, act=act),
        grid=(M // bm, N // bn),
        in_specs=[pl.BlockSpec((bm, K), lambda i, j: (i, 0)),
                  pl.BlockSpec((K, bn), lambda i, j: (0, j))],
        out_specs=pl.BlockSpec((bm, bn), lambda i, j: (i, j)),
        out_shape=jax.ShapeDtypeStruct((M, N), out_dtype),
        name=name,
    )(a, b)


def _hy_proj_body(w_ref, x_ref, cp_ref, o_ref):
    u = _dot(w_ref[...], x_ref[...], NT)
    L = u.shape[1]
    t = lax.broadcasted_iota(jnp.int32, u.shape, 1)
    prev = jnp.where(t == 0, 0.0, pltpu.roll(u, 1, axis=1))
    nxt = jnp.where(t == L - 1, 0.0, pltpu.roll(u, L - 1, axis=1))
    cp = cp_ref[...]
    o_ref[...] = cp[:, 0:1] * prev + cp[:, 1:2] * u + cp[:, 2:3] * nxt + cp[:, 3:4]


def _hy_proj(w_t, xb, conv_params, B, L):
    C3, D = w_t.shape
    bc = _blk(C3, 512)
    nc = C3 // bc
    return _call(
        _hy_proj_body,
        grid=(B, nc),
        in_specs=[pl.BlockSpec((bc, D), lambda b, c: (c, 0)),
                  pl.BlockSpec((L, D), lambda b, c: (b, 0)),
                  pl.BlockSpec((bc, 4), lambda b, c: (c, 0))],
        out_specs=pl.BlockSpec((bc, L), lambda b, c: (b * nc + c, 0)),
        out_shape=jax.ShapeDtypeStruct((B * C3, L), F32),
        name="hy_proj_conv",
    )(w_t, xb, conv_params)


def _log_sigmoid(z):
    return -(jnp.maximum(-z, 0.0) + jnp.log1p(jnp.exp(-jnp.abs(z))))


def _decay_body(x_ref, wab_ref, w2f_ref, bf_ref, w2b_ref, bb_ref, laf_ref, lab_ref):
    ab = _dot_f32(x_ref[...], wab_ref[...])
    zf = _dot_f32(ab, w2f_ref[...]) + bf_ref[...]
    zb = _dot_f32(ab, w2b_ref[...]) + bb_ref[...]
    laf_ref[...] = _log_sigmoid(zf) * (1.0 / GLA_TAU)
    lab_ref[...] = _log_sigmoid(zb) * (1.0 / GLA_TAU)


def _gla_decays(x2d, w_ab, wa2_f, ba_f, wa2_b, ba_b):
    M, D = x2d.shape
    rank, KEY = wa2_f.shape
    P = V7X_LANES
    assert 2 * rank <= P
    w_ab = jnp.pad(w_ab, ((0, 0), (0, P - 2 * rank)))
    w2f = jnp.pad(wa2_f, ((0, P - rank), (0, 0)))
    w2b = jnp.pad(wa2_b, ((rank, P - 2 * rank), (0, 0)))
    bm = _blk(M, 512)
    full = lambda shape: pl.BlockSpec(shape, lambda i: (0, 0))
    out = jax.ShapeDtypeStruct((M, KEY), F32)
    return _call(
        _decay_body,
        grid=(M // bm,),
        in_specs=[pl.BlockSpec((bm, D), lambda i: (i, 0)), full((D, P)),
                  full((P, KEY)), full((1, KEY)), full((P, KEY)), full((1, KEY))],
        out_specs=[pl.BlockSpec((bm, KEY), lambda i: (i, 0))] * 2,
        out_shape=[out, out],
        name="gla_decays",
    )(x2d, w_ab, w2f, ba_f.reshape(1, KEY), w2b, ba_b.reshape(1, KEY))


def _gla_body(q_ref, k_ref, v_ref, r_ref, laf_ref, lab_ref, g_ref, y_ref, o_acc, st_ref, *, chunk):
    L, DK = q_ref.shape
    DV = v_ref.shape[1]
    C = chunk
    n = L // C
    scale = DK ** -0.5
    row = lax.broadcasted_iota(jnp.int32, (C, C), 0)
    col = lax.broadcasted_iota(jnp.int32, (C, C), 1)
    incl_lower = (col <= row)
    strict_upper = (col > row)
    tri_f = incl_lower.astype(BF16)
    tri_b = (col >= row).astype(BF16)

    def chunk_step(c, la_ref, tri, mask, edge, first):
        rows = pl.ds(pl.multiple_of(c * C, C), C)
        q = q_ref[rows, :] * scale
        k = k_ref[rows, :]
        v = v_ref[rows, :].astype(BF16)
        cum = _dot_exact_lhs(tri, la_ref[rows, :])
        tot = cum[edge:edge + 1, :]
        q_in = (q * jnp.exp(cum)).astype(BF16)
        k_in = k * jnp.exp(-cum)
        k_end = k * jnp.exp(tot - cum)
        scores = jnp.where(mask, _dot(q_in, k_in, NT), 0.0)
        st = st_ref[...]
        o = _dot(scores, v) + _dot(q_in, st, NT)
        if first:
            o_acc[rows, :] = o
        else:
            o_acc[rows, :] += o
        st_ref[...] = st * jnp.exp(tot) + _dot(v, k_end, TN)

    st_ref[...] = jnp.zeros_like(st_ref)

    def fwd(c, carry):
        chunk_step(c, laf_ref, tri_f, incl_lower, C - 1, True)
        return carry

    lax.fori_loop(0, n, fwd, 0)
    st_ref[...] = jnp.zeros_like(st_ref)

    def bwd(i, carry):
        chunk_step(n - 1 - i, lab_ref, tri_b, strict_upper, 0, False)
        return carry

    lax.fori_loop(0, n, bwd, 0)

    g = g_ref[...]

    def finish(c, carry):
        rows = pl.ds(pl.multiple_of(c * C, C), C)
        o = o_acc[rows, :]
        o = o * lax.rsqrt(jnp.mean(o * o, axis=-1, keepdims=True) + LN_EPS) * g
        r = r_ref[rows, :]
        y_ref[rows, :] = (o * (r * jax.nn.sigmoid(r))).astype(y_ref.dtype)
        return carry

    lax.fori_loop(0, n, finish, 0)


def _gla(p_qkvr, laf, lab, norm_g, B, L, KEY, VAL):
    H = GLA_HEADS
    DK, DV = KEY // H, VAL // H
    M = B * L
    kq = KEY // DK
    vq = 2 * KEY // DV
    rq = (2 * KEY + VAL) // DV
    return _call(
        functools.partial(_gla_body, chunk=_blk(L, GLA_CHUNK)),
        grid=(B, H),
        in_specs=[pl.BlockSpec((L, DK), lambda b, h: (b, h)),
                  pl.BlockSpec((L, DK), lambda b, h: (b, kq + h)),
                  pl.BlockSpec((L, DV), lambda b, h: (b, vq + h)),
                  pl.BlockSpec((L, DV), lambda b, h: (b, rq + h)),
                  pl.BlockSpec((L, DK), lambda b, h: (b, h)),
                  pl.BlockSpec((L, DK), lambda b, h: (b, h)),
                  pl.BlockSpec((1, DV), lambda b, h: (0, h))],
        out_specs=pl.BlockSpec((L, DV), lambda b, h: (b, h)),
        out_shape=jax.ShapeDtypeStruct((M, VAL), BF16),
        scratch_shapes=[pltpu.VMEM((L, DV), F32), pltpu.VMEM((DV, DK), F32)],
        name="gla_bidir",
    )(p_qkvr, p_qkvr, p_qkvr, p_qkvr, laf, lab, norm_g.reshape(1, VAL))


def _filter_body(emb_ref, w1_ref, b1_ref, w2_ref, b2_ref, w3_ref, b3_ref, fr_ref,
                 w4f_ref, b4f_ref, w4b_ref, b4b_ref, dl_ref, a_ref, bm_ref, ny_ref):
    emb = emb_ref[...]
    fr = fr_ref[...]
    h = jnp.sin(fr * (_dot_f32(w1_ref[...], emb) + b1_ref[...]))
    h = jnp.sin(fr * (_dot_f32(w2_ref[...], h) + b2_ref[...]))
    h = jnp.sin(fr * (_dot_f32(w3_ref[...], h) + b3_ref[...]))
    t_lin = emb[0:1, :]
    decay = jnp.exp(-t_lin * dl_ref[...])
    hf = (_dot_f32(w4f_ref[...], h) + b4f_ref[...]) * decay
    hb = (_dot_f32(w4b_ref[...], h) + b4b_ref[...]) * decay
    t = lax.broadcasted_iota(jnp.int32, hf.shape, 1)
    hb = jnp.where(t == 0, 0.0, hb)
    a = hf + hb
    a_ref[...] = a.astype(a_ref.dtype)
    bm_ref[...] = (hf - hb).astype(bm_ref.dtype)
    sign = jnp.where((t & 1) == 0, 1.0, -1.0)
    ny_ref[...] = jnp.sum(a * sign, axis=-1, keepdims=True)


def _hyena_filters(emb_t, w1, b1, w2, b2, w3, b3, w4, b4, freq, deltas, W):
    L = emb_t.shape[1]
    HID = w1.shape[1]
    n_ord = w4.shape[1] // (2 * W)
    EMB = -(-emb_t.shape[0] // V7X_LANES) * V7X_LANES
    w1 = jnp.pad(w1, ((0, EMB - w1.shape[0]), (0, 0)))
    emb_t = jnp.pad(emb_t, ((0, EMB - emb_t.shape[0]), (0, 0)))
    cb = _blk(W, 512)
    ncb = W // cb
    w4t = w4.T
    b4c = b4.reshape(-1, 1)
    colv = lambda v: v.reshape(-1, 1)
    full = lambda shape: pl.BlockSpec(shape, lambda o, c: (0, 0))
    rows = n_ord * W
    return _call(
        _filter_body,
        grid=(n_ord, ncb),
        in_specs=[full((EMB, L)), full((HID, EMB)), full((HID, 1)), full((HID, HID)), full((HID, 1)),
                  full((HID, HID)), full((HID, 1)), full((HID, 1)),
                  pl.BlockSpec((cb, HID), lambda o, c: (o * 2 * ncb + c, 0)),
                  pl.BlockSpec((cb, 1), lambda o, c: (o * 2 * ncb + c, 0)),
                  pl.BlockSpec((cb, HID), lambda o, c: (o * 2 * ncb + ncb + c, 0)),
                  pl.BlockSpec((cb, 1), lambda o, c: (o * 2 * ncb + ncb + c, 0)),
                  pl.BlockSpec((cb, 1), lambda o, c: (c, 0))],
        out_specs=[pl.BlockSpec((cb, L), lambda o, c: (o * ncb + c, 0)),
                   pl.BlockSpec((cb, L), lambda o, c: (o * ncb + c, 0)),
                   pl.BlockSpec((cb, 1), lambda o, c: (o * ncb + c, 0))],
        out_shape=[jax.ShapeDtypeStruct((rows, L), BF16), jax.ShapeDtypeStruct((rows, L), BF16),
                   jax.ShapeDtypeStruct((rows, 1), F32)],
        name="hy_filters",
    )(emb_t, w1.T, colv(b1), w2.T, colv(b2), w3.T, colv(b3), colv(freq), w4t, b4c, w4t, b4c, deltas)


def _spec_body(a_ref, f_ref, s_ref, o_ref):
    o_ref[...] = _dot(a_ref[...], f_ref[...]) * s_ref[...]


def _filter_spectrum(a, fmat, scale):
    R, L = a.shape
    bm, bn = _blk(R, 1024), _blk(L, 1024)
    return _call(
        _spec_body,
        grid=(R // bm, L // bn),
        in_specs=[pl.BlockSpec((bm, L), lambda i, j: (i, 0)),
                  pl.BlockSpec((L, bn), lambda i, j: (0, j)),
                  pl.BlockSpec((1, bn), lambda i, j: (0, j))],
        out_specs=pl.BlockSpec((bm, bn), lambda i, j: (i, j)),
        out_shape=jax.ShapeDtypeStruct((R, L), F32),
        name="hy_filter_spectrum",
    )(a, fmat, scale)


def _fwd_dft_body(z_ref, c_ref, s_ref, hr_ref, hi_ref, hn_ref, pr_ref, pi_ref):
    z = z_ref[...].astype(BF16)
    zr = _dot(z, c_ref[...])
    zi = _dot(z, s_ref[...])
    hr, hi = hr_ref[...], hi_ref[...]
    k = lax.broadcasted_iota(jnp.int32, zr.shape, 1) + pl.program_id(2) * zr.shape[1]
    dc = (k == 0)
    pr = zr * hr - jnp.where(dc, 0.0, zi * hi)
    pi = jnp.where(dc, zi * hn_ref[...], zr * hi + zi * hr)
    pr_ref[...] = pr.astype(pr_ref.dtype)
    pi_ref[...] = pi.astype(pi_ref.dtype)


def _fwd_dft(z2d, z_row0, z_rows_per_b, cmat, smat, hr, hi, hn, h_row0, B, W, L):
    bm, bn = _blk(W, 512), _blk(L, 512)
    ncb = W // bm
    zb, z0, h0 = z_rows_per_b // bm, z_row0 // bm, h_row0 // bm
    out = jax.ShapeDtypeStruct((B * W, L), BF16)
    return _call(
        _fwd_dft_body,
        grid=(B, ncb, L // bn),
        in_specs=[pl.BlockSpec((bm, L), lambda b, c, j: (b * zb + z0 + c, 0)),
                  pl.BlockSpec((L, bn), lambda b, c, j: (0, j)),
                  pl.BlockSpec((L, bn), lambda b, c, j: (0, j)),
                  pl.BlockSpec((bm, bn), lambda b, c, j: (h0 + c, j)),
                  pl.BlockSpec((bm, bn), lambda b, c, j: (h0 + c, j)),
                  pl.BlockSpec((bm, 1), lambda b, c, j: (h0 + c, 0))],
        out_specs=[pl.BlockSpec((bm, bn), lambda b, c, j: (b * ncb + c, j))] * 2,
        out_shape=[out, out],
        name="hy_fwd_dft",
    )(z2d, cmat, smat, hr, hi, hn)


def _inv_dft_body(pr_ref, pi_ref, c_ref, st_ref, gate_ref, z_ref, skip_ref, o_ref):
    zc = _dot(pr_ref[...], c_ref[...]) + _dot(pi_ref[...], st_ref[...])
    o_ref[...] = (gate_ref[...] * (zc + skip_ref[...] * z_ref[...])).astype(o_ref.dtype)


def _inv_dft(pr, pi, cmat, smat_t, gates2d, g_row0, g_rows_per_b, z2d, z_row0, z_rows_per_b,
             skip_col, s_row0, B, W, L, out_dtype):
    bm, bn = _blk(W, 512), _blk(L, 1024)
    ncb = W // bm
    gb, g0 = g_rows_per_b // bm, g_row0 // bm
    zb, z0 = z_rows_per_b // bm, z_row0 // bm
    s0 = s_row0 // bm
    return _call(
        _inv_dft_body,
        grid=(B, ncb, L // bn),
        in_specs=[pl.BlockSpec((bm, L), lambda b, c, j: (b * ncb + c, 0)),
                  pl.BlockSpec((bm, L), lambda b, c, j: (b * ncb + c, 0)),
                  pl.BlockSpec((L, bn), lambda b, c, j: (0, j)),
                  pl.BlockSpec((L, bn), lambda b, c, j: (0, j)),
                  pl.BlockSpec((bm, bn), lambda b, c, j: (b * gb + g0 + c, j)),
                  pl.BlockSpec((bm, bn), lambda b, c, j: (b * zb + z0 + c, j)),
                  pl.BlockSpec((bm, 1), lambda b, c, j: (s0 + c, 0))],
        out_specs=pl.BlockSpec((bm, bn), lambda b, c, j: (b * ncb + c, j)),
        out_shape=jax.ShapeDtypeStruct((B * W, L), out_dtype),
        name="hy_inv_dft",
    )(pr, pi, cmat, smat_t, gates2d, z2d, skip_col)


def _gla_out_body(y_ref, w_ref, g_ref, o_ref):
    o_ref[...] = g_ref[...] * _dot(y_ref[...], w_ref[...])


def _gla_out(y_gla, w_o, gates, D):
    M, K = y_gla.shape
    bm, bn = _blk(M, 1024), _blk(D, 1024)
    return _call(
        _gla_out_body,
        grid=(M // bm, D // bn),
        in_specs=[pl.BlockSpec((bm, K), lambda i, j: (i, 0)),
                  pl.BlockSpec((K, bn), lambda i, j: (0, j)),
                  pl.BlockSpec((bm, bn), lambda i, j: (i, j))],
        out_specs=pl.BlockSpec((bm, bn), lambda i, j: (i, j)),
        out_shape=jax.ShapeDtypeStruct((M, D), F32),
        name="gla_out_proj",
    )(y_gla, w_o, gates)


def _hy_out_body(y_ref, w_ref, g_ref, t_ref, o_ref):
    o_ref[...] = (t_ref[...] + g_ref[...] * _dot(y_ref[...], w_ref[...], TN)).astype(o_ref.dtype)


def _hy_out_merge(y_hy_t, w_o, gates, t_gla, B, L, W, D):
    bt, bn = _blk(L, 512), _blk(D, 1024)
    nt = L // bt
    g1 = D // bn
    return _call(
        _hy_out_body,
        grid=(B, nt, D // bn),
        in_specs=[pl.BlockSpec((W, bt), lambda b, t, j: (b, t)),
                  pl.BlockSpec((W, bn), lambda b, t, j: (0, j)),
                  pl.BlockSpec((bt, bn), lambda b, t, j: (b * nt + t, g1 + j)),
                  pl.BlockSpec((bt, bn), lambda b, t, j: (b * nt + t, j))],
        out_specs=pl.BlockSpec((bt, bn), lambda b, t, j: (b * nt + t, j)),
        out_shape=jax.ShapeDtypeStruct((B * L, D), BF16),
        name="hy_out_proj_merge",
    )(y_hy_t, w_o, gates, t_gla)


def _proj_ln_body(a_ref, w_ref, x_ref, g_ref, b_ref, o_ref, ob_ref, *, alpha):
    y = alpha * x_ref[...] + _dot(a_ref[...], w_ref[...])
    h = _layernorm_rows(y, g_ref[...], b_ref[...])
    o_ref[...] = h
    ob_ref[...] = h.astype(ob_ref.dtype)


def _proj_ln(a, w, x2d, ln_g, ln_b, alpha):
    M, K = a.shape
    D = w.shape[1]
    bm = _blk(M, 512)
    row = lambda i: (i, 0)
    full = lambda shape: pl.BlockSpec(shape, lambda i: (0, 0))
    return _call(
        functools.partial(_proj_ln_body, alpha=alpha),
        grid=(M // bm,),
        in_specs=[pl.BlockSpec((bm, K), row), full((K, D)), pl.BlockSpec((bm, D), row),
                  full((1, D)), full((1, D))],
        out_specs=[pl.BlockSpec((bm, D), row)] * 2,
        out_shape=[jax.ShapeDtypeStruct((M, D), F32), jax.ShapeDtypeStruct((M, D), BF16)],
        name="out_proj_ln1",
    )(a, w, x2d, ln_g.reshape(1, D), ln_b.reshape(1, D))


def _ff2_ln_body(a_ref, w_ref, h_ref, g_ref, b_ref, o_ref, acc_ref, *, alpha):
    kk = pl.program_id(1)

    @pl.when(kk == 0)
    def _():
        acc_ref[...] = jnp.zeros_like(acc_ref)

    acc_ref[...] += _dot(a_ref[...], w_ref[...])

    @pl.when(kk == pl.num_programs(1) - 1)
    def _():
        y = alpha * h_ref[...] + acc_ref[...]
        o_ref[...] = _layernorm_rows(y, g_ref[...], b_ref[...])


def _ff2_ln(a, w, h, ln_g, ln_b, alpha):
    M, K = a.shape
    D = w.shape[1]
    bm, bk = _blk(M, 512), _blk(K, 1024)
    return _call(
        functools.partial(_ff2_ln_body, alpha=alpha),
        grid=(M // bm, K // bk),
        in_specs=[pl.BlockSpec((bm, bk), lambda i, k: (i, k)),
                  pl.BlockSpec((bk, D), lambda i, k: (k, 0)),
                  pl.BlockSpec((bm, D), lambda i, k: (i, 0)),
                  pl.BlockSpec((1, D), lambda i, k: (0, 0)),
                  pl.BlockSpec((1, D), lambda i, k: (0, 0))],
        out_specs=pl.BlockSpec((bm, D), lambda i, k: (i, 0)),
        out_shape=jax.ShapeDtypeStruct((M, D), F32),
        scratch_shapes=[pltpu.VMEM((bm, D), F32)],
        name="ff2_ln2",
    )(a, w, h, ln_g.reshape(1, D), ln_b.reshape(1, D))


def _dft_tables(L):
    n = jnp.arange(L, dtype=jnp.int32)
    ph = ((n[:, None] * n[None, :]) % (2 * L)).astype(F32) * (math.pi / L)
    cmat = jnp.cos(ph)
    nyq = jnp.where(n % 2 == 0, 1.0, -1.0).astype(F32)
    smat = jnp.where(n[None, :] == 0, nyq[:, None], jnp.sin(ph))
    return cmat.astype(BF16), smat.astype(BF16), smat.T.astype(BF16)


def _position_features(L, emb_dim):
    t = jnp.linspace(0.0, 1.0, L, dtype=F32)[:, None]
    bands = (emb_dim - 1) // 2
    f = jnp.linspace(1e-4, bands - 1, bands, dtype=F32)
    wpos = 2.0 * math.pi * jnp.arange(L, dtype=F32) / L
    ang = wpos[:, None] * f[None, :]
    return jnp.concatenate([t, jnp.cos(ang), -jnp.sin(ang)], axis=-1).T


def _layer(h, w_in, gla_wa2_f, gla_ba_f, gla_wa2_b, gla_ba_b, gla_norm_g, w_gla_o,
           hy_conv_w, hy_conv_b, hy_w1, hy_b1, hy_w2, hy_b2, hy_w3, hy_b3, hy_w4, hy_b4,
           hy_freq, hy_skip, w_hy_o, w_out, ln1_g, ln1_b, w_ff1, w_ff2, ln2_g, ln2_b, alpha):
    B, L, D = h.shape
    M = B * L
    rank, KEY = gla_wa2_f.shape
    VAL = gla_norm_g.shape[0]
    n_ord, W = hy_skip.shape

    sizes = (KEY, KEY, VAL, VAL, rank, rank, (n_ord + 1) * W, 2 * D)
    offs = np.concatenate([[0], np.cumsum(sizes)])
    seg = lambda i, j: w_in[:, int(offs[i]):int(offs[j])]
    w_qkvr = seg(0, 4).astype(BF16)
    w_ab = seg(4, 6)
    w_hy_t = seg(6, 7).T.astype(BF16)
    w_gate = seg(7, 8).astype(BF16)

    x2d = h.reshape(M, D)
    xb = x2d.astype(BF16)

    p_qkvr = _matmul(xb, w_qkvr, out_dtype=F32, name="in_proj_qkvr")
    gates = _matmul(xb, w_gate, out_dtype=F32, act="sigmoid", name="in_proj_gates")
    conv_params = jnp.concatenate([hy_conv_w.T, hy_conv_b[:, None]], axis=1)
    u_t = _hy_proj(w_hy_t, xb, conv_params, B, L)

    laf, lab = _gla_decays(x2d, w_ab, gla_wa2_f, gla_ba_f, gla_wa2_b, gla_ba_b)
    y_gla = _gla(p_qkvr, laf, lab, gla_norm_g, B, L, KEY, VAL)

    emb_t = _position_features(L, hy_w1.shape[0])
    min_decay = math.log(HY_DECAY_TARGET) / HY_SLOW_DECAY
    max_decay = math.log(HY_DECAY_TARGET) / HY_FAST_DECAY
    deltas = jnp.abs(jnp.linspace(min_decay, max_decay, W, dtype=F32)).reshape(W, 1)
    a_even, a_odd, nyq = _hyena_filters(emb_t, hy_w1, hy_b1, hy_w2, hy_b2, hy_w3, hy_b3,
                                        hy_w4, hy_b4, hy_freq, deltas, W)
    cmat, smat, smat_t = _dft_tables(L)
    bin_w = jnp.full((1, L), 1.0 / L, F32).at[0, 0].set(0.5 / L)
    h_re = _filter_spectrum(a_even, cmat, bin_w)
    h_im = _filter_spectrum(a_odd, smat, bin_w)
    h_ny = nyq * (0.5 / L)

    skip_col = hy_skip.reshape(n_ord * W, 1)
    z, z_row0, z_rows = u_t, 0, (n_ord + 1) * W
    for o in range(n_ord):
        pr, pi = _fwd_dft(z, z_row0, z_rows, cmat, smat, h_re, h_im, h_ny, o * W, B, W, L)
        last = o == n_ord - 1
        z = _inv_dft(pr, pi, cmat, smat_t, u_t, (o + 1) * W, (n_ord + 1) * W, z, z_row0, z_rows,
                     skip_col, o * W, B, W, L, BF16 if last else F32)
        z_row0, z_rows = 0, W
    y_hy_t = z

    t_gla = _gla_out(y_gla, w_gla_o.astype(BF16), gates, D)
    merged = _hy_out_merge(y_hy_t, w_hy_o.astype(BF16), gates, t_gla, B, L, W, D)
    h1, h1b = _proj_ln(merged, w_out.astype(BF16), x2d, ln1_g, ln1_b, alpha)
    act = _matmul(h1b, w_ff1.astype(BF16), out_dtype=BF16, act="relu2", name="ff1_relu2")
    out = _ff2_ln(act, w_ff2.astype(BF16), h1, ln2_g, ln2_b, alpha)
    return out.reshape(B, L, D)


def kernel(x, w_in, gla_wa2_f, gla_ba_f, gla_wa2_b, gla_ba_b, gla_norm_g, w_gla_o, hy_conv_w, hy_conv_b, hy_w1, hy_b1, hy_w2, hy_b2, hy_w3, hy_b3, hy_w4, hy_b4, hy_freq, hy_skip, w_hy_o, w_out, ln1_g, ln1_b, w_ff1, w_ff2, ln2_g, ln2_b):
    depth = w_in.shape[0]
    alpha = (2 * depth) ** 0.25
    params = (w_in, gla_wa2_f, gla_ba_f, gla_wa2_b, gla_ba_b, gla_norm_g, w_gla_o, hy_conv_w, hy_conv_b,
              hy_w1, hy_b1, hy_w2, hy_b2, hy_w3, hy_b3, hy_w4, hy_b4, hy_freq, hy_skip, w_hy_o, w_out,
              ln1_g, ln1_b, w_ff1, w_ff2, ln2_g, ln2_b)
    h = x
    for l in range(depth):
        h = _layer(h, *(p[l] for p in params), alpha)
    return h
```

```python
import functools
import math

import jax
import jax.numpy as jnp
import numpy as np
from jax import lax
from jax.experimental import pallas as pl
from jax.experimental.pallas import tpu as pltpu

F32 = jnp.float32
BF16 = jnp.bfloat16

GLA_HEADS = 4
GLA_TAU = 16.0
GLA_CHUNK = 64
GLA_BLOCK = 256
HY_FAST_DECAY = 0.3
HY_SLOW_DECAY = 1.5
HY_DECAY_TARGET = 1e-2
LN_EPS = 1e-5

V7X_VMEM_LIMIT_BYTES = 56 * 1024 * 1024
V7X_LANES = 128

NN = (((1,), (0,)), ((), ()))
NT = (((1,), (1,)), ((), ()))
TN = (((0,), (0,)), ((), ()))


def _dot(a, b, dims=NN):
    return lax.dot_general(a.astype(BF16), b.astype(BF16), dims, preferred_element_type=F32)


def _split3(a):
    h1 = a.astype(BF16)
    r1 = a - h1.astype(F32)
    h2 = r1.astype(BF16)
    h3 = (r1 - h2.astype(F32)).astype(BF16)
    return h1, h2, h3


def _dot_f32(a, b, dims=NN):
    a1, a2, _ = _split3(a)
    b1, b2, _ = _split3(b)
    d = lambda x, y: lax.dot_general(x, y, dims, preferred_element_type=F32)
    return (d(a2, b1) + d(a1, b2)) + d(a1, b1)


def _dot_exact_lhs(a_bf16, b, dims=NN):
    b1, b2, b3 = _split3(b)
    d = lambda y: lax.dot_general(a_bf16, y, dims, preferred_element_type=F32)
    return (d(b3) + d(b2)) + d(b1)


def _call(body, *, grid, in_specs, out_specs, out_shape, name, scratch_shapes=()):
    return pl.pallas_call(
        body,
        grid=grid,
        in_specs=in_specs,
        out_specs=out_specs,
        out_shape=out_shape,
        scratch_shapes=scratch_shapes,
        compiler_params=pltpu.CompilerParams(
            dimension_semantics=("arbitrary",) * len(grid),
            vmem_limit_bytes=V7X_VMEM_LIMIT_BYTES,
        ),
        name=name,
    )


def _blk(n, want):
    b = min(n, want)
    while n % b:
        b //= 2
    return b


def _layernorm_rows(y, g, b):
    mu = jnp.mean(y, axis=-1, keepdims=True)
    d = y - mu
    var = jnp.mean(d * d, axis=-1, keepdims=True)
    return d * lax.rsqrt(var + LN_EPS) * g + b


def _mm_body(a_ref, b_ref, o_ref, *, act):
    acc = _dot(a_ref[...], b_ref[...])
    if act == "sigmoid":
        acc = jax.nn.sigmoid(acc)
    elif act == "relu2":
        acc = jnp.square(jnp.maximum(acc, 0.0))
    o_ref[...] = acc.astype(o_ref.dtype)


def _matmul(a, b, *, out_dtype, act=None, name, bm=1024, bn=1024):
    M, K = a.shape
    N = b.shape[1]
    bm, bn = _blk(M, bm), _blk(N, bn)
    return _call(
        functools.partial(_mm_body, act=act),
        grid=(M // bm, N // bn),
        in_specs=[pl.BlockSpec((bm, K), lambda i, j: (i, 0)),
                  pl.BlockSpec((K, bn), lambda i, j: (0, j))],
        out_specs=pl.BlockSpec((bm, bn), lambda i, j: (i, j)),
        out_shape=jax.ShapeDtypeStruct((M, N), out_dtype),
        name=name,
    )(a, b)


def _hy_proj_body(w_ref, x_ref, cp_ref, o_ref):
    u = _dot(w_ref[...], x_ref[...], NT)
    L = u.shape[1]
    t = lax.broadcasted_iota(jnp.int32, u.shape, 1)
    prev = jnp.where(t == 0, 0.0, pltpu.roll(u, 1, axis=1))
    nxt = jnp.where(t == L - 1, 0.0, pltpu.roll(u, L - 1, axis=1))
    cp = cp_ref[...]
    o_ref[...] = cp[:, 0:1] * prev + cp[:, 1:2] * u + cp[:, 2:3] * nxt + cp[:, 3:4]


def _hy_proj(w_t, xb, conv_params, B, L):
    C3, D = w_t.shape
    bc = _blk(C3, 512)
    nc = C3 // bc
    return _call(
        _hy_proj_body,
        grid=(B, nc),
        in_specs=[pl.BlockSpec((bc, D), lambda b, c: (c, 0)),
                  pl.BlockSpec((L, D), lambda b, c: (b, 0)),
                  pl.BlockSpec((bc, 4), lambda b, c: (c, 0))],
        out_specs=pl.BlockSpec((bc, L), lambda b, c: (b * nc + c, 0)),
        out_shape=jax.ShapeDtypeStruct((B * C3, L), F32),
        name="hy_proj_conv",
    )(w_t, xb, conv_params)


def _log_sigmoid(z):
    return -(jnp.maximum(-z, 0.0) + jnp.log1p(jnp.exp(-jnp.abs(z))))


def _decay_body(x_ref, wab_ref, w2f_ref, bf_ref, w2b_ref, bb_ref, cf_ref, cb_ref, xb_ref, *, block):
    x = x_ref[...]
    xb_ref[...] = x.astype(xb_ref.dtype)
    ab = _dot_f32(x, wab_ref[...])
    zf = _dot_f32(ab, w2f_ref[...]) + bf_ref[...]
    zb = _dot_f32(ab, w2b_ref[...]) + bb_ref[...]
    laf = _log_sigmoid(zf) * (1.0 / GLA_TAU)
    lab = _log_sigmoid(zb) * (1.0 / GLA_TAU)
    T = block
    row = lax.broadcasted_iota(jnp.int32, (T, T), 0)
    col = lax.broadcasted_iota(jnp.int32, (T, T), 1)
    tri_f = (col <= row).astype(BF16)
    tri_b = (col >= row).astype(BF16)
    for s in range(x.shape[0] // T):
        cf_ref[s * T:(s + 1) * T, :] = _dot_exact_lhs(tri_f, laf[s * T:(s + 1) * T, :])
        cb_ref[s * T:(s + 1) * T, :] = _dot_exact_lhs(tri_b, lab[s * T:(s + 1) * T, :])


def _gla_decays(x2d, w_ab, wa2_f, ba_f, wa2_b, ba_b, block):
    M, D = x2d.shape
    rank, KEY = wa2_f.shape
    P = V7X_LANES
    assert 2 * rank <= P
    w_ab = jnp.pad(w_ab, ((0, 0), (0, P - 2 * rank)))
    w2f = jnp.pad(wa2_f, ((0, P - rank), (0, 0)))
    w2b = jnp.pad(wa2_b, ((rank, P - 2 * rank), (0, 0)))
    bm = max(_blk(M, 512), block)
    full = lambda shape: pl.BlockSpec(shape, lambda i: (0, 0))
    out = jax.ShapeDtypeStruct((M, KEY), F32)
    return _call(
        functools.partial(_decay_body, block=block),
        grid=(M // bm,),
        in_specs=[pl.BlockSpec((bm, D), lambda i: (i, 0)), full((D, P)),
                  full((P, KEY)), full((1, KEY)), full((P, KEY)), full((1, KEY))],
        out_specs=[pl.BlockSpec((bm, KEY), lambda i: (i, 0))] * 2 + [pl.BlockSpec((bm, D), lambda i: (i, 0))],
        out_shape=[out, out, jax.ShapeDtypeStruct((M, D), BF16)],
        name="gla_decays",
    )(x2d, w_ab, w2f, ba_f.reshape(1, KEY), w2b, ba_b.reshape(1, KEY))


def _gla_body(q_ref, k_ref, v_ref, r_ref, cf_ref, cb_ref, g_ref, y_ref, o_acc,
              stf_ref, sf_ref, qbf_ref, kef_ref, klf_ref,
              stb_ref, sb_ref, qbb_ref, keb_ref, klb_ref, *, block, sub):
    L, DK = q_ref.shape
    T, C = block, sub
    n, ns = L // T, T // C
    scale = DK ** -0.5
    row = lax.broadcasted_iota(jnp.int32, (C, C), 0)
    col = lax.broadcasted_iota(jnp.int32, (C, C), 1)

    def boundary_row(cum_ref, start, pick):
        return cum_ref[pl.ds(pl.multiple_of(start, 8), 8), :][pick:pick + 1, :]

    def block_step(blk, cum_ref, st_ref, s_ref, qb_ref, ke_ref, kl_ref, forward, assign):
        r0 = pl.multiple_of(blk * T, T)
        tot = boundary_row(cum_ref, r0 + T - 8, 7) if forward else boundary_row(cum_ref, r0, 0)
        mask = (col <= row) if forward else (col > row)
        refs = {}
        for I in (range(ns) if forward else range(ns - 1, -1, -1)):
            sl = slice(I * C, (I + 1) * C)
            rows = pl.ds(r0 + I * C, C)
            if forward:
                ref = boundary_row(cum_ref, r0 + I * C - 8, 7) if I > 0 else jnp.zeros_like(tot)
            else:
                ref = boundary_row(cum_ref, r0 + (I + 1) * C, 0) if I < ns - 1 else jnp.zeros_like(tot)
            d = cum_ref[rows, :] - ref
            q_loc = q_ref[rows, :] * (scale * jnp.exp(d))
            k_loc = k_ref[rows, :] * jnp.exp(-d)
            qb_ref[sl, :] = (q_loc * jnp.exp(ref)).astype(BF16)
            ke_ref[sl, :] = (k_loc * jnp.exp(tot - ref)).astype(BF16)
            k_loc = k_loc.astype(BF16)
            kl_ref[sl, :] = k_loc
            s_ref[sl, sl] = jnp.where(mask, _dot(q_loc, k_loc, NT), 0.0).astype(BF16)
            for J, ref_j in refs.items():
                sj = slice(J * C, (J + 1) * C)
                s_ref[sl, sj] = _dot(q_loc * jnp.exp(ref - ref_j), kl_ref[sj, :], NT).astype(BF16)
            refs[I] = ref
        rows = pl.ds(r0, T)
        vb = v_ref[rows, :].astype(BF16)
        st = st_ref[...]
        o = _dot(s_ref[...], vb) + _dot(qb_ref[...], st, NT)
        if assign:
            o_acc[rows, :] = o
        else:
            o_acc[rows, :] += o
        st_ref[...] = st * jnp.exp(tot) + _dot(vb, ke_ref[...], TN)

    fwd = functools.partial(block_step, cum_ref=cf_ref, st_ref=stf_ref, s_ref=sf_ref, qb_ref=qbf_ref,
                            ke_ref=kef_ref, kl_ref=klf_ref, forward=True)
    bwd = functools.partial(block_step, cum_ref=cb_ref, st_ref=stb_ref, s_ref=sb_ref, qb_ref=qbb_ref,
                            ke_ref=keb_ref, kl_ref=klb_ref, forward=False)

    for ref in (stf_ref, stb_ref, sf_ref, sb_ref):
        ref[...] = jnp.zeros_like(ref)

    def sweep(lo, hi, step):
        def body(i, carry):
            step(i)
            return carry
        lax.fori_loop(lo, hi, body, 0)

    if n % 2 == 0:
        def first_half(i):
            fwd(i, assign=True)
            bwd(n - 1 - i, assign=True)

        def second_half(i):
            fwd(i, assign=False)
            bwd(n - 1 - i, assign=False)

        sweep(0, n // 2, first_half)
        sweep(n // 2, n, second_half)
    else:
        sweep(0, n, lambda i: fwd(i, assign=True))
        sweep(0, n, lambda i: bwd(n - 1 - i, assign=False))

    g = g_ref[...]

    def finish(c):
        rows = pl.ds(pl.multiple_of(c * C, C), C)
        o = o_acc[rows, :]
        o = o * lax.rsqrt(jnp.mean(o * o, axis=-1, keepdims=True) + LN_EPS) * g
        r = r_ref[rows, :]
        y_ref[rows, :] = (o * (r * jax.nn.sigmoid(r))).astype(y_ref.dtype)

    sweep(0, L // C, finish)


def _gla(p_qkvr, cum_f, cum_b, norm_g, B, L, KEY, VAL, block):
    H = GLA_HEADS
    DK, DV = KEY // H, VAL // H
    M = B * L
    kq = KEY // DK
    vq = 2 * KEY // DV
    rq = (2 * KEY + VAL) // DV
    T = block
    direction_scratch = [pltpu.VMEM((DV, DK), F32), pltpu.VMEM((T, T), BF16), pltpu.VMEM((T, DK), BF16),
                         pltpu.VMEM((T, DK), BF16), pltpu.VMEM((T, DK), BF16)]
    return _call(
        functools.partial(_gla_body, block=T, sub=_blk(T, GLA_CHUNK)),
        grid=(B, H),
        in_specs=[pl.BlockSpec((L, DK), lambda b, h: (b, h)),
                  pl.BlockSpec((L, DK), lambda b, h: (b, kq + h)),
                  pl.BlockSpec((L, DV), lambda b, h: (b, vq + h)),
                  pl.BlockSpec((L, DV), lambda b, h: (b, rq + h)),
                  pl.BlockSpec((L, DK), lambda b, h: (b, h)),
                  pl.BlockSpec((L, DK), lambda b, h: (b, h)),
                  pl.BlockSpec((1, DV), lambda b, h: (0, h))],
        out_specs=pl.BlockSpec((L, DV), lambda b, h: (b, h)),
        out_shape=jax.ShapeDtypeStruct((M, VAL), BF16),
        scratch_shapes=[pltpu.VMEM((L, DV), F32)] + direction_scratch * 2,
        name="gla_bidir",
    )(p_qkvr, p_qkvr, p_qkvr, p_qkvr, cum_f, cum_b, norm_g.reshape(1, VAL))


def _filter_body(emb_ref, w1_ref, b1_ref, w2_ref, b2_ref, w3_ref, b3_ref, fr_ref,
                 w4f_ref, b4f_ref, w4b_ref, b4b_ref, dl_ref, a_ref, bm_ref, ny_ref):
    emb = emb_ref[...]
    fr = fr_ref[...]
    h = jnp.sin(fr * (_dot_f32(w1_ref[...], emb) + b1_ref[...]))
    h = jnp.sin(fr * (_dot_f32(w2_ref[...], h) + b2_ref[...]))
    h = jnp.sin(fr * (_dot_f32(w3_ref[...], h) + b3_ref[...]))
    t_lin = emb[0:1, :]
    decay = jnp.exp(-t_lin * dl_ref[...])
    hf = (_dot_f32(w4f_ref[...], h) + b4f_ref[...]) * decay
    hb = (_dot_f32(w4b_ref[...], h) + b4b_ref[...]) * decay
    t = lax.broadcasted_iota(jnp.int32, hf.shape, 1)
    hb = jnp.where(t == 0, 0.0, hb)
    a = hf + hb
    a_ref[...] = a.astype(a_ref.dtype)
    bm_ref[...] = (hf - hb).astype(bm_ref.dtype)
    sign = jnp.where((t & 1) == 0, 1.0, -1.0)
    ny_ref[...] = jnp.sum(a * sign, axis=-1, keepdims=True)


def _hyena_filters(emb_t, w1, b1, w2, b2, w3, b3, w4, b4, freq, deltas, W):
    L = emb_t.shape[1]
    HID = w1.shape[1]
    n_ord = w4.shape[1] // (2 * W)
    EMB = -(-emb_t.shape[0] // V7X_LANES) * V7X_LANES
    w1 = jnp.pad(w1, ((0, EMB - w1.shape[0]), (0, 0)))
    emb_t = jnp.pad(emb_t, ((0, EMB - emb_t.shape[0]), (0, 0)))
    cb = _blk(W, 512)
    ncb = W // cb
    w4t = w4.T
    b4c = b4.reshape(-1, 1)
    colv = lambda v: v.reshape(-1, 1)
    full = lambda shape: pl.BlockSpec(shape, lambda o, c: (0, 0))
    rows = n_ord * W
    return _call(
        _filter_body,
        grid=(n_ord, ncb),
        in_specs=[full((EMB, L)), full((HID, EMB)), full((HID, 1)), full((HID, HID)), full((HID, 1)),
                  full((HID, HID)), full((HID, 1)), full((HID, 1)),
                  pl.BlockSpec((cb, HID), lambda o, c: (o * 2 * ncb + c, 0)),
                  pl.BlockSpec((cb, 1), lambda o, c: (o * 2 * ncb + c, 0)),
                  pl.BlockSpec((cb, HID), lambda o, c: (o * 2 * ncb + ncb + c, 0)),
                  pl.BlockSpec((cb, 1), lambda o, c: (o * 2 * ncb + ncb + c, 0)),
                  pl.BlockSpec((cb, 1), lambda o, c: (c, 0))],
        out_specs=[pl.BlockSpec((cb, L), lambda o, c: (o * ncb + c, 0)),
                   pl.BlockSpec((cb, L), lambda o, c: (o * ncb + c, 0)),
                   pl.BlockSpec((cb, 1), lambda o, c: (o * ncb + c, 0))],
        out_shape=[jax.ShapeDtypeStruct((rows, L), BF16), jax.ShapeDtypeStruct((rows, L), BF16),
                   jax.ShapeDtypeStruct((rows, 1), F32)],
        name="hy_filters",
    )(emb_t, w1.T, colv(b1), w2.T, colv(b2), w3.T, colv(b3), colv(freq), w4t, b4c, w4t, b4c, deltas)


def _spec_body(a_ref, f_ref, s_ref, o_ref):
    o_ref[...] = _dot(a_ref[...], f_ref[...]) * s_ref[...]


def _filter_spectrum(a, fmat, scale):
    R, L = a.shape
    bm, bn = _blk(R, 1024), _blk(L, 1024)
    return _call(
        _spec_body,
        grid=(R // bm, L // bn),
        in_specs=[pl.BlockSpec((bm, L), lambda i, j: (i, 0)),
                  pl.BlockSpec((L, bn), lambda i, j: (0, j)),
                  pl.BlockSpec((1, bn), lambda i, j: (0, j))],
        out_specs=pl.BlockSpec((bm, bn), lambda i, j: (i, j)),
        out_shape=jax.ShapeDtypeStruct((R, L), F32),
        name="hy_filter_spectrum",
    )(a, fmat, scale)


def _fwd_dft_body(z_ref, c_ref, s_ref, hr_ref, hi_ref, hn_ref, pr_ref, pi_ref):
    z = z_ref[...].astype(BF16)
    zr = _dot(z, c_ref[...])
    zi = _dot(z, s_ref[...])
    hr, hi = hr_ref[...], hi_ref[...]
    k = lax.broadcasted_iota(jnp.int32, zr.shape, 1) + pl.program_id(2) * zr.shape[1]
    dc = (k == 0)
    pr = zr * hr - jnp.where(dc, 0.0, zi * hi)
    pi = jnp.where(dc, zi * hn_ref[...], zr * hi + zi * hr)
    pr_ref[...] = pr.astype(pr_ref.dtype)
    pi_ref[...] = pi.astype(pi_ref.dtype)


def _fwd_dft(z2d, z_row0, z_rows_per_b, cmat, smat, hr, hi, hn, h_row0, B, W, L):
    bm, bn = _blk(W, 1024), _blk(L, 512)
    ncb = W // bm
    zb, z0, h0 = z_rows_per_b // bm, z_row0 // bm, h_row0 // bm
    out = jax.ShapeDtypeStruct((B * W, L), BF16)
    return _call(
        _fwd_dft_body,
        grid=(B, ncb, L // bn),
        in_specs=[pl.BlockSpec((bm, L), lambda b, c, j: (b * zb + z0 + c, 0)),
                  pl.BlockSpec((L, bn), lambda b, c, j: (0, j)),
                  pl.BlockSpec((L, bn), lambda b, c, j: (0, j)),
                  pl.BlockSpec((bm, bn), lambda b, c, j: (h0 + c, j)),
                  pl.BlockSpec((bm, bn), lambda b, c, j: (h0 + c, j)),
                  pl.BlockSpec((bm, 1), lambda b, c, j: (h0 + c, 0))],
        out_specs=[pl.BlockSpec((bm, bn), lambda b, c, j: (b * ncb + c, j))] * 2,
        out_shape=[out, out],
        name="hy_fwd_dft",
    )(z2d, cmat, smat, hr, hi, hn)


def _inv_dft_body(pr_ref, pi_ref, c_ref, st_ref, gate_ref, z_ref, skip_ref, o_ref):
    zc = _dot(pr_ref[...], c_ref[...]) + _dot(pi_ref[...], st_ref[...])
    o_ref[...] = (gate_ref[...] * (zc + skip_ref[...] * z_ref[...])).astype(o_ref.dtype)


def _inv_dft(pr, pi, cmat, smat_t, gates2d, g_row0, g_rows_per_b, z2d, z_row0, z_rows_per_b,
             skip_col, s_row0, B, W, L, out_dtype):
    bm, bn = _blk(W, 1024), _blk(L, 512)
    ncb = W // bm
    gb, g0 = g_rows_per_b // bm, g_row0 // bm
    zb, z0 = z_rows_per_b // bm, z_row0 // bm
    s0 = s_row0 // bm
    return _call(
        _inv_dft_body,
        grid=(B, ncb, L // bn),
        in_specs=[pl.BlockSpec((bm, L), lambda b, c, j: (b * ncb + c, 0)),
                  pl.BlockSpec((bm, L), lambda b, c, j: (b * ncb + c, 0)),
                  pl.BlockSpec((L, bn), lambda b, c, j: (0, j)),
                  pl.BlockSpec((L, bn), lambda b, c, j: (0, j)),
                  pl.BlockSpec((bm, bn), lambda b, c, j: (b * gb + g0 + c, j)),
                  pl.BlockSpec((bm, bn), lambda b, c, j: (b * zb + z0 + c, j)),
                  pl.BlockSpec((bm, 1), lambda b, c, j: (s0 + c, 0))],
        out_specs=pl.BlockSpec((bm, bn), lambda b, c, j: (b * ncb + c, j)),
        out_shape=jax.ShapeDtypeStruct((B * W, L), out_dtype),
        name="hy_inv_dft",
    )(pr, pi, cmat, smat_t, gates2d, z2d, skip_col)


def _gla_out_body(y_ref, w_ref, g_ref, o_ref):
    o_ref[...] = g_ref[...] * _dot(y_ref[...], w_ref[...])


def _gla_out(y_gla, w_o, gates, D):
    M, K = y_gla.shape
    bm, bn = _blk(M, 1024), _blk(D, 1024)
    return _call(
        _gla_out_body,
        grid=(M // bm, D // bn),
        in_specs=[pl.BlockSpec((bm, K), lambda i, j: (i, 0)),
                  pl.BlockSpec((K, bn), lambda i, j: (0, j)),
                  pl.BlockSpec((bm, bn), lambda i, j: (i, j))],
        out_specs=pl.BlockSpec((bm, bn), lambda i, j: (i, j)),
        out_shape=jax.ShapeDtypeStruct((M, D), F32),
        name="gla_out_proj",
    )(y_gla, w_o, gates)


def _hy_out_body(y_ref, w_ref, g_ref, t_ref, o_ref):
    o_ref[...] = (t_ref[...] + g_ref[...] * _dot(y_ref[...], w_ref[...], TN)).astype(o_ref.dtype)


def _hy_out_merge(y_hy_t, w_o, gates, t_gla, B, L, W, D):
    bt, bn = _blk(L, 512), _blk(D, 1024)
    nt = L // bt
    g1 = D // bn
    return _call(
        _hy_out_body,
        grid=(B, nt, D // bn),
        in_specs=[pl.BlockSpec((W, bt), lambda b, t, j: (b, t)),
                  pl.BlockSpec((W, bn), lambda b, t, j: (0, j)),
                  pl.BlockSpec((bt, bn), lambda b, t, j: (b * nt + t, g1 + j)),
                  pl.BlockSpec((bt, bn), lambda b, t, j: (b * nt + t, j))],
        out_specs=pl.BlockSpec((bt, bn), lambda b, t, j: (b * nt + t, j)),
        out_shape=jax.ShapeDtypeStruct((B * L, D), BF16),
        name="hy_out_proj_merge",
    )(y_hy_t, w_o, gates, t_gla)


def _proj_ln_body(a_ref, w_ref, x_ref, g_ref, b_ref, o_ref, ob_ref, *, alpha):
    y = alpha * x_ref[...] + _dot(a_ref[...], w_ref[...])
    h = _layernorm_rows(y, g_ref[...], b_ref[...])
    o_ref[...] = h
    ob_ref[...] = h.astype(ob_ref.dtype)


def _proj_ln(a, w, x2d, ln_g, ln_b, alpha):
    M, K = a.shape
    D = w.shape[1]
    bm = _blk(M, 512)
    row = lambda i: (i, 0)
    full = lambda shape: pl.BlockSpec(shape, lambda i: (0, 0))
    return _call(
        functools.partial(_proj_ln_body, alpha=alpha),
        grid=(M // bm,),
        in_specs=[pl.BlockSpec((bm, K), row), full((K, D)), pl.BlockSpec((bm, D), row),
                  full((1, D)), full((1, D))],
        out_specs=[pl.BlockSpec((bm, D), row)] * 2,
        out_shape=[jax.ShapeDtypeStruct((M, D), F32), jax.ShapeDtypeStruct((M, D), BF16)],
        name="out_proj_ln1",
    )(a, w, x2d, ln_g.reshape(1, D), ln_b.reshape(1, D))


def _ff2_ln_body(a_ref, w_ref, h_ref, g_ref, b_ref, o_ref, acc_ref, *, alpha):
    kk = pl.program_id(1)

    @pl.when(kk == 0)
    def _():
        acc_ref[...] = jnp.zeros_like(acc_ref)

    acc_ref[...] += _dot(a_ref[...], w_ref[...])

    @pl.when(kk == pl.num_programs(1) - 1)
    def _():
        y = alpha * h_ref[...] + acc_ref[...]
        o_ref[...] = _layernorm_rows(y, g_ref[...], b_ref[...])


def _ff2_ln(a, w, h, ln_g, ln_b, alpha):
    M, K = a.shape
    D = w.shape[1]
    bm, bk = _blk(M, 512), _blk(K, 1024)
    return _call(
        functools.partial(_ff2_ln_body, alpha=alpha),
        grid=(M // bm, K // bk),
        in_specs=[pl.BlockSpec((bm, bk), lambda i, k: (i, k)),
                  pl.BlockSpec((bk, D), lambda i, k: (k, 0)),
                  pl.BlockSpec((bm, D), lambda i, k: (i, 0)),
                  pl.BlockSpec((1, D), lambda i, k: (0, 0)),
                  pl.BlockSpec((1, D), lambda i, k: (0, 0))],
        out_specs=pl.BlockSpec((bm, D), lambda i, k: (i, 0)),
        out_shape=jax.ShapeDtypeStruct((M, D), F32),
        scratch_shapes=[pltpu.VMEM((bm, D), F32)],
        name="ff2_ln2",
    )(a, w, h, ln_g.reshape(1, D), ln_b.reshape(1, D))


def _dft_tables(L):
    LO = _blk(L, 32)
    k = jnp.arange(L, dtype=jnp.int32)
    ang = lambda m: ((m[:, None] * k[None, :]) % (2 * L)).astype(F32) * (math.pi / L)
    a_hi = ang(jnp.arange(L // LO, dtype=jnp.int32) * LO)[:, None, :]
    a_lo = ang(jnp.arange(LO, dtype=jnp.int32))[None, :, :]
    cos_t = (jnp.cos(a_hi) * jnp.cos(a_lo) - jnp.sin(a_hi) * jnp.sin(a_lo)).reshape(L, L)
    sin_t = (jnp.sin(a_hi) * jnp.cos(a_lo) + jnp.cos(a_hi) * jnp.sin(a_lo)).reshape(L, L)
    nyq = jnp.where(k % 2 == 0, 1.0, -1.0).astype(F32)
    smat = jnp.where(k[None, :] == 0, nyq[:, None], sin_t)
    smat_t = jnp.where(k[:, None] == 0, nyq[None, :], sin_t)
    return cos_t.astype(BF16), smat.astype(BF16), smat_t.astype(BF16)


def _position_features(L, emb_dim):
    t = jnp.linspace(0.0, 1.0, L, dtype=F32)[:, None]
    bands = (emb_dim - 1) // 2
    f = jnp.linspace(1e-4, bands - 1, bands, dtype=F32)
    wpos = 2.0 * math.pi * jnp.arange(L, dtype=F32) / L
    ang = wpos[:, None] * f[None, :]
    return jnp.concatenate([t, jnp.cos(ang), -jnp.sin(ang)], axis=-1).T


def _layer(h, w_in, gla_wa2_f, gla_ba_f, gla_wa2_b, gla_ba_b, gla_norm_g, w_gla_o,
           hy_conv_w, hy_conv_b, hy_w1, hy_b1, hy_w2, hy_b2, hy_w3, hy_b3, hy_w4, hy_b4,
           hy_freq, hy_skip, w_hy_o, w_out, ln1_g, ln1_b, w_ff1, w_ff2, ln2_g, ln2_b, alpha):
    B, L, D = h.shape
    M = B * L
    rank, KEY = gla_wa2_f.shape
    VAL = gla_norm_g.shape[0]
    n_ord, W = hy_skip.shape

    sizes = (KEY, KEY, VAL, VAL, rank, rank, (n_ord + 1) * W, 2 * D)
    offs = np.concatenate([[0], np.cumsum(sizes)])
    seg = lambda i, j: w_in[:, int(offs[i]):int(offs[j])]
    w_qkvr = seg(0, 4).astype(BF16)
    w_ab = seg(4, 6)
    w_hy_t = seg(6, 7).T.astype(BF16)
    w_gate = seg(7, 8).astype(BF16)

    x2d = h.reshape(M, D)
    gla_block = _blk(L, GLA_BLOCK)
    cum_f, cum_b, xb = _gla_decays(x2d, w_ab, gla_wa2_f, gla_ba_f, gla_wa2_b, gla_ba_b, gla_block)

    p_qkvr = _matmul(xb, w_qkvr, out_dtype=F32, name="in_proj_qkvr")
    gates = _matmul(xb, w_gate, out_dtype=F32, act="sigmoid", name="in_proj_gates")
    conv_params = jnp.concatenate([hy_conv_w.T, hy_conv_b[:, None]], axis=1)
    u_t = _hy_proj(w_hy_t, xb, conv_params, B, L)

    y_gla = _gla(p_qkvr, cum_f, cum_b, gla_norm_g, B, L, KEY, VAL, gla_block)

    emb_t = _position_features(L, hy_w1.shape[0])
    min_decay = math.log(HY_DECAY_TARGET) / HY_SLOW_DECAY
    max_decay = math.log(HY_DECAY_TARGET) / HY_FAST_DECAY
    deltas = jnp.abs(jnp.linspace(min_decay, max_decay, W, dtype=F32)).reshape(W, 1)
    a_even, a_odd, nyq = _hyena_filters(emb_t, hy_w1, hy_b1, hy_w2, hy_b2, hy_w3, hy_b3,
                                        hy_w4, hy_b4, hy_freq, deltas, W)
    cmat, smat, smat_t = _dft_tables(L)
    bin_w = jnp.full((1, L), 1.0 / L, F32).at[0, 0].set(0.5 / L)
    h_re = _filter_spectrum(a_even, cmat, bin_w)
    h_im = _filter_spectrum(a_odd, smat, bin_w)
    h_ny = nyq * (0.5 / L)

    skip_col = hy_skip.reshape(n_ord * W, 1)
    z, z_row0, z_rows = u_t, 0, (n_ord + 1) * W
    for o in range(n_ord):
        pr, pi = _fwd_dft(z, z_row0, z_rows, cmat, smat, h_re, h_im, h_ny, o * W, B, W, L)
        last = o == n_ord - 1
        z = _inv_dft(pr, pi, cmat, smat_t, u_t, (o + 1) * W, (n_ord + 1) * W, z, z_row0, z_rows,
                     skip_col, o * W, B, W, L, BF16 if last else F32)
        z_row0, z_rows = 0, W
    y_hy_t = z

    t_gla = _gla_out(y_gla, w_gla_o.astype(BF16), gates, D)
    merged = _hy_out_merge(y_hy_t, w_hy_o.astype(BF16), gates, t_gla, B, L, W, D)
    h1, h1b = _proj_ln(merged, w_out.astype(BF16), x2d, ln1_g, ln1_b, alpha)
    act = _matmul(h1b, w_ff1.astype(BF16), out_dtype=BF16, act="relu2", name="ff1_relu2")
    out = _ff2_ln(act, w_ff2.astype(BF16), h1, ln2_g, ln2_b, alpha)
    return out.reshape(B, L, D)


def kernel(x, w_in, gla_wa2_f, gla_ba_f, gla_wa2_b, gla_ba_b, gla_norm_g, w_gla_o, hy_conv_w, hy_conv_b, hy_w1, hy_b1, hy_w2, hy_b2, hy_w3, hy_b3, hy_w4, hy_b4, hy_freq, hy_skip, w_hy_o, w_out, ln1_g, ln1_b, w_ff1, w_ff2, ln2_g, ln2_b):
    depth = w_in.shape[0]
    alpha = (2 * depth) ** 0.25
    params = (w_in, gla_wa2_f, gla_ba_f, gla_wa2_b, gla_ba_b, gla_norm_g, w_gla_o, hy_conv_w, hy_conv_b,
              hy_w1, hy_b1, hy_w2, hy_b2, hy_w3, hy_b3, hy_w4, hy_b4, hy_freq, hy_skip, w_hy_o, w_out,
              ln1_g, ln1_b, w_ff1, w_ff2, ln2_g, ln2_b)
    h = x
    for l in range(depth):
        h = _layer(h, *(p[l] for p in params), alpha)
    return h
```

```python
import functools
import math

import jax
import jax.numpy as jnp
import numpy as np
from jax import lax
from jax.experimental import pallas as pl
from jax.experimental.pallas import tpu as pltpu

F32 = jnp.float32
BF16 = jnp.bfloat16

GLA_HEADS = 4
GLA_TAU = 16.0
GLA_CHUNK = 64
GLA_BLOCK = 256
HY_FAST_DECAY = 0.3
HY_SLOW_DECAY = 1.5
HY_DECAY_TARGET = 1e-2
LN_EPS = 1e-5

V7X_VMEM_LIMIT_BYTES = 56 * 1024 * 1024
V7X_LANES = 128

NN = (((1,), (0,)), ((), ()))
NT = (((1,), (1,)), ((), ()))
TN = (((0,), (0,)), ((), ()))
TNT = (((0,), (1,)), ((), ()))


def _dot(a, b, dims=NN):
    return lax.dot_general(a.astype(BF16), b.astype(BF16), dims, preferred_element_type=F32)


def _split2(a):
    h1 = a.astype(BF16)
    h2 = (a - h1.astype(F32)).astype(BF16)
    return h1, h2


def _dot_f32(a, b, dims=NN):
    a1, a2 = _split2(a)
    b1, b2 = _split2(b)
    d = lambda x, y: lax.dot_general(x, y, dims, preferred_element_type=F32)
    return (d(a2, b1) + d(a1, b2)) + d(a1, b1)


def _dot_exact_lhs(a_bf16, b, dims=NN):
    b1, b2 = _split2(b)
    d = lambda y: lax.dot_general(a_bf16, y, dims, preferred_element_type=F32)
    return d(b2) + d(b1)


def _call(body, *, grid, in_specs, out_specs, out_shape, name, scratch_shapes=()):
    return pl.pallas_call(
        body,
        grid=grid,
        in_specs=in_specs,
        out_specs=out_specs,
        out_shape=out_shape,
        scratch_shapes=scratch_shapes,
        compiler_params=pltpu.CompilerParams(
            dimension_semantics=("arbitrary",) * len(grid),
            vmem_limit_bytes=V7X_VMEM_LIMIT_BYTES,
        ),
        name=name,
    )


def _blk(n, want):
    b = min(n, want)
    while n % b:
        b //= 2
    return b


def _layernorm_rows(y, g, b):
    mu = jnp.mean(y, axis=-1, keepdims=True)
    d = y - mu
    var = jnp.mean(d * d, axis=-1, keepdims=True)
    return d * lax.rsqrt(var + LN_EPS) * g + b


def _mm_body(a_ref, b_ref, o_ref, *, act):
    acc = _dot(a_ref[...], b_ref[...])
    if act == "sigmoid":
        acc = jax.nn.sigmoid(acc)
    elif act == "relu2":
        acc = jnp.square(jnp.maximum(acc, 0.0))
    o_ref[...] = acc.astype(o_ref.dtype)


def _matmul(a, b, *, out_dtype, act=None, name, bm=1024, bn=1024):
    M, K = a.shape
    N = b.shape[1]
    bm, bn = _blk(M, bm), _blk(N, bn)
    return _call(
        functools.partial(_mm_body, act=act),
        grid=(M // bm, N // bn),
        in_specs=[pl.BlockSpec((bm, K), lambda i, j: (i, 0)),
                  pl.BlockSpec((K, bn), lambda i, j: (0, j))],
        out_specs=pl.BlockSpec((bm, bn), lambda i, j: (i, j)),
        out_shape=jax.ShapeDtypeStruct((M, N), out_dtype),
        name=name,
    )(a, b)


def _hy_proj_body(w_ref, x_ref, cp_ref, o_ref):
    u = _dot(w_ref[...], x_ref[...], TNT)
    L = u.shape[1]
    t = lax.broadcasted_iota(jnp.int32, u.shape, 1)
    prev = jnp.where(t == 0, 0.0, pltpu.roll(u, 1, axis=1))
    nxt = jnp.where(t == L - 1, 0.0, pltpu.roll(u, L - 1, axis=1))
    cp = cp_ref[...]
    o_ref[...] = (cp[:, 0:1] * prev + cp[:, 1:2] * u + cp[:, 2:3] * nxt + cp[:, 3:4]).astype(o_ref.dtype)


def _hy_proj(w, xb, conv_params, B, L):
    D, C3 = w.shape
    bc = _blk(C3, 512)
    nc = C3 // bc
    return _call(
        _hy_proj_body,
        grid=(B, nc),
        in_specs=[pl.BlockSpec((D, bc), lambda b, c: (0, c)),
                  pl.BlockSpec((L, D), lambda b, c: (b, 0)),
                  pl.BlockSpec((bc, 4), lambda b, c: (c, 0))],
        out_specs=pl.BlockSpec((bc, L), lambda b, c: (b * nc + c, 0)),
        out_shape=jax.ShapeDtypeStruct((B * C3, L), BF16),
        name="hy_proj_conv",
    )(w, xb, conv_params)


def _log_sigmoid(z):
    return -(jnp.maximum(-z, 0.0) + jnp.log1p(jnp.exp(-jnp.abs(z))))


def _decay_body(x_ref, wab_ref, w2f_ref, bf_ref, w2b_ref, bb_ref, cf_ref, cb_ref, xb_ref, *, block):
    xb = x_ref[...].astype(xb_ref.dtype)
    xb_ref[...] = xb
    ab = _dot(xb, wab_ref[...])
    zf = _dot_f32(ab, w2f_ref[...]) + bf_ref[...]
    zb = _dot_f32(ab, w2b_ref[...]) + bb_ref[...]
    laf = _log_sigmoid(zf) * (1.0 / GLA_TAU)
    lab = _log_sigmoid(zb) * (1.0 / GLA_TAU)
    T = block
    row = lax.broadcasted_iota(jnp.int32, (T, T), 0)
    col = lax.broadcasted_iota(jnp.int32, (T, T), 1)
    tri_f = (col <= row).astype(BF16)
    tri_b = (col >= row).astype(BF16)
    for s in range(xb.shape[0] // T):
        cf_ref[s * T:(s + 1) * T, :] = _dot_exact_lhs(tri_f, laf[s * T:(s + 1) * T, :])
        cb_ref[s * T:(s + 1) * T, :] = _dot_exact_lhs(tri_b, lab[s * T:(s + 1) * T, :])


def _gla_decays(x2d, w_ab, wa2_f, ba_f, wa2_b, ba_b, block):
    M, D = x2d.shape
    rank, KEY = wa2_f.shape
    P = V7X_LANES
    assert 2 * rank <= P
    w_ab = jnp.pad(w_ab, ((0, 0), (0, P - 2 * rank)))
    w2f = jnp.pad(wa2_f, ((0, P - rank), (0, 0)))
    w2b = jnp.pad(wa2_b, ((rank, P - 2 * rank), (0, 0)))
    bm = max(_blk(M, 512), block)
    full = lambda shape: pl.BlockSpec(shape, lambda i: (0, 0))
    out = jax.ShapeDtypeStruct((M, KEY), F32)
    return _call(
        functools.partial(_decay_body, block=block),
        grid=(M // bm,),
        in_specs=[pl.BlockSpec((bm, D), lambda i: (i, 0)), full((D, P)),
                  full((P, KEY)), full((1, KEY)), full((P, KEY)), full((1, KEY))],
        out_specs=[pl.BlockSpec((bm, KEY), lambda i: (i, 0))] * 2 + [pl.BlockSpec((bm, D), lambda i: (i, 0))],
        out_shape=[out, out, jax.ShapeDtypeStruct((M, D), BF16)],
        name="gla_decays",
    )(x2d, w_ab, w2f, ba_f.reshape(1, KEY), w2b, ba_b.reshape(1, KEY))


def _gla_body(q_ref, k_ref, v_ref, r_ref, cf_ref, cb_ref, g_ref, y_ref, o_acc,
              stf_ref, sf_ref, qbf_ref, kef_ref, klf_ref,
              stb_ref, sb_ref, qbb_ref, keb_ref, klb_ref, *, block, sub):
    L, DK = q_ref.shape
    T, C = block, sub
    n, ns = L // T, T // C
    scale = DK ** -0.5
    row = lax.broadcasted_iota(jnp.int32, (C, C), 0)
    col = lax.broadcasted_iota(jnp.int32, (C, C), 1)

    def boundary_row(cum_ref, start, pick):
        return cum_ref[pl.ds(pl.multiple_of(start, 8), 8), :][pick:pick + 1, :]

    def block_step(blk, cum_ref, st_ref, s_ref, qb_ref, ke_ref, kl_ref, forward, assign):
        r0 = pl.multiple_of(blk * T, T)
        tot = boundary_row(cum_ref, r0 + T - 8, 7) if forward else boundary_row(cum_ref, r0, 0)
        mask = (col <= row) if forward else (col > row)
        refs = {}
        for I in (range(ns) if forward else range(ns - 1, -1, -1)):
            sl = slice(I * C, (I + 1) * C)
            rows = pl.ds(r0 + I * C, C)
            if forward:
                ref = boundary_row(cum_ref, r0 + I * C - 8, 7) if I > 0 else jnp.zeros_like(tot)
            else:
                ref = boundary_row(cum_ref, r0 + (I + 1) * C, 0) if I < ns - 1 else jnp.zeros_like(tot)
            d = cum_ref[rows, :] - ref
            q_loc = q_ref[rows, :] * (scale * jnp.exp(d))
            k_loc = k_ref[rows, :] * jnp.exp(-d)
            qb_ref[sl, :] = (q_loc * jnp.exp(ref)).astype(BF16)
            ke_ref[sl, :] = (k_loc * jnp.exp(tot - ref)).astype(BF16)
            k_loc = k_loc.astype(BF16)
            kl_ref[sl, :] = k_loc
            s_ref[sl, sl] = jnp.where(mask, _dot(q_loc, k_loc, NT), 0.0).astype(BF16)
            for J, ref_j in refs.items():
                sj = slice(J * C, (J + 1) * C)
                s_ref[sl, sj] = _dot(q_loc * jnp.exp(ref - ref_j), kl_ref[sj, :], NT).astype(BF16)
            refs[I] = ref
        rows = pl.ds(r0, T)
        vb = v_ref[rows, :].astype(BF16)
        st = st_ref[...]
        o = _dot(s_ref[...], vb) + _dot(qb_ref[...], st, NT)
        if assign:
            o_acc[rows, :] = o
        else:
            o_acc[rows, :] += o
        st_ref[...] = st * jnp.exp(tot) + _dot(vb, ke_ref[...], TN)

    fwd = functools.partial(block_step, cum_ref=cf_ref, st_ref=stf_ref, s_ref=sf_ref, qb_ref=qbf_ref,
                            ke_ref=kef_ref, kl_ref=klf_ref, forward=True)
    bwd = functools.partial(block_step, cum_ref=cb_ref, st_ref=stb_ref, s_ref=sb_ref, qb_ref=qbb_ref,
                            ke_ref=keb_ref, kl_ref=klb_ref, forward=False)

    for ref in (stf_ref, stb_ref, sf_ref, sb_ref):
        ref[...] = jnp.zeros_like(ref)

    def sweep(lo, hi, step):
        def body(i, carry):
            step(i)
            return carry
        lax.fori_loop(lo, hi, body, 0)

    if n % 2 == 0:
        def first_half(i):
            fwd(i, assign=True)
            bwd(n - 1 - i, assign=True)

        def second_half(i):
            fwd(i, assign=False)
            bwd(n - 1 - i, assign=False)

        sweep(0, n // 2, first_half)
        sweep(n // 2, n, second_half)
    else:
        sweep(0, n, lambda i: fwd(i, assign=True))
        sweep(0, n, lambda i: bwd(n - 1 - i, assign=False))

    g = g_ref[...]

    def finish(c):
        rows = pl.ds(pl.multiple_of(c * C, C), C)
        o = o_acc[rows, :]
        o = o * lax.rsqrt(jnp.mean(o * o, axis=-1, keepdims=True) + LN_EPS) * g
        r = r_ref[rows, :].astype(F32)
        y_ref[rows, :] = (o *(r * jax.nn.sigmoid(r))).astype(y_ref.dtype)

    sweep(0, L // C, finish)


def _gla(p_qkvr, cum_f, cum_b, norm_g, B, L, KEY, VAL, block):
    H = GLA_HEADS
    DK, DV = KEY // H, VAL // H
    M = B * L
    kq = KEY // DK
    vq = 2 * KEY // DV
    rq = (2 * KEY + VAL) // DV
    T = block
    direction_scratch = [pltpu.VMEM((DV, DK), F32), pltpu.VMEM((T, T), BF16), pltpu.VMEM((T, DK), BF16),
                         pltpu.VMEM((T, DK), BF16), pltpu.VMEM((T, DK), BF16)]
    return _call(
        functools.partial(_gla_body, block=T, sub=_blk(T, GLA_CHUNK)),
        grid=(B, H),
        in_specs=[pl.BlockSpec((L, DK), lambda b, h: (b, h)),
                  pl.BlockSpec((L, DK), lambda b, h: (b, kq + h)),
                  pl.BlockSpec((L, DV), lambda b, h: (b, vq + h)),
                  pl.BlockSpec((L, DV), lambda b, h: (b, rq + h)),
                  pl.BlockSpec((L, DK), lambda b, h: (b, h)),
                  pl.BlockSpec((L, DK), lambda b, h: (b, h)),
                  pl.BlockSpec((1, DV), lambda b, h: (0, h))],
        out_specs=pl.BlockSpec((L, DV), lambda b, h: (b, h)),
        out_shape=jax.ShapeDtypeStruct((M, VAL), BF16),
        scratch_shapes=[pltpu.VMEM((L, DV), F32)] + direction_scratch * 2,
        name="gla_bidir",
    )(p_qkvr, p_qkvr, p_qkvr, p_qkvr, cum_f, cum_b, norm_g.reshape(1, VAL))


def _filter_body(emb_ref, w1_ref, b1_ref, w2_ref, b2_ref, w3_ref, b3_ref, fr_ref,
                 w4f_ref, b4f_ref, w4b_ref, b4b_ref, dl_ref, a_ref, bm_ref, ny_ref):
    emb = emb_ref[...]
    fr = fr_ref[...]
    h = jnp.sin(fr * (_dot_f32(w1_ref[...], emb) + b1_ref[...]))
    h = jnp.sin(fr * (_dot_f32(w2_ref[...], h) + b2_ref[...]))
    h = jnp.sin(fr * (_dot_f32(w3_ref[...], h) + b3_ref[...]))
    t_lin = emb[0:1, :]
    decay = jnp.exp(-t_lin * dl_ref[...])
    hf = (_dot_f32(w4f_ref[...], h) + b4f_ref[...]) * decay
    hb = (_dot_f32(w4b_ref[...], h) + b4b_ref[...]) * decay
    t = lax.broadcasted_iota(jnp.int32, hf.shape, 1)
    hb = jnp.where(t == 0, 0.0, hb)
    a = hf + hb
    a_ref[...] = a.astype(a_ref.dtype)
    bm_ref[...] = (hf - hb).astype(bm_ref.dtype)
    sign = jnp.where((t & 1) == 0, 1.0, -1.0)
    ny_ref[...] = jnp.sum(a * sign, axis=-1, keepdims=True)


def _hyena_filters(emb_t, w1, b1, w2, b2, w3, b3, w4, b4, freq, deltas, W):
    L = emb_t.shape[1]
    HID = w1.shape[1]
    n_ord = w4.shape[1] // (2 * W)
    EMB = -(-emb_t.shape[0] // V7X_LANES) * V7X_LANES
    w1 = jnp.pad(w1, ((0, EMB - w1.shape[0]), (0, 0)))
    emb_t = jnp.pad(emb_t, ((0, EMB - emb_t.shape[0]), (0, 0)))
    cb = _blk(W, 512)
    ncb = W // cb
    w4t = w4.T
    b4c = b4.reshape(-1, 1)
    colv = lambda v: v.reshape(-1, 1)
    full = lambda shape: pl.BlockSpec(shape, lambda o, c: (0, 0))
    rows = n_ord * W
    return _call(
        _filter_body,
        grid=(n_ord, ncb),
        in_specs=[full((EMB, L)), full((HID, EMB)), full((HID, 1)), full((HID, HID)), full((HID, 1)),
                  full((HID, HID)), full((HID, 1)), full((HID, 1)),
                  pl.BlockSpec((cb, HID), lambda o, c: (o * 2 * ncb + c, 0)),
                  pl.BlockSpec((cb, 1), lambda o, c: (o * 2 * ncb + c, 0)),
                  pl.BlockSpec((cb, HID), lambda o, c: (o * 2 * ncb + ncb + c, 0)),
                  pl.BlockSpec((cb, 1), lambda o, c: (o * 2 * ncb + ncb + c, 0)),
                  pl.BlockSpec((cb, 1), lambda o, c: (c, 0))],
        out_specs=[pl.BlockSpec((cb, L), lambda o, c: (o * ncb + c, 0)),
                   pl.BlockSpec((cb, L), lambda o, c: (o * ncb + c, 0)),
                   pl.BlockSpec((cb, 1), lambda o, c: (o * ncb + c, 0))],
        out_shape=[jax.ShapeDtypeStruct((rows, L), BF16), jax.ShapeDtypeStruct((rows, L), BF16),
                   jax.ShapeDtypeStruct((rows, 1), F32)],
        name="hy_filters",
    )(emb_t, w1.T, colv(b1), w2.T, colv(b2), w3.T, colv(b3), colv(freq), w4t, b4c, w4t, b4c, deltas)


def _spec_body(a_ref, f_ref, s_ref, o_ref):
    o_ref[...] = _dot(a_ref[...], f_ref[...]) * s_ref[...]


def _filter_spectrum(a, fmat, scale):
    R, L = a.shape
    bm, bn = _blk(R, 1024), _blk(L, 1024)
    return _call(
        _spec_body,
        grid=(R // bm, L // bn),
        in_specs=[pl.BlockSpec((bm, L), lambda i, j: (i, 0)),
                  pl.BlockSpec((L, bn), lambda i, j: (0, j)),
                  pl.BlockSpec((1, bn), lambda i, j: (0, j))],
        out_specs=pl.BlockSpec((bm, bn), lambda i, j: (i, j)),
        out_shape=jax.ShapeDtypeStruct((R, L), F32),
        name="hy_filter_spectrum",
    )(a, fmat, scale)


def _fwd_dft_body(z_ref, c_ref, s_ref, hr_ref, hi_ref, hn_ref, pr_ref, pi_ref):
    z = z_ref[...]
    zr = _dot(z, c_ref[...])
    zi = _dot(z, s_ref[...])
    hr, hi = hr_ref[...], hi_ref[...]
    k = lax.broadcasted_iota(jnp.int32, zr.shape, 1) + pl.program_id(1) * zr.shape[1]
    dc = (k == 0)
    pr = zr * hr - jnp.where(dc, 0.0, zi * hi)
    pi = jnp.where(dc, zi * hn_ref[...], zr * hi + zi * hr)
    pr_ref[...] = pr.astype(pr_ref.dtype)
    pi_ref[...] = pi.astype(pi_ref.dtype)


def _fwd_dft(z2d, z_row0, z_rows_per_b, cmat, smat, hr, hi, hn, h_row0, B, W, L):
    bm, bn = _blk(W, 1024), _blk(L, 512)
    ncb = W // bm
    zb, z0, h0 = z_rows_per_b // bm, z_row0 // bm, h_row0 // bm
    out = jax.ShapeDtypeStruct((B * W, L), BF16)
    return _call(
        _fwd_dft_body,
        grid=(ncb, L // bn, B),
        in_specs=[pl.BlockSpec((bm, L), lambda c, j, b: (b * zb + z0 + c, 0)),
                  pl.BlockSpec((L, bn), lambda c, j, b: (0, j)),
                  pl.BlockSpec((L, bn), lambda c, j, b: (0, j)),
                  pl.BlockSpec((bm, bn), lambda c, j, b: (h0 + c, j)),
                  pl.BlockSpec((bm, bn), lambda c, j, b: (h0 + c, j)),
                  pl.BlockSpec((bm, 1), lambda c, j, b: (h0 + c, 0))],
        out_specs=[pl.BlockSpec((bm, bn), lambda c, j, b: (b * ncb + c, j))] * 2,
        out_shape=[out, out],
        name="hy_fwd_dft",
    )(z2d, cmat, smat, hr, hi, hn)


def _inv_dft_body(pr_ref, pi_ref, c_ref, st_ref, gate_ref, z_ref, skip_ref, o_ref):
    zc = _dot(pr_ref[...], c_ref[...]) + _dot(pi_ref[...], st_ref[...])
    o_ref[...] = (gate_ref[...] * (zc + skip_ref[...] * z_ref[...])).astype(o_ref.dtype)


def _inv_dft(pr, pi, cmat, smat_t, gates2d, g_row0, g_rows_per_b, z2d, z_row0, z_rows_per_b,
             skip_col, s_row0, B, W, L, out_dtype):
    bm, bn = _blk(W, 1024), _blk(L, 1024)
    ncb = W // bm
    gb, g0 = g_rows_per_b // bm, g_row0 // bm
    zb, z0 = z_rows_per_b // bm, z_row0 // bm
    s0 = s_row0 // bm
    return _call(
        _inv_dft_body,
        grid=(B, ncb, L // bn),
        in_specs=[pl.BlockSpec((bm, L), lambda b, c, j: (b * ncb + c, 0)),
                  pl.BlockSpec((bm, L), lambda b, c, j: (b * ncb + c, 0)),
                  pl.BlockSpec((L, bn), lambda b, c, j: (0, j)),
                  pl.BlockSpec((L, bn), lambda b, c, j: (0, j)),
                  pl.BlockSpec((bm, bn), lambda b, c, j: (b * gb + g0 + c, j)),
                  pl.BlockSpec((bm, bn), lambda b, c, j: (b * zb + z0 + c, j)),
                  pl.BlockSpec((bm, 1), lambda b, c, j: (s0 + c, 0))],
        out_specs=pl.BlockSpec((bm, bn), lambda b, c, j: (b * ncb + c, j)),
        out_shape=jax.ShapeDtypeStruct((B * W, L), out_dtype),
        name="hy_inv_dft",
    )(pr, pi, cmat, smat_t, gates2d, z2d, skip_col)


def _merge_body(yg_ref, yh_ref, wg_ref, wh_ref, g0_ref, g1_ref, o_ref):
    tg = _dot(yg_ref[...], wg_ref[...])
    th = _dot(yh_ref[...], wh_ref[...], TN)
    o_ref[...] = (g0_ref[...] * tg + g1_ref[...] * th).astype(o_ref.dtype)


def _branch_merge(y_gla, y_hy_t, w_gla_o, w_hy_o, gates, B, L, D):
    VAL, W = y_gla.shape[1], w_hy_o.shape[0]
    bt = _blk(L, 512)
    nt = L // bt
    resident = lambda shape: pl.BlockSpec(shape, lambda b, t: (0, 0), pipeline_mode=pl.Buffered(1))
    return _call(
        _merge_body,
        grid=(B, nt),
        in_specs=[pl.BlockSpec((bt, VAL), lambda b, t: (b * nt + t, 0)),
                  pl.BlockSpec((W, bt), lambda b, t: (b, t)),
                  resident((VAL, D)), resident((W, D)),
                  pl.BlockSpec((bt, D), lambda b, t: (b * nt + t, 0)),
                  pl.BlockSpec((bt, D), lambda b, t: (b * nt + t, 1))],
        out_specs=pl.BlockSpec((bt, D), lambda b, t: (b * nt + t, 0)),
        out_shape=jax.ShapeDtypeStruct((B * L, D), BF16),
        name="branch_merge",
    )(y_gla, y_hy_t, w_gla_o, w_hy_o, gates, gates)


def _proj_ln_body(a_ref, w_ref, x_ref, g_ref, b_ref, o_ref, ob_ref, *, alpha):
    y = alpha * x_ref[...] + _dot(a_ref[...], w_ref[...])
    h = _layernorm_rows(y, g_ref[...], b_ref[...])
    o_ref[...] = h
    ob_ref[...] = h.astype(ob_ref.dtype)


def _proj_ln(a, w, x2d, ln_g, ln_b, alpha):
    M, K = a.shape
    D = w.shape[1]
    bm = _blk(M, 512)
    row = lambda i: (i, 0)
    full = lambda shape: pl.BlockSpec(shape, lambda i: (0, 0))
    return _call(
        functools.partial(_proj_ln_body, alpha=alpha),
        grid=(M // bm,),
        in_specs=[pl.BlockSpec((bm, K), row), full((K, D)), pl.BlockSpec((bm, D), row),
                  full((1, D)), full((1, D))],
        out_specs=[pl.BlockSpec((bm, D), row)] * 2,
        out_shape=[jax.ShapeDtypeStruct((M, D), F32), jax.ShapeDtypeStruct((M, D), BF16)],
        name="out_proj_ln1",
    )(a, w, x2d, ln_g.reshape(1, D), ln_b.reshape(1, D))


def _ff2_ln_body(a_ref, w_ref, h_ref, g_ref, b_ref, o_ref, *, alpha, ln_rows):
    kk = pl.program_id(1)
    part = _dot(a_ref[...], w_ref[...])

    @pl.when(kk == 0)
    def _():
        o_ref[...] = part

    @pl.when(kk > 0)
    def _():
        o_ref[...] += part

    @pl.when(kk == pl.num_programs(1) - 1)
    def _():
        g, b = g_ref[...], b_ref[...]

        def norm_rows(r, carry):
            rows = pl.ds(pl.multiple_of(r * ln_rows, ln_rows), ln_rows)
            y = alpha * h_ref[rows, :] + o_ref[rows, :]
            o_ref[rows, :] = _layernorm_rows(y, g, b)
            return carry

        lax.fori_loop(0, o_ref.shape[0] // ln_rows, norm_rows, 0)


def _ff2_ln(a, w, h, ln_g, ln_b, alpha):
    M, K = a.shape
    D = w.shape[1]
    bm, bk = _blk(M, 1024), _blk(K, 1024)
    return _call(
        functools.partial(_ff2_ln_body, alpha=alpha, ln_rows=_blk(bm, 128)),
        grid=(M // bm, K // bk),
        in_specs=[pl.BlockSpec((bm, bk), lambda i, k: (i, k)),
                  pl.BlockSpec((bk, D), lambda i, k: (k, 0)),
                  pl.BlockSpec((bm, D), lambda i, k: (i, 0)),
                  pl.BlockSpec((1, D), lambda i, k: (0, 0)),
                  pl.BlockSpec((1, D), lambda i, k: (0, 0))],
        out_specs=pl.BlockSpec((bm, D), lambda i, k: (i, 0)),
        out_shape=jax.ShapeDtypeStruct((M, D), F32),
        name="ff2_ln2",
    )(a, w, h, ln_g.reshape(1, D), ln_b.reshape(1, D))


def _dft_tables(L):
    LO = _blk(L, 32)
    k = jnp.arange(L, dtype=jnp.int32)
    ang = lambda m: ((m[:, None] * k[None, :]) % (2 * L)).astype(F32) * (math.pi / L)
    a_hi = ang(jnp.arange(L // LO, dtype=jnp.int32) * LO)[:, None, :]
    a_lo = ang(jnp.arange(LO, dtype=jnp.int32))[None, :, :]
    cos_t = (jnp.cos(a_hi) * jnp.cos(a_lo) - jnp.sin(a_hi) * jnp.sin(a_lo)).reshape(L, L)
    sin_t = (jnp.sin(a_hi) * jnp.cos(a_lo) + jnp.cos(a_hi) * jnp.sin(a_lo)).reshape(L, L)
    nyq = jnp.where(k % 2 == 0, 1.0, -1.0).astype(F32)
    smat = jnp.where(k[None, :] == 0, nyq[:, None], sin_t)
    smat_t = jnp.where(k[:, None] == 0, nyq[None, :], sin_t)
    return cos_t.astype(BF16), smat.astype(BF16), smat_t.astype(BF16)


def _position_features(L, emb_dim):
    t = jnp.linspace(0.0, 1.0, L, dtype=F32)[:, None]
    bands = (emb_dim - 1) // 2
    f = jnp.linspace(1e-4, bands - 1, bands, dtype=F32)
    wpos = 2.0 * math.pi * jnp.arange(L, dtype=F32) / L
    ang = wpos[:, None] * f[None, :]
    return jnp.concatenate([t, jnp.cos(ang), -jnp.sin(ang)], axis=-1).T


def _layer(h, w_in, gla_wa2_f, gla_ba_f, gla_wa2_b, gla_ba_b, gla_norm_g, w_gla_o,
           hy_conv_w, hy_conv_b, hy_w1, hy_b1, hy_w2, hy_b2, hy_w3, hy_b3, hy_w4, hy_b4,
           hy_freq, hy_skip, w_hy_o, w_out, ln1_g, ln1_b, w_ff1, w_ff2, ln2_g, ln2_b, alpha):
    B, L, D = h.shape
    M = B * L
    rank, KEY = gla_wa2_f.shape
    VAL = gla_norm_g.shape[0]
    n_ord, W = hy_skip.shape

    sizes = (KEY, KEY, VAL, VAL, rank, rank, (n_ord + 1) * W, 2 * D)
    offs = np.concatenate([[0], np.cumsum(sizes)])
    seg = lambda i, j: w_in[:, int(offs[i]):int(offs[j])]
    w_qkvr = seg(0, 4).astype(BF16)
    w_ab = seg(4, 6)
    w_hy = seg(6, 7).astype(BF16)
    w_gate = seg(7, 8).astype(BF16)

    x2d = h.reshape(M, D)
    gla_block = _blk(L, GLA_BLOCK)
    cum_f, cum_b, xb = _gla_decays(x2d, w_ab, gla_wa2_f, gla_ba_f, gla_wa2_b, gla_ba_b, gla_block)

    p_qkvr = _matmul(xb, w_qkvr, out_dtype=BF16, name="in_proj_qkvr", bm=2048)
    gates = _matmul(xb, w_gate, out_dtype=BF16, act="sigmoid", name="in_proj_gates", bm=2048)
    conv_params = jnp.concatenate([hy_conv_w.T, hy_conv_b[:, None]], axis=1)
    u_t = _hy_proj(w_hy, xb, conv_params, B, L)

    y_gla = _gla(p_qkvr, cum_f, cum_b, gla_norm_g, B, L, KEY, VAL, gla_block)

    emb_t = _position_features(L, hy_w1.shape[0])
    min_decay = math.log(HY_DECAY_TARGET) / HY_SLOW_DECAY
    max_decay = math.log(HY_DECAY_TARGET) / HY_FAST_DECAY
    deltas = jnp.abs(jnp.linspace(min_decay, max_decay, W, dtype=F32)).reshape(W, 1)
    a_even, a_odd, nyq = _hyena_filters(emb_t, hy_w1, hy_b1, hy_w2, hy_b2, hy_w3, hy_b3,
                                        hy_w4, hy_b4, hy_freq, deltas, W)
    cmat, smat, smat_t = _dft_tables(L)
    bin_w = jnp.full((1, L), 1.0 / L, F32).at[0, 0].set(0.5 / L)
    h_re = _filter_spectrum(a_even, cmat, bin_w)
    h_im = _filter_spectrum(a_odd, smat, bin_w)
    h_ny = nyq * (0.5 / L)

    skip_col = hy_skip.reshape(n_ord * W, 1)
    z, z_row0, z_rows = u_t, 0, (n_ord + 1) * W
    for o in range(n_ord):
        pr, pi = _fwd_dft(z, z_row0, z_rows, cmat, smat, h_re, h_im, h_ny, o * W, B, W, L)
        z = _inv_dft(pr, pi, cmat, smat_t, u_t, (o + 1) * W, (n_ord + 1) * W, z, z_row0, z_rows,
                     skip_col, o * W, B, W, L, BF16)
        z_row0, z_rows = 0, W
    y_hy_t = z

    merged = _branch_merge(y_gla, y_hy_t, w_gla_o.astype(BF16), w_hy_o.astype(BF16), gates, B, L, D)
    h1, h1b = _proj_ln(merged, w_out.astype(BF16), x2d, ln1_g, ln1_b, alpha)
    act = _matmul(h1b, w_ff1.astype(BF16), out_dtype=BF16, act="relu2", name="ff1_relu2")
    out = _ff2_ln(act, w_ff2.astype(BF16), h1, ln2_g, ln2_b, alpha)
    return out.reshape(B, L, D)


def kernel(x, w_in, gla_wa2_f, gla_ba_f, gla_wa2_b, gla_ba_b, gla_norm_g, w_gla_o, hy_conv_w, hy_conv_b, hy_w1, hy_b1, hy_w2, hy_b2, hy_w3, hy_b3, hy_w4, hy_b4, hy_freq, hy_skip, w_hy_o, w_out, ln1_g, ln1_b, w_ff1, w_ff2, ln2_g, ln2_b):
    depth = w_in.shape[0]
    alpha = (2 * depth) ** 0.25
    params = (w_in, gla_wa2_f, gla_ba_f, gla_wa2_b, gla_ba_b, gla_norm_g, w_gla_o, hy_conv_w, hy_conv_b,
              hy_w1, hy_b1, hy_w2, hy_b2, hy_w3, hy_b3, hy_w4, hy_b4, hy_freq, hy_skip, w_hy_o, w_out,
              ln1_g, ln1_b, w_ff1, w_ff2, ln2_g, ln2_b)
    h = x
    for l in range(depth):
        h = _layer(h, *(p[l] for p in params), alpha)
    return h
```

```python
import functools
import math

import jax
import jax.numpy as jnp
import numpy as np
from jax import lax
from jax.experimental import pallas as pl
from jax.experimental.pallas import tpu as pltpu

F32 = jnp.float32
BF16 = jnp.bfloat16

GLA_HEADS = 4
GLA_TAU = 16.0
GLA_CHUNK = 64
GLA_BLOCK = 256
HY_FAST_DECAY = 0.3
HY_SLOW_DECAY = 1.5
HY_DECAY_TARGET = 1e-2
LN_EPS = 1e-5

V7X_VMEM_LIMIT_BYTES = 56 * 1024 * 1024
V7X_LANES = 128

NN = (((1,), (0,)), ((), ()))
NT = (((1,), (1,)), ((), ()))
TN = (((0,), (0,)), ((), ()))
TNT = (((0,), (1,)), ((), ()))


def _dot(a, b, dims=NN):
    return lax.dot_general(a.astype(BF16), b.astype(BF16), dims, preferred_element_type=F32)


def _split2(a):
    h1 = a.astype(BF16)
    h2 = (a - h1.astype(F32)).astype(BF16)
    return h1, h2


def _dot_f32(a, b, dims=NN):
    a1, a2 = _split2(a)
    b1, b2 = _split2(b)
    d = lambda x, y: lax.dot_general(x, y, dims, preferred_element_type=F32)
    return (d(a2, b1) + d(a1, b2)) + d(a1, b1)


def _dot_exact_lhs(a_bf16, b, dims=NN):
    b1, b2 = _split2(b)
    d = lambda y: lax.dot_general(a_bf16, y, dims, preferred_element_type=F32)
    return d(b2) + d(b1)


def _call(body, *, grid, in_specs, out_specs, out_shape, name, scratch_shapes=()):
    return pl.pallas_call(
        body,
        grid=grid,
        in_specs=in_specs,
        out_specs=out_specs,
        out_shape=out_shape,
        scratch_shapes=scratch_shapes,
        compiler_params=pltpu.CompilerParams(
            dimension_semantics=("arbitrary",) * len(grid),
            vmem_limit_bytes=V7X_VMEM_LIMIT_BYTES,
        ),
        name=name,
    )


def _blk(n, want):
    b = min(n, want)
    while n % b:
        b //= 2
    return b


def _layernorm_rows(y, g, b):
    mu = jnp.mean(y, axis=-1, keepdims=True)
    d = y - mu
    var = jnp.mean(d * d, axis=-1, keepdims=True)
    return d * lax.rsqrt(var + LN_EPS) * g + b


def _mm_body(a_ref, b_ref, o_ref, *, act):
    acc = _dot(a_ref[...], b_ref[...])
    if act == "sigmoid":
        acc = jax.nn.sigmoid(acc)
    elif act == "relu2":
        acc = jnp.square(jnp.maximum(acc, 0.0))
    o_ref[...] = acc.astype(o_ref.dtype)


def _matmul(a, b, *, out_dtype, act=None, name, bm=1024, bn=1024):
    M, K = a.shape
    N = b.shape[1]
    bm, bn = _blk(M, bm), _blk(N, bn)
    return _call(
        functools.partial(_mm_body, act=act),
        grid=(M // bm, N // bn),
        in_specs=[pl.BlockSpec((bm, K), lambda i, j: (i, 0)),
                  pl.BlockSpec((K, bn), lambda i, j: (0, j))],
        out_specs=pl.BlockSpec((bm, bn), lambda i, j: (i, j)),
        out_shape=jax.ShapeDtypeStruct((M, N), out_dtype),
        name=name,
    )(a, b)


def _mm_f32w_body(a_ref, w_ref, o_ref, wb_ref, *, act):
    @pl.when(pl.program_id(1) == 0)
    def _():
        wb_ref[...] = w_ref[...].astype(wb_ref.dtype)

    acc = _dot(a_ref[...], wb_ref[...])
    if act == "relu2":
        acc = jnp.square(jnp.maximum(acc, 0.0))
    o_ref[...] = acc.astype(o_ref.dtype)


def _matmul_f32w(a, w, n_cols, *, out_dtype, act=None, name, bm=1024, bn=1024):
    M, K = a.shape
    bm, bn = _blk(M, bm), _blk(n_cols, bn)
    return _call(
        functools.partial(_mm_f32w_body, act=act),
        grid=(n_cols // bn, M // bm),
        in_specs=[pl.BlockSpec((bm, K), lambda j, i: (i, 0)),
                  pl.BlockSpec((K, bn), lambda j, i: (0, j))],
        out_specs=pl.BlockSpec((bm, bn), lambda j, i: (i, j)),
        out_shape=jax.ShapeDtypeStruct((M, n_cols), out_dtype),
        scratch_shapes=[pltpu.VMEM((K, bn), BF16)],
        name=name,
    )(a, w)


def _shift_cols_body(a_ref, b_ref, o_ref, *, shift, rows):
    def chunk(r, carry):
        rs = pl.ds(pl.multiple_of(r * rows, rows), rows)
        o_ref[rs, :] = jnp.concatenate([a_ref[rs, shift:], b_ref[rs, :shift]], axis=1).astype(o_ref.dtype)
        return carry

    lax.fori_loop(0, o_ref.shape[0] // rows, chunk, 0)


def _take_cols_bf16(w, start, width):
    K = w.shape[0]
    bn = _blk(width, 1024)
    shift = start % bn
    assert 0 < shift < V7X_LANES and (start - shift) % bn == 0
    a0 = (start - shift) // bn
    b0 = (start - shift) // V7X_LANES
    per = bn // V7X_LANES
    return _call(
        functools.partial(_shift_cols_body, shift=shift, rows=_blk(K, 256)),
        grid=(width // bn,),
        in_specs=[pl.BlockSpec((K, bn), lambda j: (0, a0 + j)),
                  pl.BlockSpec((K, V7X_LANES), lambda j: (0, b0 + per * (j + 1)))],
        out_specs=pl.BlockSpec((K, bn), lambda j: (0, j)),
        out_shape=jax.ShapeDtypeStruct((K, width), BF16),
        name="take_cols_bf16",
    )(w, w)


def _hy_proj_body(w_ref, x_ref, cp_ref, o_ref):
    u = _dot(w_ref[...], x_ref[...], TNT)
    L = u.shape[1]
    t = lax.broadcasted_iota(jnp.int32, u.shape, 1)
    prev = jnp.where(t == 0, 0.0, pltpu.roll(u, 1, axis=1))
    nxt = jnp.where(t == L - 1, 0.0, pltpu.roll(u, L - 1, axis=1))
    cp = cp_ref[...]
    o_ref[...] = (cp[:, 0:1] * prev + cp[:, 1:2] * u + cp[:, 2:3] * nxt + cp[:, 3:4]).astype(o_ref.dtype)


def _hy_proj(w, xb, conv_params, B, L):
    D, C3 = w.shape
    bc = _blk(C3, 512)
    nc = C3 // bc
    return _call(
        _hy_proj_body,
        grid=(B, nc),
        in_specs=[pl.BlockSpec((D, bc), lambda b, c: (0, c)),
                  pl.BlockSpec((L, D), lambda b, c: (b, 0)),
                  pl.BlockSpec((bc, 4), lambda b, c: (c, 0))],
        out_specs=pl.BlockSpec((bc, L), lambda b, c: (b * nc + c, 0)),
        out_shape=jax.ShapeDtypeStruct((B * C3, L), BF16),
        name="hy_proj_conv",
    )(w, xb, conv_params)


def _log_sigmoid(z):
    return -(jnp.maximum(-z, 0.0) + jnp.log(1.0 + jnp.exp(-jnp.abs(z))))


def _decay_body(x_ref, wab_ref, w2f_ref, bf_ref, w2b_ref, bb_ref, cf_ref, cb_ref, xb_ref, *, block):
    xb = x_ref[...].astype(xb_ref.dtype)
    xb_ref[...] = xb
    ab = _dot(xb, wab_ref[...])
    zf = _dot_f32(ab, w2f_ref[...]) + bf_ref[...]
    zb = _dot_f32(ab, w2b_ref[...]) + bb_ref[...]
    laf = _log_sigmoid(zf) * (1.0 / GLA_TAU)
    lab = _log_sigmoid(zb) * (1.0 / GLA_TAU)
    T = block
    row = lax.broadcasted_iota(jnp.int32, (T, T), 0)
    col = lax.broadcasted_iota(jnp.int32, (T, T), 1)
    tri_f = (col <= row).astype(BF16)
    tri_b = (col >= row).astype(BF16)
    for s in range(xb.shape[0] // T):
        cf_ref[s * T:(s + 1) * T, :] = _dot_exact_lhs(tri_f, laf[s * T:(s + 1) * T, :])
        cb_ref[s * T:(s + 1) * T, :] = _dot_exact_lhs(tri_b, lab[s * T:(s + 1) * T, :])


def _gla_decays(x2d, w_in, ab_col, wa2_f, ba_f, wa2_b, ba_b, block):
    M, D = x2d.shape
    rank, KEY = wa2_f.shape
    P = V7X_LANES
    assert 2 * rank <= P and ab_col % P == 0
    w2f = jnp.pad(wa2_f, ((0, P - rank), (0, 0)))
    w2b = jnp.pad(wa2_b, ((rank, P - 2 * rank), (0, 0)))
    bm = max(_blk(M, 512), block)
    full = lambda shape: pl.BlockSpec(shape, lambda i: (0, 0))
    out = jax.ShapeDtypeStruct((M, KEY), F32)
    return _call(
        functools.partial(_decay_body, block=block),
        grid=(M // bm,),
        in_specs=[pl.BlockSpec((bm, D), lambda i: (i, 0)), pl.BlockSpec((D, P), lambda i: (0, ab_col // P)),
                  full((P, KEY)), full((1, KEY)), full((P, KEY)), full((1, KEY))],
        out_specs=[pl.BlockSpec((bm, KEY), lambda i: (i, 0))] * 2 + [pl.BlockSpec((bm, D), lambda i: (i, 0))],
        out_shape=[out, out, jax.ShapeDtypeStruct((M, D), BF16)],
        name="gla_decays",
    )(x2d, w_in, w2f, ba_f.reshape(1, KEY), w2b, ba_b.reshape(1, KEY))


def _gla_body(q_ref, k_ref, v_ref, r_ref, cf_ref, cb_ref, g_ref, y_ref, o_acc,
              stf_ref, sf_ref, qbf_ref, kef_ref, klf_ref,
              stb_ref, sb_ref, qbb_ref, keb_ref, klb_ref, *, block, sub):
    L, DK = q_ref.shape
    T, C = block, sub
    n, ns = L // T, T // C
    scale = DK ** -0.5
    row = lax.broadcasted_iota(jnp.int32, (C, C), 0)
    col = lax.broadcasted_iota(jnp.int32, (C, C), 1)

    def boundary_row(cum_ref, start, pick):
        return cum_ref[pl.ds(pl.multiple_of(start, 8), 8), :][pick:pick + 1, :]

    def block_step(blk, cum_ref, st_ref, s_ref, qb_ref, ke_ref, kl_ref, forward, first_visit):
        r0 = pl.multiple_of(blk * T, T)
        tot = boundary_row(cum_ref, r0 + T - 8, 7) if forward else boundary_row(cum_ref, r0, 0)
        mask = (col <= row) if forward else (col > row)
        refs = {}
        for I in (range(ns) if forward else range(ns - 1, -1, -1)):
            sl = slice(I * C, (I + 1) * C)
            rows = pl.ds(r0 + I * C, C)
            if forward:
                ref = boundary_row(cum_ref, r0 + I * C - 8, 7) if I > 0 else jnp.zeros_like(tot)
            else:
                ref = boundary_row(cum_ref, r0 + (I + 1) * C, 0) if I < ns - 1 else jnp.zeros_like(tot)
            d = cum_ref[rows, :] - ref
            q_loc = q_ref[rows, :] * (scale * jnp.exp(d))
            k_loc = k_ref[rows, :] * jnp.exp(-d)
            qb_ref[sl, :] = (q_loc * jnp.exp(ref)).astype(BF16)
            ke_ref[sl, :] = (k_loc * jnp.exp(tot - ref)).astype(BF16)
            k_loc = k_loc.astype(BF16)
            kl_ref[sl, :] = k_loc
            s_ref[sl, sl] = jnp.where(mask, _dot(q_loc, k_loc, NT), 0.0).astype(BF16)
            for J, ref_j in refs.items():
                sj = slice(J * C, (J + 1) * C)
                s_ref[sl, sj] = _dot(q_loc * jnp.exp(ref - ref_j), kl_ref[sj, :], NT).astype(BF16)
            refs[I] = ref
        rows = pl.ds(r0, T)
        vb = v_ref[rows, :].astype(BF16)
        st = st_ref[...]
        o = _dot(s_ref[...], vb) + _dot(qb_ref[...], st, NT)
        st_ref[...] = st * jnp.exp(tot) + _dot(vb, ke_ref[...], TN)
        if first_visit:
            o_acc[rows, :] = o
        else:
            o = o_acc[rows, :] + o
            o = o * lax.rsqrt(jnp.mean(o * o, axis=-1, keepdims=True) + LN_EPS) * g_ref[...]
            r = r_ref[rows, :].astype(F32)
            y_ref[rows, :] = (o * (r * jax.nn.sigmoid(r))).astype(y_ref.dtype)

    fwd = functools.partial(block_step, cum_ref=cf_ref, st_ref=stf_ref, s_ref=sf_ref, qb_ref=qbf_ref,
                            ke_ref=kef_ref, kl_ref=klf_ref, forward=True)
    bwd = functools.partial(block_step, cum_ref=cb_ref, st_ref=stb_ref, s_ref=sb_ref, qb_ref=qbb_ref,
                            ke_ref=keb_ref, kl_ref=klb_ref, forward=False)

    for ref in (stf_ref, stb_ref, sf_ref, sb_ref):
        ref[...] = jnp.zeros_like(ref)

    def sweep(lo, hi, step):
        def body(i, carry):
            step(i)
            return carry
        lax.fori_loop(lo, hi, body, 0)

    if n % 2 == 0:
        def first_half(i):
            fwd(i, first_visit=True)
            bwd(n - 1 - i, first_visit=True)

        def second_half(i):
            fwd(i, first_visit=False)
            bwd(n - 1 - i, first_visit=False)

        sweep(0, n // 2, first_half)
        sweep(n // 2, n, second_half)
    else:
        sweep(0, n, lambda i: fwd(i, first_visit=True))
        sweep(0, n, lambda i: bwd(n - 1 - i, first_visit=False))


def _gla(p_qkvr, cum_f, cum_b, norm_g, B, L, KEY, VAL, block):
    H = GLA_HEADS
    DK, DV = KEY // H, VAL // H
    M = B * L
    kq = KEY // DK
    vq = 2 * KEY // DV
    rq = (2 * KEY + VAL) // DV
    T = block
    direction_scratch = [pltpu.VMEM((DV, DK), F32), pltpu.VMEM((T, T), BF16), pltpu.VMEM((T, DK), BF16),
                         pltpu.VMEM((T, DK), BF16), pltpu.VMEM((T, DK), BF16)]
    return _call(
        functools.partial(_gla_body, block=T, sub=_blk(T, GLA_CHUNK)),
        grid=(B, H),
        in_specs=[pl.BlockSpec((L, DK), lambda b, h: (b, h)),
                  pl.BlockSpec((L, DK), lambda b, h: (b, kq + h)),
                  pl.BlockSpec((L, DV), lambda b, h: (b, vq + h)),
                  pl.BlockSpec((L, DV), lambda b, h: (b, rq + h)),
                  pl.BlockSpec((L, DK), lambda b, h: (b, h)),
                  pl.BlockSpec((L, DK), lambda b, h: (b, h)),
                  pl.BlockSpec((1, DV), lambda b, h: (0, h))],
        out_specs=pl.BlockSpec((L, DV), lambda b, h: (b, h)),
        out_shape=jax.ShapeDtypeStruct((M, VAL), BF16),
        scratch_shapes=[pltpu.VMEM((L, DV), F32)] + direction_scratch * 2,
        name="gla_bidir",
    )(p_qkvr, p_qkvr, p_qkvr, p_qkvr, cum_f, cum_b, norm_g.reshape(1, VAL))


def _filter_body(emb_ref, w1_ref, b1_ref, w2_ref, b2_ref, w3_ref, b3_ref, fr_ref,
                 w4f_ref, b4f_ref, w4b_ref, b4b_ref, dl_ref, a_ref, bm_ref, ny_ref):
    emb = emb_ref[...]
    fr = fr_ref[...]
    h = jnp.sin(fr * (_dot_f32(w1_ref[...], emb) + b1_ref[...]))
    h = jnp.sin(fr * (_dot_f32(w2_ref[...], h) + b2_ref[...]))
    h = jnp.sin(fr * (_dot_f32(w3_ref[...], h) + b3_ref[...]))
    t_lin = emb[0:1, :]
    decay = jnp.exp(-t_lin * dl_ref[...])
    hf = (_dot_f32(w4f_ref[...], h) + b4f_ref[...]) * decay
    hb = (_dot_f32(w4b_ref[...], h) + b4b_ref[...]) * decay
    t = lax.broadcasted_iota(jnp.int32, hf.shape, 1)
    hb = jnp.where(t == 0, 0.0, hb)
    a = hf + hb
    a_ref[...] = a.astype(a_ref.dtype)
    bm_ref[...] = (hf - hb).astype(bm_ref.dtype)
    sign = jnp.where((t & 1) == 0, 1.0, -1.0)
    ny_ref[...] = jnp.sum(a * sign, axis=-1, keepdims=True)


def _hyena_filters(emb_t, w1, b1, w2, b2, w3, b3, w4, b4, freq, deltas, W):
    L = emb_t.shape[1]
    HID = w1.shape[1]
    n_ord = w4.shape[1] // (2 * W)
    EMB = -(-emb_t.shape[0] // V7X_LANES) * V7X_LANES
    w1 = jnp.pad(w1, ((0, EMB - w1.shape[0]), (0, 0)))
    emb_t = jnp.pad(emb_t, ((0, EMB - emb_t.shape[0]), (0, 0)))
    cb = _blk(W, 512)
    ncb = W // cb
    w4t = w4.T
    b4c = b4.reshape(-1, 1)
    colv = lambda v: v.reshape(-1, 1)
    full = lambda shape: pl.BlockSpec(shape, lambda o, c: (0, 0))
    rows = n_ord * W
    return _call(
        _filter_body,
        grid=(n_ord, ncb),
        in_specs=[full((EMB, L)), full((HID, EMB)), full((HID, 1)), full((HID, HID)), full((HID, 1)),
                  full((HID, HID)), full((HID, 1)), full((HID, 1)),
                  pl.BlockSpec((cb, HID), lambda o, c: (o * 2 * ncb + c, 0)),
                  pl.BlockSpec((cb, 1), lambda o, c: (o * 2 * ncb + c, 0)),
                  pl.BlockSpec((cb, HID), lambda o, c: (o * 2 * ncb + ncb + c, 0)),
                  pl.BlockSpec((cb, 1), lambda o, c: (o * 2 * ncb + ncb + c, 0)),
                  pl.BlockSpec((cb, 1), lambda o, c: (c, 0))],
        out_specs=[pl.BlockSpec((cb, L), lambda o, c: (o * ncb + c, 0)),
                   pl.BlockSpec((cb, L), lambda o, c: (o * ncb + c, 0)),
                   pl.BlockSpec((cb, 1), lambda o, c: (o * ncb + c, 0))],
        out_shape=[jax.ShapeDtypeStruct((rows, L), BF16), jax.ShapeDtypeStruct((rows, L), BF16),
                   jax.ShapeDtypeStruct((rows, 1), F32)],
        name="hy_filters",
    )(emb_t, w1.T, colv(b1), w2.T, colv(b2), w3.T, colv(b3), colv(freq), w4t, b4c, w4t, b4c, deltas)


def _spec_body(a_ref, f_ref, s_ref, o_ref):
    o_ref[...] = _dot(a_ref[...], f_ref[...]) * s_ref[...]


def _filter_spectrum(a, fmat, scale):
    R, L = a.shape
    bm, bn = _blk(R, 1024), _blk(L, 1024)
    return _call(
        _spec_body,
        grid=(R // bm, L // bn),
        in_specs=[pl.BlockSpec((bm, L), lambda i, j: (i, 0)),
                  pl.BlockSpec((L, bn), lambda i, j: (0, j)),
                  pl.BlockSpec((1, bn), lambda i, j: (0, j))],
        out_specs=pl.BlockSpec((bm, bn), lambda i, j: (i, j)),
        out_shape=jax.ShapeDtypeStruct((R, L), F32),
        name="hy_filter_spectrum",
    )(a, fmat, scale)


def _fwd_dft_body(z_ref, c_ref, s_ref, hr_ref, hi_ref, hn_ref, pr_ref, pi_ref):
    z = z_ref[...]
    zr = _dot(z, c_ref[...])
    zi = _dot(z, s_ref[...])
    hr, hi = hr_ref[...], hi_ref[...]
    k = lax.broadcasted_iota(jnp.int32, zr.shape, 1) + pl.program_id(1) * zr.shape[1]
    dc = (k == 0)
    pr = zr * hr - jnp.where(dc, 0.0, zi * hi)
    pi = jnp.where(dc, zi * hn_ref[...], zr * hi + zi * hr)
    pr_ref[...] = pr.astype(pr_ref.dtype)
    pi_ref[...] = pi.astype(pi_ref.dtype)


def _fwd_dft(z2d, z_row0, z_rows_per_b, cmat, smat, hr, hi, hn, h_row0, B, W, L):
    bm, bn = _blk(W, 1024), _blk(L, 512)
    ncb = W // bm
    zb, z0, h0 = z_rows_per_b // bm, z_row0 // bm, h_row0 // bm
    out = jax.ShapeDtypeStruct((B * W, L), BF16)
    return _call(
        _fwd_dft_body,
        grid=(ncb, L // bn, B),
        in_specs=[pl.BlockSpec((bm, L), lambda c, j, b: (b * zb + z0 + c, 0)),
                  pl.BlockSpec((L, bn), lambda c, j, b: (0, j)),
                  pl.BlockSpec((L, bn), lambda c, j, b: (0, j)),
                  pl.BlockSpec((bm, bn), lambda c, j, b: (h0 + c, j)),
                  pl.BlockSpec((bm, bn), lambda c, j, b: (h0 + c, j)),
                  pl.BlockSpec((bm, 1), lambda c, j, b: (h0 + c, 0))],
        out_specs=[pl.BlockSpec((bm, bn), lambda c, j, b: (b * ncb + c, j))] * 2,
        out_shape=[out, out],
        name="hy_fwd_dft",
    )(z2d, cmat, smat, hr, hi, hn)


def _inv_dft_body(pr_ref, pi_ref, c_ref, st_ref, gate_ref, z_ref, skip_ref, o_ref):
    zc = _dot(pr_ref[...], c_ref[...]) + _dot(pi_ref[...], st_ref[...])
    o_ref[...] = (gate_ref[...] * (zc + skip_ref[...] * z_ref[...])).astype(o_ref.dtype)


def _inv_dft(pr, pi, cmat, smat_t, gates2d, g_row0, g_rows_per_b, z2d, z_row0, z_rows_per_b,
             skip_col, s_row0, B, W, L, out_dtype):
    bm, bn = _blk(W, 1024), _blk(L, 1024)
    ncb = W // bm
    gb, g0 = g_rows_per_b // bm, g_row0 // bm
    zb, z0 = z_rows_per_b // bm, z_row0 // bm
    s0 = s_row0 // bm
    return _call(
        _inv_dft_body,
        grid=(B, ncb, L // bn),
        in_specs=[pl.BlockSpec((bm, L), lambda b, c, j: (b * ncb + c, 0)),
                  pl.BlockSpec((bm, L), lambda b, c, j: (b * ncb + c, 0)),
                  pl.BlockSpec((L, bn), lambda b, c, j: (0, j)),
                  pl.BlockSpec((L, bn), lambda b, c, j: (0, j)),
                  pl.BlockSpec((bm, bn), lambda b, c, j: (b * gb + g0 + c, j)),
                  pl.BlockSpec((bm, bn), lambda b, c, j: (b * zb + z0 + c, j)),
                  pl.BlockSpec((bm, 1), lambda b, c, j: (s0 + c, 0))],
        out_specs=pl.BlockSpec((bm, bn), lambda b, c, j: (b * ncb + c, j)),
        out_shape=jax.ShapeDtypeStruct((B * W, L), out_dtype),
        name="hy_inv_dft",
    )(pr, pi, cmat, smat_t, gates2d, z2d, skip_col)


def _merge_body(yg_ref, yh_ref, wg_ref, wh_ref, g0_ref, g1_ref, o_ref):
    tg = _dot(yg_ref[...], wg_ref[...])
    th = _dot(yh_ref[...], wh_ref[...], TN)
    o_ref[...] = (g0_ref[...] * tg + g1_ref[...] * th).astype(o_ref.dtype)


def _branch_merge(y_gla, y_hy_t, w_gla_o, w_hy_o, gates, B, L, D):
    VAL, W = y_gla.shape[1], w_hy_o.shape[0]
    bt = _blk(L, 512)
    nt = L // bt
    resident = lambda shape: pl.BlockSpec(shape, lambda b, t: (0, 0), pipeline_mode=pl.Buffered(1))
    return _call(
        _merge_body,
        grid=(B, nt),
        in_specs=[pl.BlockSpec((bt, VAL), lambda b, t: (b * nt + t, 0)),
                  pl.BlockSpec((W, bt), lambda b, t: (b, t)),
                  resident((VAL, D)), resident((W, D)),
                  pl.BlockSpec((bt, D), lambda b, t: (b * nt + t, 0)),
                  pl.BlockSpec((bt, D), lambda b, t: (b * nt + t, 1))],
        out_specs=pl.BlockSpec((bt, D), lambda b, t: (b * nt + t, 0)),
        out_shape=jax.ShapeDtypeStruct((B * L, D), BF16),
        name="branch_merge",
    )(y_gla, y_hy_t, w_gla_o, w_hy_o, gates, gates)


def _proj_ln_body(a_ref, w_ref, x_ref, g_ref, b_ref, o_ref, ob_ref, *, alpha):
    y = alpha * x_ref[...] + _dot(a_ref[...], w_ref[...])
    h = _layernorm_rows(y, g_ref[...], b_ref[...])
    o_ref[...] = h
    ob_ref[...] = h.astype(ob_ref.dtype)


def _proj_ln(a, w, x2d, ln_g, ln_b, alpha):
    M, K = a.shape
    D = w.shape[1]
    bm = _blk(M, 512)
    row = lambda i: (i, 0)
    full = lambda shape: pl.BlockSpec(shape, lambda i: (0, 0))
    return _call(
        functools.partial(_proj_ln_body, alpha=alpha),
        grid=(M // bm,),
        in_specs=[pl.BlockSpec((bm, K), row), full((K, D)), pl.BlockSpec((bm, D), row),
                  full((1, D)), full((1, D))],
        out_specs=[pl.BlockSpec((bm, D), row)] * 2,
        out_shape=[jax.ShapeDtypeStruct((M, D), F32), jax.ShapeDtypeStruct((M, D), BF16)],
        name="out_proj_ln1",
    )(a, w, x2d, ln_g.reshape(1, D), ln_b.reshape(1, D))


def _ff2_ln_body(a_ref, w_ref, h_ref, g_ref, b_ref, o_ref, *, alpha, ln_rows):
    kk = pl.program_id(1)

    @pl.when(kk == 0)
    def _():
        o_ref[...] = jnp.zeros_like(o_ref)

    o_ref[...] += _dot(a_ref[...], w_ref[...])

    @pl.when(kk == pl.num_programs(1) - 1)
    def _():
        g, b = g_ref[...], b_ref[...]

        def norm_rows(r, carry):
            rows = pl.ds(pl.multiple_of(r * ln_rows, ln_rows), ln_rows)
            y = alpha * h_ref[rows, :] + o_ref[rows, :]
            o_ref[rows, :] = _layernorm_rows(y, g, b)
            return carry

        lax.fori_loop(0, o_ref.shape[0] // ln_rows, norm_rows, 0)


def _ff2_ln(a, w, h, ln_g, ln_b, alpha):
    M, K = a.shape
    D = w.shape[1]
    bm, bk = _blk(M, 1024), _blk(K, 1024)
    return _call(
        functools.partial(_ff2_ln_body, alpha=alpha, ln_rows=_blk(bm, 128)),
        grid=(M // bm, K // bk),
        in_specs=[pl.BlockSpec((bm, bk), lambda i, k: (i, k)),
                  pl.BlockSpec((bk, D), lambda i, k: (k, 0)),
                  pl.BlockSpec((bm, D), lambda i, k: (i, 0)),
                  pl.BlockSpec((1, D), lambda i, k: (0, 0)),
                  pl.BlockSpec((1, D), lambda i, k: (0, 0))],
        out_specs=pl.BlockSpec((bm, D), lambda i, k: (i, 0)),
        out_shape=jax.ShapeDtypeStruct((M, D), F32),
        name="ff2_ln2",
    )(a, w, h, ln_g.reshape(1, D), ln_b.reshape(1, D))


def _dft_tables(L):
    LO = _blk(L, 32)
    k = jnp.arange(L, dtype=jnp.int32)
    ang = lambda m: ((m[:, None] * k[None, :]) % (2 * L)).astype(F32) * (math.pi / L)
    a_hi = ang(jnp.arange(L // LO, dtype=jnp.int32) * LO)[:, None, :]
    a_lo = ang(jnp.arange(LO, dtype=jnp.int32))[None, :, :]
    cos_t = (jnp.cos(a_hi) * jnp.cos(a_lo) - jnp.sin(a_hi) * jnp.sin(a_lo)).reshape(L, L)
    sin_t = (jnp.sin(a_hi) * jnp.cos(a_lo) + jnp.cos(a_hi) * jnp.sin(a_lo)).reshape(L, L)
    nyq = jnp.where(k % 2 == 0, 1.0, -1.0).astype(F32)
    smat = jnp.where(k[None, :] == 0, nyq[:, None], sin_t)
    smat_t = jnp.where(k[:, None] == 0, nyq[None, :], sin_t)
    return cos_t.astype(BF16), smat.astype(BF16), smat_t.astype(BF16)


def _position_features(L, emb_dim):
    t = jnp.linspace(0.0, 1.0, L, dtype=F32)[:, None]
    bands = (emb_dim - 1) // 2
    f = jnp.linspace(1e-4, bands - 1, bands, dtype=F32)
    wpos = 2.0 * math.pi * jnp.arange(L, dtype=F32) / L
    ang = wpos[:, None] * f[None, :]
    return jnp.concatenate([t, jnp.cos(ang), -jnp.sin(ang)], axis=-1).T


def _layer(h, w_in, gla_wa2_f, gla_ba_f, gla_wa2_b, gla_ba_b, gla_norm_g, w_gla_o,
           hy_conv_w, hy_conv_b, hy_w1, hy_b1, hy_w2, hy_b2, hy_w3, hy_b3, hy_w4, hy_b4,
           hy_freq, hy_skip, w_hy_o, w_out, ln1_g, ln1_b, w_ff1, w_ff2, ln2_g, ln2_b, alpha):
    B, L, D = h.shape
    M = B * L
    rank, KEY = gla_wa2_f.shape
    VAL = gla_norm_g.shape[0]
    n_ord, W = hy_skip.shape

    sizes = (KEY, KEY, VAL, VAL, rank, rank, (n_ord + 1) * W, 2 * D)
    offs = [int(v) for v in np.concatenate([[0], np.cumsum(sizes)])]
    w_hy = _take_cols_bf16(w_in, offs[6], sizes[6])
    w_gate = _take_cols_bf16(w_in, offs[7], sizes[7])

    x2d = h.reshape(M, D)
    gla_block = _blk(L, GLA_BLOCK)
    cum_f, cum_b, xb = _gla_decays(x2d, w_in, offs[4], gla_wa2_f, gla_ba_f, gla_wa2_b, gla_ba_b, gla_block)

    p_qkvr = _matmul_f32w(xb, w_in, offs[4], out_dtype=BF16, name="in_proj_qkvr", bm=2048)
    gates = _matmul(xb, w_gate, out_dtype=BF16, act="sigmoid", name="in_proj_gates", bm=2048)
    conv_params = jnp.concatenate([hy_conv_w.T, hy_conv_b[:, None]], axis=1)
    u_t = _hy_proj(w_hy, xb, conv_params, B, L)

    y_gla = _gla(p_qkvr, cum_f, cum_b, gla_norm_g, B, L, KEY, VAL, gla_block)

    emb_t = _position_features(L, hy_w1.shape[0])
    min_decay = math.log(HY_DECAY_TARGET) / HY_SLOW_DECAY
    max_decay = math.log(HY_DECAY_TARGET) / HY_FAST_DECAY
    deltas = jnp.abs(jnp.linspace(min_decay, max_decay, W, dtype=F32)).reshape(W, 1)
    a_even, a_odd, nyq = _hyena_filters(emb_t, hy_w1, hy_b1, hy_w2, hy_b2, hy_w3, hy_b3,
                                        hy_w4, hy_b4, hy_freq, deltas, W)
    cmat, smat, smat_t = _dft_tables(L)
    bin_w = jnp.full((1, L), 1.0 / L, F32).at[0, 0].set(0.5 / L)
    h_re = _filter_spectrum(a_even, cmat, bin_w)
    h_im = _filter_spectrum(a_odd, smat, bin_w)
    h_ny = nyq * (0.5 / L)

    skip_col = hy_skip.reshape(n_ord * W, 1)
    z, z_row0, z_rows = u_t, 0, (n_ord + 1) * W
    for o in range(n_ord):
        pr, pi = _fwd_dft(z, z_row0, z_rows, cmat, smat, h_re, h_im, h_ny, o * W, B, W, L)
        z = _inv_dft(pr, pi, cmat, smat_t, u_t, (o + 1) * W, (n_ord + 1) * W, z, z_row0, z_rows,
                     skip_col, o * W, B, W, L, BF16)
        z_row0, z_rows = 0, W
    y_hy_t = z

    merged = _branch_merge(y_gla, y_hy_t, w_gla_o.astype(BF16), w_hy_o.astype(BF16), gates, B, L, D)
    h1, h1b = _proj_ln(merged, w_out.astype(BF16), x2d, ln1_g, ln1_b, alpha)
    act = _matmul_f32w(h1b, w_ff1, w_ff1.shape[1], out_dtype=BF16, act="relu2", name="ff1_relu2")
    out = _ff2_ln(act, w_ff2.astype(BF16), h1, ln2_g, ln2_b, alpha)
    return out.reshape(B, L, D)


def kernel(x, w_in, gla_wa2_f, gla_ba_f, gla_wa2_b, gla_ba_b, gla_norm_g, w_gla_o, hy_conv_w, hy_conv_b, hy_w1, hy_b1, hy_w2, hy_b2, hy_w3, hy_b3, hy_w4, hy_b4, hy_freq, hy_skip, w_hy_o, w_out, ln1_g, ln1_b, w_ff1, w_ff2, ln2_g, ln2_b):
    depth = w_in.shape[0]
    alpha = (2 * depth) ** 0.25
    params = (w_in, gla_wa2_f, gla_ba_f, gla_wa2_b, gla_ba_b, gla_norm_g, w_gla_o, hy_conv_w, hy_conv_b,
              hy_w1, hy_b1, hy_w2, hy_b2, hy_w3, hy_b3, hy_w4, hy_b4, hy_freq, hy_skip, w_hy_o, w_out,
              ln1_g, ln1_b, w_ff1, w_ff2, ln2_g, ln2_b)
    h = x
    for l in range(depth):
        h = _layer(h, *(p[l] for p in params), alpha)
    return h
```

```python
import functools
import math

import jax
import jax.numpy as jnp
import numpy as np
from jax import lax
from jax.experimental import pallas as pl
from jax.experimental.pallas import tpu as pltpu

F32 = jnp.float32
BF16 = jnp.bfloat16

GLA_HEADS = 4
GLA_TAU = 16.0
GLA_CHUNK = 64
GLA_BLOCK = 256
HY_FAST_DECAY = 0.3
HY_SLOW_DECAY = 1.5
HY_DECAY_TARGET = 1e-2
LN_EPS = 1e-5

V7X_VMEM_LIMIT_BYTES = 56 * 1024 * 1024
V7X_LANES = 128

NN = (((1,), (0,)), ((), ()))
NT = (((1,), (1,)), ((), ()))
TN = (((0,), (0,)), ((), ()))


def _dot(a, b, dims=NN):
    return lax.dot_general(a.astype(BF16), b.astype(BF16), dims, preferred_element_type=F32)


def _split2(a):
    h1 = a.astype(BF16)
    h2 = (a - h1.astype(F32)).astype(BF16)
    return h1, h2


def _dot_f32(a, b, dims=NN):
    a1, a2 = _split2(a)
    b1, b2 = _split2(b)
    d = lambda x, y: lax.dot_general(x, y, dims, preferred_element_type=F32)
    return (d(a2, b1) + d(a1, b2)) + d(a1, b1)


def _dot_exact_lhs(a_bf16, b, dims=NN):
    b1, b2 = _split2(b)
    d = lambda y: lax.dot_general(a_bf16, y, dims, preferred_element_type=F32)
    return d(b2) + d(b1)


def _call(body, *, grid, in_specs, out_specs, out_shape, name, scratch_shapes=()):
    return pl.pallas_call(
        body,
        grid=grid,
        in_specs=in_specs,
        out_specs=out_specs,
        out_shape=out_shape,
        scratch_shapes=scratch_shapes,
        compiler_params=pltpu.CompilerParams(
            dimension_semantics=("arbitrary",) * len(grid),
            vmem_limit_bytes=V7X_VMEM_LIMIT_BYTES,
        ),
        name=name,
    )


def _blk(n, want):
    b = min(n, want)
    while n % b:
        b //= 2
    return b


def _layernorm_rows(y, g, b):
    mu = jnp.mean(y, axis=-1, keepdims=True)
    d = y - mu
    var = jnp.mean(d * d, axis=-1, keepdims=True)
    return d * lax.rsqrt(var + LN_EPS) * g + b


def _act(acc, act):
    if act == "sigmoid":
        return jax.nn.sigmoid(acc)
    if act == "relu2":
        return jnp.square(jnp.maximum(acc, 0.0))
    return acc


def _mm_body(a_ref, b_ref, o_ref, *, act, dims):
    o_ref[...] = _act(_dot(a_ref[...], b_ref[...], dims), act).astype(o_ref.dtype)


def _weight_spec(K, bn, transposed, index):
    if transposed:
        return pl.BlockSpec((bn, K), lambda *g: (index(*g), 0))
    return pl.BlockSpec((K, bn), lambda *g: (0, index(*g)))


def _matmul(a, b, *, out_dtype, act=None, name, bm=1024, bn=1024, transposed=False):
    M, K = a.shape
    N = b.shape[0] if transposed else b.shape[1]
    bm, bn = _blk(M, bm), _blk(N, bn)
    return _call(
        functools.partial(_mm_body, act=act, dims=NT if transposed else NN),
        grid=(M // bm, N // bn),
        in_specs=[pl.BlockSpec((bm, K), lambda i, j: (i, 0)),
                  _weight_spec(K, bn, transposed, lambda i, j: j)],
        out_specs=pl.BlockSpec((bm, bn), lambda i, j: (i, j)),
        out_shape=jax.ShapeDtypeStruct((M, N), out_dtype),
        name=name,
    )(a, b)


def _mm_f32w_body(a_ref, w_ref, o_ref, wb_ref, *, act, dims):
    @pl.when(pl.program_id(1) == 0)
    def _():
        wb_ref[...] = w_ref[...].astype(wb_ref.dtype)

    o_ref[...] = _act(_dot(a_ref[...], wb_ref[...], dims), act).astype(o_ref.dtype)


def _matmul_f32w(a, w, n_out, *, out_dtype, act=None, name, bm=1024, bn=1024, transposed=False):
    M, K = a.shape
    bm, bn = _blk(M, bm), _blk(n_out, bn)
    return _call(
        functools.partial(_mm_f32w_body, act=act, dims=NT if transposed else NN),
        grid=(n_out // bn, M // bm),
        in_specs=[pl.BlockSpec((bm, K), lambda j, i: (i, 0)),
                  _weight_spec(K, bn, transposed, lambda j, i: j)],
        out_specs=pl.BlockSpec((bm, bn), lambda j, i: (i, j)),
        out_shape=jax.ShapeDtypeStruct((M, n_out), out_dtype),
        scratch_shapes=[pltpu.VMEM((bn, K) if transposed else (K, bn), BF16)],
        name=name,
    )(a, w)


def _hy_proj_body(w_ref, x_ref, cp_ref, o_ref):
    u = _dot(w_ref[...], x_ref[...], NT)
    L = u.shape[1]
    t = lax.broadcasted_iota(jnp.int32, u.shape, 1)
    prev = jnp.where(t == 0, 0.0, pltpu.roll(u, 1, axis=1))
    nxt = jnp.where(t == L - 1, 0.0, pltpu.roll(u, L - 1, axis=1))
    cp = cp_ref[...]
    o_ref[...] = (cp[:, 0:1] * prev + cp[:, 1:2] * u + cp[:, 2:3] * nxt + cp[:, 3:4]).astype(o_ref.dtype)


def _hy_proj(w_t, xb, conv_params, B, L):
    C3, D = w_t.shape
    bc = _blk(C3, 512)
    nc = C3 // bc
    return _call(
        _hy_proj_body,
        grid=(B, nc),
        in_specs=[pl.BlockSpec((bc, D), lambda b, c: (c, 0)),
                  pl.BlockSpec((L, D), lambda b, c: (b, 0)),
                  pl.BlockSpec((bc, 4), lambda b, c: (c, 0))],
        out_specs=pl.BlockSpec((bc, L), lambda b, c: (b * nc + c, 0)),
        out_shape=jax.ShapeDtypeStruct((B * C3, L), BF16),
        name="hy_proj_conv",
    )(w_t, xb, conv_params)


def _log_sigmoid(z):
    return -(jnp.maximum(-z, 0.0) + jnp.log(1.0 + jnp.exp(-jnp.abs(z))))


def _decay_body(x_ref, wab_ref, w2f_ref, bf_ref, w2b_ref, bb_ref, cf_ref, cb_ref, xb_ref, *, block):
    xb = x_ref[...].astype(xb_ref.dtype)
    xb_ref[...] = xb
    ab = _dot(xb, wab_ref[...], NT)
    zf = _dot_f32(ab, w2f_ref[...]) + bf_ref[...]
    zb = _dot_f32(ab, w2b_ref[...]) + bb_ref[...]
    laf = _log_sigmoid(zf) * (1.0 / GLA_TAU)
    lab = _log_sigmoid(zb) * (1.0 / GLA_TAU)
    T = block
    row = lax.broadcasted_iota(jnp.int32, (T, T), 0)
    col = lax.broadcasted_iota(jnp.int32, (T, T), 1)
    tri_f = (col <= row).astype(BF16)
    tri_b = (col >= row).astype(BF16)
    for s in range(xb.shape[0] // T):
        cf_ref[s * T:(s + 1) * T, :] = _dot_exact_lhs(tri_f, laf[s * T:(s + 1) * T, :])
        cb_ref[s * T:(s + 1) * T, :] = _dot_exact_lhs(tri_b, lab[s * T:(s + 1) * T, :])


def _gla_decays(x2d, w_in_t, ab_col, wa2_f, ba_f, wa2_b, ba_b, block):
    M, D = x2d.shape
    rank, KEY = wa2_f.shape
    P = V7X_LANES
    assert 2 * rank <= P and ab_col % P == 0
    w2f = jnp.pad(wa2_f, ((0, P - rank), (0, 0)))
    w2b = jnp.pad(wa2_b, ((rank, P - 2 * rank), (0, 0)))
    bm = max(_blk(M, 512), block)
    full = lambda shape: pl.BlockSpec(shape, lambda i: (0, 0))
    out = jax.ShapeDtypeStruct((M, KEY), F32)
    return _call(
        functools.partial(_decay_body, block=block),
        grid=(M // bm,),
        in_specs=[pl.BlockSpec((bm, D), lambda i: (i, 0)), pl.BlockSpec((P, D), lambda i: (ab_col // P, 0)),
                  full((P, KEY)), full((1, KEY)), full((P, KEY)), full((1, KEY))],
        out_specs=[pl.BlockSpec((bm, KEY), lambda i: (i, 0))] * 2 + [pl.BlockSpec((bm, D), lambda i: (i, 0))],
        out_shape=[out, out, jax.ShapeDtypeStruct((M, D), BF16)],
        name="gla_decays",
    )(x2d, w_in_t, w2f, ba_f.reshape(1, KEY), w2b, ba_b.reshape(1, KEY))


def _gla_body(q_ref, k_ref, v_ref, r_ref, cf_ref, cb_ref, g_ref, y_ref, o_acc,
              stf_ref, sf_ref, qbf_ref, kef_ref, klf_ref,
              stb_ref, sb_ref, qbb_ref, keb_ref, klb_ref, *, block, sub):
    L, DK = q_ref.shape
    T, C = block, sub
    n, ns = L // T, T // C
    scale = DK ** -0.5
    row = lax.broadcasted_iota(jnp.int32, (C, C), 0)
    col = lax.broadcasted_iota(jnp.int32, (C, C), 1)

    def boundary_row(cum_ref, start, pick):
        return cum_ref[pl.ds(pl.multiple_of(start, 8), 8), :][pick:pick + 1, :]

    def block_step(blk, cum_ref, st_ref, s_ref, qb_ref, ke_ref, kl_ref, forward, first_visit):
        r0 = pl.multiple_of(blk * T, T)
        tot = boundary_row(cum_ref, r0 + T - 8, 7) if forward else boundary_row(cum_ref, r0, 0)
        mask = (col <= row) if forward else (col > row)
        refs = {}
        for I in (range(ns) if forward else range(ns - 1, -1, -1)):
            sl = slice(I * C, (I + 1) * C)
            rows = pl.ds(r0 + I * C, C)
            if forward:
                ref = boundary_row(cum_ref, r0 + I * C - 8, 7) if I > 0 else jnp.zeros_like(tot)
            else:
                ref = boundary_row(cum_ref, r0 + (I + 1) * C, 0) if I < ns - 1 else jnp.zeros_like(tot)
            d = cum_ref[rows, :] - ref
            q_loc = q_ref[rows, :] * (scale * jnp.exp(d))
            k_loc = k_ref[rows, :] * jnp.exp(-d)
            qb_ref[sl, :] = (q_loc * jnp.exp(ref)).astype(BF16)
            ke_ref[sl, :] = (k_loc * jnp.exp(tot - ref)).astype(BF16)
            k_loc = k_loc.astype(BF16)
            kl_ref[sl, :] = k_loc
            s_ref[sl, sl] = jnp.where(mask, _dot(q_loc, k_loc, NT), 0.0).astype(BF16)
            for J, ref_j in refs.items():
                sj = slice(J * C, (J + 1) * C)
                s_ref[sl, sj] = _dot(q_loc * jnp.exp(ref - ref_j), kl_ref[sj, :], NT).astype(BF16)
            refs[I] = ref
        rows = pl.ds(r0, T)
        vb = v_ref[rows, :].astype(BF16)
        st = st_ref[...]
        o = _dot(s_ref[...], vb) + _dot(qb_ref[...], st, NT)
        st_ref[...] = st * jnp.exp(tot) + _dot(vb, ke_ref[...], TN)
        if first_visit:
            o_acc[rows, :] = o
        else:
            o = o_acc[rows, :] + o
            o = o * lax.rsqrt(jnp.mean(o * o, axis=-1, keepdims=True) + LN_EPS) * g_ref[...]
            r = r_ref[rows, :].astype(F32)
            y_ref[rows, :] = (o * (r * jax.nn.sigmoid(r))).astype(y_ref.dtype)

    fwd = functools.partial(block_step, cum_ref=cf_ref, st_ref=stf_ref, s_ref=sf_ref, qb_ref=qbf_ref,
                            ke_ref=kef_ref, kl_ref=klf_ref, forward=True)
    bwd = functools.partial(block_step, cum_ref=cb_ref, st_ref=stb_ref, s_ref=sb_ref, qb_ref=qbb_ref,
                            ke_ref=keb_ref, kl_ref=klb_ref, forward=False)

    for ref in (stf_ref, stb_ref, sf_ref, sb_ref):
        ref[...] = jnp.zeros_like(ref)

    def sweep(lo, hi, step):
        def body(i, carry):
            step(i)
            return carry
        lax.fori_loop(lo, hi, body, 0)

    if n % 2 == 0:
        def first_half(i):
            fwd(i, first_visit=True)
            bwd(n - 1 - i, first_visit=True)

        def second_half(i):
            fwd(i, first_visit=False)
            bwd(n - 1 - i, first_visit=False)

        sweep(0, n // 2, first_half)
        sweep(n // 2, n, second_half)
    else:
        sweep(0, n, lambda i: fwd(i, first_visit=True))
        sweep(0, n, lambda i: bwd(n - 1 - i, first_visit=False))


def _gla(p_qkvr, cum_f, cum_b, norm_g, B, L, KEY, VAL, block):
    H = GLA_HEADS
    DK, DV = KEY // H, VAL // H
    M = B * L
    kq = KEY // DK
    vq = 2 * KEY // DV
    rq = (2 * KEY + VAL) // DV
    T = block
    direction_scratch = [pltpu.VMEM((DV, DK), F32), pltpu.VMEM((T, T), BF16), pltpu.VMEM((T, DK), BF16),
                         pltpu.VMEM((T, DK), BF16), pltpu.VMEM((T, DK), BF16)]
    return _call(
        functools.partial(_gla_body, block=T, sub=_blk(T, GLA_CHUNK)),
        grid=(B, H),
        in_specs=[pl.BlockSpec((L, DK), lambda b, h: (b, h)),
                  pl.BlockSpec((L, DK), lambda b, h: (b, kq + h)),
                  pl.BlockSpec((L, DV), lambda b, h: (b, vq + h)),
                  pl.BlockSpec((L, DV), lambda b, h: (b, rq + h)),
                  pl.BlockSpec((L, DK), lambda b, h: (b, h)),
                  pl.BlockSpec((L, DK), lambda b, h: (b, h)),
                  pl.BlockSpec((1, DV), lambda b, h: (0, h))],
        out_specs=pl.BlockSpec((L, DV), lambda b, h: (b, h)),
        out_shape=jax.ShapeDtypeStruct((M, VAL), BF16),
        scratch_shapes=[pltpu.VMEM((L, DV), F32)] + direction_scratch * 2,
        name="gla_bidir",
    )(p_qkvr, p_qkvr, p_qkvr, p_qkvr, cum_f, cum_b, norm_g.reshape(1, VAL))


def _filter_body(emb_ref, w1_ref, b1_ref, w2_ref, b2_ref, w3_ref, b3_ref, fr_ref,
                 w4f_ref, b4f_ref, w4b_ref, b4b_ref, dl_ref, a_ref, bm_ref, ny_ref, h_ref):
    @pl.when((pl.program_id(0) == 0) & (pl.program_id(1) == 0))
    def _():
        fr = fr_ref[...]
        h = jnp.sin(fr * (_dot_f32(w1_ref[...], emb_ref[...]) + b1_ref[...]))
        h = jnp.sin(fr * (_dot_f32(w2_ref[...], h) + b2_ref[...]))
        h_ref[...] = jnp.sin(fr * (_dot_f32(w3_ref[...], h) + b3_ref[...]))

    h = h_ref[...]
    t_lin = emb_ref[0:1, :]
    decay = jnp.exp(-t_lin * dl_ref[...])
    hf = (_dot_f32(w4f_ref[...], h) + b4f_ref[...]) * decay
    hb = (_dot_f32(w4b_ref[...], h) + b4b_ref[...]) * decay
    t = lax.broadcasted_iota(jnp.int32, hf.shape, 1)
    hb = jnp.where(t == 0, 0.0, hb)
    a = hf + hb
    a_ref[...] = a.astype(a_ref.dtype)
    bm_ref[...] = (hf - hb).astype(bm_ref.dtype)
    sign = jnp.where((t & 1) == 0, 1.0, -1.0)
    ny_ref[...] = jnp.sum(a * sign, axis=-1, keepdims=True)


def _hyena_filters(emb_t, w1, b1, w2, b2, w3, b3, w4, b4, freq, deltas, W):
    L = emb_t.shape[1]
    HID = w1.shape[1]
    n_ord = w4.shape[1] // (2 * W)
    EMB = -(-emb_t.shape[0] // V7X_LANES) * V7X_LANES
    w1 = jnp.pad(w1, ((0, EMB - w1.shape[0]), (0, 0)))
    emb_t = jnp.pad(emb_t, ((0, EMB - emb_t.shape[0]), (0, 0)))
    cb = _blk(W, 512)
    ncb = W // cb
    w4t = w4.T
    b4c = b4.reshape(-1, 1)
    colv = lambda v: v.reshape(-1, 1)
    full = lambda shape: pl.BlockSpec(shape, lambda o, c: (0, 0))
    rows = n_ord * W
    return _call(
        _filter_body,
        grid=(n_ord, ncb),
        in_specs=[full((EMB, L)), full((HID, EMB)), full((HID, 1)), full((HID, HID)), full((HID, 1)),
                  full((HID, HID)), full((HID, 1)), full((HID, 1)),
                  pl.BlockSpec((cb, HID), lambda o, c: (o * 2 * ncb + c, 0)),
                  pl.BlockSpec((cb, 1), lambda o, c: (o * 2 * ncb + c, 0)),
                  pl.BlockSpec((cb, HID), lambda o, c: (o * 2 * ncb + ncb + c, 0)),
                  pl.BlockSpec((cb, 1), lambda o, c: (o * 2 * ncb + ncb + c, 0)),
                  pl.BlockSpec((cb, 1), lambda o, c: (c, 0))],
        out_specs=[pl.BlockSpec((cb, L), lambda o, c: (o * ncb + c, 0)),
                   pl.BlockSpec((cb, L), lambda o, c: (o * ncb + c, 0)),
                   pl.BlockSpec((cb, 1), lambda o, c: (o * ncb + c, 0))],
        out_shape=[jax.ShapeDtypeStruct((rows, L), BF16), jax.ShapeDtypeStruct((rows, L), BF16),
                   jax.ShapeDtypeStruct((rows, 1), F32)],
        scratch_shapes=[pltpu.VMEM((HID, L), F32)],
        name="hy_filters",
    )(emb_t, w1.T, colv(b1), w2.T, colv(b2), w3.T, colv(b3), colv(freq), w4t, b4c, w4t, b4c, deltas)


def _spec_body(a_ref, f_ref, s_ref, o_ref):
    o_ref[...] = _dot(a_ref[...], f_ref[...]) * s_ref[...]


def _filter_spectrum(a, fmat, scale):
    R, L = a.shape
    bm, bn = _blk(R, 1024), _blk(L, 1024)
    return _call(
        _spec_body,
        grid=(R // bm, L // bn),
        in_specs=[pl.BlockSpec((bm, L), lambda i, j: (i, 0)),
                  pl.BlockSpec((L, bn), lambda i, j: (0, j)),
                  pl.BlockSpec((1, bn), lambda i, j: (0, j))],
        out_specs=pl.BlockSpec((bm, bn), lambda i, j: (i, j)),
        out_shape=jax.ShapeDtypeStruct((R, L), F32),
        name="hy_filter_spectrum",
    )(a, fmat, scale)


def _fwd_dft_body(z_ref, c_ref, s_ref, hr_ref, hi_ref, hn_ref, pr_ref, pi_ref):
    z = z_ref[...]
    zr = _dot(z, c_ref[...])
    zi = _dot(z, s_ref[...])
    hr, hi = hr_ref[...], hi_ref[...]
    k = lax.broadcasted_iota(jnp.int32, zr.shape, 1) + pl.program_id(1) * zr.shape[1]
    dc = (k == 0)
    pr = zr * hr - jnp.where(dc, 0.0, zi * hi)
    pi = jnp.where(dc, zi * hn_ref[...], zr * hi + zi * hr)
    pr_ref[...] = pr.astype(pr_ref.dtype)
    pi_ref[...] = pi.astype(pi_ref.dtype)


def _fwd_dft(z2d, z_row0, z_rows_per_b, cmat, smat, hr, hi, hn, h_row0, B, W, L):
    bm, bn = _blk(W, 1024), _blk(L, 512)
    ncb = W // bm
    zb, z0, h0 = z_rows_per_b // bm, z_row0 // bm, h_row0 // bm
    out = jax.ShapeDtypeStruct((B * W, L), BF16)
    return _call(
        _fwd_dft_body,
        grid=(ncb, L // bn, B),
        in_specs=[pl.BlockSpec((bm, L), lambda c, j, b: (b * zb + z0 + c, 0)),
                  pl.BlockSpec((L, bn), lambda c, j, b: (0, j)),
                  pl.BlockSpec((L, bn), lambda c, j, b: (0, j)),
                  pl.BlockSpec((bm, bn), lambda c, j, b: (h0 + c, j)),
                  pl.BlockSpec((bm, bn), lambda c, j, b: (h0 + c, j)),
                  pl.BlockSpec((bm, 1), lambda c, j, b: (h0 + c, 0))],
        out_specs=[pl.BlockSpec((bm, bn), lambda c, j, b: (b * ncb + c, j))] * 2,
        out_shape=[out, out],
        name="hy_fwd_dft",
    )(z2d, cmat, smat, hr, hi, hn)


def _inv_dft_body(pr_ref, pi_ref, c_ref, st_ref, gate_ref, z_ref, skip_ref, o_ref):
    zc = _dot(pr_ref[...], c_ref[...]) + _dot(pi_ref[...], st_ref[...])
    o_ref[...] = (gate_ref[...] * (zc + skip_ref[...] * z_ref[...])).astype(o_ref.dtype)


def _inv_dft(pr, pi, cmat, smat_t, gates2d, g_row0, g_rows_per_b, z2d, z_row0, z_rows_per_b,
             skip_col, s_row0, B, W, L, out_dtype):
    bm, bn = _blk(W, 1024), _blk(L, 1024)
    ncb = W // bm
    gb, g0 = g_rows_per_b // bm, g_row0 // bm
    zb, z0 = z_rows_per_b // bm, z_row0 // bm
    s0 = s_row0 // bm
    return _call(
        _inv_dft_body,
        grid=(B, ncb, L // bn),
        in_specs=[pl.BlockSpec((bm, L), lambda b, c, j: (b * ncb + c, 0)),
                  pl.BlockSpec((bm, L), lambda b, c, j: (b * ncb + c, 0)),
                  pl.BlockSpec((L, bn), lambda b, c, j: (0, j)),
                  pl.BlockSpec((L, bn), lambda b, c, j: (0, j)),
                  pl.BlockSpec((bm, bn), lambda b, c, j: (b * gb + g0 + c, j)),
                  pl.BlockSpec((bm, bn), lambda b, c, j: (b * zb + z0 + c, j)),
                  pl.BlockSpec((bm, 1), lambda b, c, j: (s0 + c, 0))],
        out_specs=pl.BlockSpec((bm, bn), lambda b, c, j: (b * ncb + c, j)),
        out_shape=jax.ShapeDtypeStruct((B * W, L), out_dtype),
        name="hy_inv_dft",
    )(pr, pi, cmat, smat_t, gates2d, z2d, skip_col)


def _merge_body(yg_ref, yh_ref, wg_ref, wh_ref, g0_ref, g1_ref, o_ref):
    tg = _dot(yg_ref[...], wg_ref[...])
    th = _dot(yh_ref[...], wh_ref[...], TN)
    o_ref[...] = (g0_ref[...] * tg + g1_ref[...] * th).astype(o_ref.dtype)


def _branch_merge(y_gla, y_hy_t, w_gla_o, w_hy_o, gates, B, L, D):
    VAL, W = y_gla.shape[1], w_hy_o.shape[0]
    bt = _blk(L, 512)
    nt = L // bt
    resident = lambda shape: pl.BlockSpec(shape, lambda b, t: (0, 0), pipeline_mode=pl.Buffered(1))
    return _call(
        _merge_body,
        grid=(B, nt),
        in_specs=[pl.BlockSpec((bt, VAL), lambda b, t: (b * nt + t, 0)),
                  pl.BlockSpec((W, bt), lambda b, t: (b, t)),
                  resident((VAL, D)), resident((W, D)),
                  pl.BlockSpec((bt, D), lambda b, t: (b * nt + t, 0)),
                  pl.BlockSpec((bt, D), lambda b, t: (b * nt + t, 1))],
        out_specs=pl.BlockSpec((bt, D), lambda b, t: (b * nt + t, 0)),
        out_shape=jax.ShapeDtypeStruct((B * L, D), BF16),
        name="branch_merge",
    )(y_gla, y_hy_t, w_gla_o, w_hy_o, gates, gates)


def _proj_ln_body(a_ref, w_ref, x_ref, g_ref, b_ref, o_ref, ob_ref, *, alpha):
    y = alpha * x_ref[...] + _dot(a_ref[...], w_ref[...])
    h = _layernorm_rows(y, g_ref[...], b_ref[...])
    o_ref[...] = h
    ob_ref[...] = h.astype(ob_ref.dtype)


def _proj_ln(a, w, x2d, ln_g, ln_b, alpha):
    M, K = a.shape
    D = w.shape[1]
    bm = _blk(M, 512)
    row = lambda i: (i, 0)
    full = lambda shape: pl.BlockSpec(shape, lambda i: (0, 0))
    return _call(
        functools.partial(_proj_ln_body, alpha=alpha),
        grid=(M // bm,),
        in_specs=[pl.BlockSpec((bm, K), row), full((K, D)), pl.BlockSpec((bm, D), row),
                  full((1, D)), full((1, D))],
        out_specs=[pl.BlockSpec((bm, D), row)] * 2,
        out_shape=[jax.ShapeDtypeStruct((M, D), F32), jax.ShapeDtypeStruct((M, D), BF16)],
        name="out_proj_ln1",
    )(a, w, x2d, ln_g.reshape(1, D), ln_b.reshape(1, D))


def _ff2_ln_body(a_ref, w_ref, h_ref, g_ref, b_ref, o_ref, *, alpha, ln_rows):
    kk = pl.program_id(1)

    @pl.when(kk == 0)
    def _():
        o_ref[...] = jnp.zeros_like(o_ref)

    o_ref[...] += _dot(a_ref[...], w_ref[...])

    @pl.when(kk == pl.num_programs(1) - 1)
    def _():
        g, b = g_ref[...], b_ref[...]

        def norm_rows(r, carry):
            rows = pl.ds(pl.multiple_of(r * ln_rows, ln_rows), ln_rows)
            y = alpha * h_ref[rows, :] + o_ref[rows, :]
            o_ref[rows, :] = _layernorm_rows(y, g, b)
            return carry

        lax.fori_loop(0, o_ref.shape[0] // ln_rows, norm_rows, 0)


def _ff2_ln(a, w, h, ln_g, ln_b, alpha):
    M, K = a.shape
    D = w.shape[1]
    bm, bk = _blk(M, 1024), _blk(K, 1024)
    return _call(
        functools.partial(_ff2_ln_body, alpha=alpha, ln_rows=_blk(bm, 128)),
        grid=(M // bm, K // bk),
        in_specs=[pl.BlockSpec((bm, bk), lambda i, k: (i, k)),
                  pl.BlockSpec((bk, D), lambda i, k: (k, 0)),
                  pl.BlockSpec((bm, D), lambda i, k: (i, 0)),
                  pl.BlockSpec((1, D), lambda i, k: (0, 0)),
                  pl.BlockSpec((1, D), lambda i, k: (0, 0))],
        out_specs=pl.BlockSpec((bm, D), lambda i, k: (i, 0)),
        out_shape=jax.ShapeDtypeStruct((M, D), F32),
        name="ff2_ln2",
    )(a, w, h, ln_g.reshape(1, D), ln_b.reshape(1, D))


def _dft_tables(L):
    LO = _blk(L, 32)
    k = jnp.arange(L, dtype=jnp.int32)
    ang = lambda m: ((m[:, None] * k[None, :]) % (2 * L)).astype(F32) * (math.pi / L)
    a_hi = ang(jnp.arange(L // LO, dtype=jnp.int32) * LO)[:, None, :]
    a_lo = ang(jnp.arange(LO, dtype=jnp.int32))[None, :, :]
    cos_t = (jnp.cos(a_hi) * jnp.cos(a_lo) - jnp.sin(a_hi) * jnp.sin(a_lo)).reshape(L, L)
    sin_t = (jnp.sin(a_hi) * jnp.cos(a_lo) + jnp.cos(a_hi) * jnp.sin(a_lo)).reshape(L, L)
    nyq = jnp.where(k % 2 == 0, 1.0, -1.0).astype(F32)
    smat = jnp.where(k[None, :] == 0, nyq[:, None], sin_t)
    smat_t = jnp.where(k[:, None] == 0, nyq[None, :], sin_t)
    return cos_t.astype(BF16), smat.astype(BF16), smat_t.astype(BF16)


def _position_features(L, emb_dim):
    t = jnp.linspace(0.0, 1.0, L, dtype=F32)[:, None]
    bands = (emb_dim - 1) // 2
    f = jnp.linspace(1e-4, bands - 1, bands, dtype=F32)
    wpos = 2.0 * math.pi * jnp.arange(L, dtype=F32) / L
    ang = wpos[:, None] * f[None, :]
    return jnp.concatenate([t, jnp.cos(ang), -jnp.sin(ang)], axis=-1).T


def _layer(h, w_in, gla_wa2_f, gla_ba_f, gla_wa2_b, gla_ba_b, gla_norm_g, w_gla_o,
           hy_conv_w, hy_conv_b, hy_w1, hy_b1, hy_w2, hy_b2, hy_w3, hy_b3, hy_w4, hy_b4,
           hy_freq, hy_skip, w_hy_o, w_out, ln1_g, ln1_b, w_ff1, w_ff2, ln2_g, ln2_b, alpha):
    B, L, D = h.shape
    M = B * L
    rank, KEY = gla_wa2_f.shape
    VAL = gla_norm_g.shape[0]
    n_ord, W = hy_skip.shape

    sizes = (KEY, KEY, VAL, VAL, rank, rank, (n_ord + 1) * W, 2 * D)
    offs = [int(v) for v in np.concatenate([[0], np.cumsum(sizes)])]
    w_in_t = w_in.T
    w_hy_t = w_in_t[offs[6]:offs[7]].astype(BF16)
    w_gate_t = w_in_t[offs[7]:offs[8]].astype(BF16)

    x2d = h.reshape(M, D)
    gla_block = _blk(L, GLA_BLOCK)
    cum_f, cum_b, xb = _gla_decays(x2d, w_in_t, offs[4], gla_wa2_f, gla_ba_f, gla_wa2_b, gla_ba_b, gla_block)

    p_qkvr = _matmul_f32w(xb, w_in_t, offs[4], out_dtype=BF16, name="in_proj_qkvr", bm=2048, transposed=True)
    gates = _matmul(xb, w_gate_t, out_dtype=BF16, act="sigmoid", name="in_proj_gates", bm=2048, transposed=True)
    conv_params = jnp.concatenate([hy_conv_w.T, hy_conv_b[:, None]], axis=1)
    u_t = _hy_proj(w_hy_t, xb, conv_params, B, L)

    y_gla = _gla(p_qkvr, cum_f, cum_b, gla_norm_g, B, L, KEY, VAL, gla_block)

    emb_t = _position_features(L, hy_w1.shape[0])
    min_decay = math.log(HY_DECAY_TARGET) / HY_SLOW_DECAY
    max_decay = math.log(HY_DECAY_TARGET) / HY_FAST_DECAY
    deltas = jnp.abs(jnp.linspace(min_decay, max_decay, W, dtype=F32)).reshape(W, 1)
    a_even, a_odd, nyq = _hyena_filters(emb_t, hy_w1, hy_b1, hy_w2, hy_b2, hy_w3, hy_b3,
                                        hy_w4, hy_b4, hy_freq, deltas, W)
    cmat, smat, smat_t = _dft_tables(L)
    bin_w = jnp.full((1, L), 1.0 / L, F32).at[0, 0].set(0.5 / L)
    h_re = _filter_spectrum(a_even, cmat, bin_w)
    h_im = _filter_spectrum(a_odd, smat, bin_w)
    h_ny = nyq * (0.5 / L)

    skip_col = hy_skip.reshape(n_ord * W, 1)
    z, z_row0, z_rows = u_t, 0, (n_ord + 1) * W
    for o in range(n_ord):
        pr, pi = _fwd_dft(z, z_row0, z_rows, cmat, smat, h_re, h_im, h_ny, o * W, B, W, L)
        z = _inv_dft(pr, pi, cmat, smat_t, u_t, (o + 1) * W, (n_ord + 1) * W, z, z_row0, z_rows,
                     skip_col, o * W, B, W, L, BF16)
        z_row0, z_rows = 0, W
    y_hy_t = z

    merged = _branch_merge(y_gla, y_hy_t, w_gla_o.astype(BF16), w_hy_o.astype(BF16), gates, B, L, D)
    h1, h1b = _proj_ln(merged, w_out.astype(BF16), x2d, ln1_g, ln1_b, alpha)
    act = _matmul_f32w(h1b, w_ff1, w_ff1.shape[1], out_dtype=BF16, act="relu2", name="ff1_relu2")
    out = _ff2_ln(act, w_ff2.astype(BF16), h1, ln2_g, ln2_b, alpha)
    return out.reshape(B, L, D)


def kernel(x, w_in, gla_wa2_f, gla_ba_f, gla_wa2_b, gla_ba_b, gla_norm_g, w_gla_o, hy_conv_w, hy_conv_b, hy_w1, hy_b1, hy_w2, hy_b2, hy_w3, hy_b3, hy_w4, hy_b4, hy_freq, hy_skip, w_hy_o, w_out, ln1_g, ln1_b, w_ff1, w_ff2, ln2_g, ln2_b):
    depth = w_in.shape[0]
    alpha = (2 * depth) ** 0.25
    params = (w_in, gla_wa2_f, gla_ba_f, gla_wa2_b, gla_ba_b, gla_norm_g, w_gla_o, hy_conv_w, hy_conv_b,
              hy_w1, hy_b1, hy_w2, hy_b2, hy_w3, hy_b3, hy_w4, hy_b4, hy_freq, hy_skip, w_hy_o, w_out,
              ln1_g, ln1_b, w_ff1, w_ff2, ln2_g, ln2_b)
    h = x
    for l in range(depth):
        h = _layer(h, *(p[l] for p in params), alpha)
    return h
```

```python
import functools
import math

import jax
import jax.numpy as jnp
import numpy as np
from jax import lax
from jax.experimental import pallas as pl
from jax.experimental.pallas import tpu as pltpu

F32 = jnp.float32
BF16 = jnp.bfloat16

GLA_HEADS = 4
GLA_TAU = 16.0
GLA_CHUNK = 64
GLA_BLOCK = 256
HY_BLOCK = 512
HY_FAST_DECAY = 0.3
HY_SLOW_DECAY = 1.5
HY_DECAY_TARGET = 1e-2
LN_EPS = 1e-5

V7X_VMEM_LIMIT_BYTES = 56 * 1024 * 1024
V7X_LANES = 128

NN = (((1,), (0,)), ((), ()))
NT = (((1,), (1,)), ((), ()))
TN = (((0,), (0,)), ((), ()))


def _dot(a, b, dims=NN):
    return lax.dot_general(a.astype(BF16), b.astype(BF16), dims, preferred_element_type=F32)


def _split2(a):
    h1 = a.astype(BF16)
    h2 = (a - h1.astype(F32)).astype(BF16)
    return h1, h2


def _dot_f32(a, b, dims=NN):
    a1, a2 = _split2(a)
    b1, b2 = _split2(b)
    d = lambda x, y: lax.dot_general(x, y, dims, preferred_element_type=F32)
    return (d(a2, b1) + d(a1, b2)) + d(a1, b1)


def _dot_exact_lhs(a_bf16, b, dims=NN):
    b1, b2 = _split2(b)
    d = lambda y: lax.dot_general(a_bf16, y, dims, preferred_element_type=F32)
    return d(b2) + d(b1)


def _call(body, *, grid, in_specs, out_specs, out_shape, name, scratch_shapes=()):
    return pl.pallas_call(
        body,
        grid=grid,
        in_specs=in_specs,
        out_specs=out_specs,
        out_shape=out_shape,
        scratch_shapes=scratch_shapes,
        compiler_params=pltpu.CompilerParams(
            dimension_semantics=("arbitrary",) * len(grid),
            vmem_limit_bytes=V7X_VMEM_LIMIT_BYTES,
        ),
        name=name,
    )


def _blk(n, want):
    b = min(n, want)
    while n % b:
        b //= 2
    return b


def _layernorm_rows(y, g, b):
    mu = jnp.mean(y, axis=-1, keepdims=True)
    d = y - mu
    var = jnp.mean(d * d, axis=-1, keepdims=True)
    return d * lax.rsqrt(var + LN_EPS) * g + b


def _act(acc, act):
    if act == "sigmoid":
        return jax.nn.sigmoid(acc)
    if act == "relu2":
        return jnp.square(jnp.maximum(acc, 0.0))
    return acc


def _mm_body(a_ref, b_ref, o_ref, *, act, dims):
    o_ref[...] = _act(_dot(a_ref[...], b_ref[...], dims), act).astype(o_ref.dtype)


def _weight_spec(K, bn, transposed, index):
    if transposed:
        return pl.BlockSpec((bn, K), lambda *g: (index(*g), 0))
    return pl.BlockSpec((K, bn), lambda *g: (0, index(*g)))


def _matmul(a, b, *, out_dtype, act=None, name, bm=1024, bn=1024, transposed=False):
    M, K = a.shape
    N = b.shape[0] if transposed else b.shape[1]
    bm, bn = _blk(M, bm), _blk(N, bn)
    return _call(
        functools.partial(_mm_body, act=act, dims=NT if transposed else NN),
        grid=(M // bm, N // bn),
        in_specs=[pl.BlockSpec((bm, K), lambda i, j: (i, 0)),
                  _weight_spec(K, bn, transposed, lambda i, j: j)],
        out_specs=pl.BlockSpec((bm, bn), lambda i, j: (i, j)),
        out_shape=jax.ShapeDtypeStruct((M, N), out_dtype),
        name=name,
    )(a, b)


def _mm_f32w_body(a_ref, w_ref, o_ref, wb_ref, *, act, dims):
    @pl.when(pl.program_id(1) == 0)
    def _():
        wb_ref[...] = w_ref[...].astype(wb_ref.dtype)

    o_ref[...] = _act(_dot(a_ref[...], wb_ref[...], dims), act).astype(o_ref.dtype)


def _matmul_f32w(a, w, n_out, *, out_dtype, act=None, name, bm=1024, bn=1024, transposed=False):
    M, K = a.shape
    bm, bn = _blk(M, bm), _blk(n_out, bn)
    return _call(
        functools.partial(_mm_f32w_body, act=act, dims=NT if transposed else NN),
        grid=(n_out // bn, M // bm),
        in_specs=[pl.BlockSpec((bm, K), lambda j, i: (i, 0)),
                  _weight_spec(K, bn, transposed, lambda j, i: j)],
        out_specs=pl.BlockSpec((bm, bn), lambda j, i: (i, j)),
        out_shape=jax.ShapeDtypeStruct((M, n_out), out_dtype),
        scratch_shapes=[pltpu.VMEM((bn, K) if transposed else (K, bn), BF16)],
        name=name,
    )(a, w)


def _hy_proj_body(w_ref, x_ref, cp_ref, o_ref):
    u = _dot(w_ref[...], x_ref[...], NT)
    L = u.shape[1]
    t = lax.broadcasted_iota(jnp.int32, u.shape, 1)
    prev = jnp.where(t == 0, 0.0, pltpu.roll(u, 1, axis=1))
    nxt = jnp.where(t == L - 1, 0.0, pltpu.roll(u, L - 1, axis=1))
    cp = cp_ref[...]
    o_ref[...] = (cp[:, 0:1] * prev + cp[:, 1:2] * u + cp[:, 2:3] * nxt + cp[:, 3:4]).astype(o_ref.dtype)


def _hy_proj(w_t, xb, conv_params, B, L):
    C3, D = w_t.shape
    bc = _blk(C3, 512)
    nc = C3 // bc
    return _call(
        _hy_proj_body,
        grid=(B, nc),
        in_specs=[pl.BlockSpec((bc, D), lambda b, c: (c, 0)),
                  pl.BlockSpec((L, D), lambda b, c: (b, 0)),
                  pl.BlockSpec((bc, 4), lambda b, c: (c, 0))],
        out_specs=pl.BlockSpec((bc, L), lambda b, c: (b * nc + c, 0)),
        out_shape=jax.ShapeDtypeStruct((B * C3, L), BF16),
        name="hy_proj_conv",
    )(w_t, xb, conv_params)


def _log_sigmoid(z):
    return -(jnp.maximum(-z, 0.0) + jnp.log(1.0 + jnp.exp(-jnp.abs(z))))


def _decay_body(x_ref, wab_ref, w2f_ref, bf_ref, w2b_ref, bb_ref, cf_ref, cb_ref, xb_ref, *, block):
    xb = x_ref[...].astype(xb_ref.dtype)
    xb_ref[...] = xb
    ab = _dot(xb, wab_ref[...], NT)
    zf = _dot_f32(ab, w2f_ref[...]) + bf_ref[...]
    zb = _dot_f32(ab, w2b_ref[...]) + bb_ref[...]
    laf = _log_sigmoid(zf) * (1.0 / GLA_TAU)
    lab = _log_sigmoid(zb) * (1.0 / GLA_TAU)
    T = block
    row = lax.broadcasted_iota(jnp.int32, (T, T), 0)
    col = lax.broadcasted_iota(jnp.int32, (T, T), 1)
    tri_f = (col <= row).astype(BF16)
    tri_b = (col >= row).astype(BF16)
    for s in range(xb.shape[0] // T):
        cf_ref[s * T:(s + 1) * T, :] = _dot_exact_lhs(tri_f, laf[s * T:(s + 1) * T, :])
        cb_ref[s * T:(s + 1) * T, :] = _dot_exact_lhs(tri_b, lab[s * T:(s + 1) * T, :])


def _gla_decays(x2d, w_in_t, ab_col, wa2_f, ba_f, wa2_b, ba_b, block):
    M, D = x2d.shape
    rank, KEY = wa2_f.shape
    P = V7X_LANES
    assert 2 * rank <= P and ab_col % P == 0
    w2f = jnp.pad(wa2_f, ((0, P - rank), (0, 0)))
    w2b = jnp.pad(wa2_b, ((rank, P - 2 * rank), (0, 0)))
    bm = max(_blk(M, 512), block)
    full = lambda shape: pl.BlockSpec(shape, lambda i: (0, 0))
    out = jax.ShapeDtypeStruct((M, KEY), F32)
    return _call(
        functools.partial(_decay_body, block=block),
        grid=(M // bm,),
        in_specs=[pl.BlockSpec((bm, D), lambda i: (i, 0)), pl.BlockSpec((P, D), lambda i: (ab_col // P, 0)),
                  full((P, KEY)), full((1, KEY)), full((P, KEY)), full((1, KEY))],
        out_specs=[pl.BlockSpec((bm, KEY), lambda i: (i, 0))] * 2 + [pl.BlockSpec((bm, D), lambda i: (i, 0))],
        out_shape=[out, out, jax.ShapeDtypeStruct((M, D), BF16)],
        name="gla_decays",
    )(x2d, w_in_t, w2f, ba_f.reshape(1, KEY), w2b, ba_b.reshape(1, KEY))


def _gla_body(q_ref, k_ref, v_ref, r_ref, cf_ref, cb_ref, g_ref, y_ref, o_acc,
              stf_ref, sf_ref, qbf_ref, kef_ref, klf_ref,
              stb_ref, sb_ref, qbb_ref, keb_ref, klb_ref, *, block, sub):
    L, DK = q_ref.shape
    T, C = block, sub
    n, ns = L // T, T // C
    scale = DK ** -0.5
    row = lax.broadcasted_iota(jnp.int32, (C, C), 0)
    col = lax.broadcasted_iota(jnp.int32, (C, C), 1)

    def boundary_row(cum_ref, start, pick):
        return cum_ref[pl.ds(pl.multiple_of(start, 8), 8), :][pick:pick + 1, :]

    def block_step(blk, cum_ref, st_ref, s_ref, qb_ref, ke_ref, kl_ref, forward, first_visit):
        r0 = pl.multiple_of(blk * T, T)
        tot = boundary_row(cum_ref, r0 + T - 8, 7) if forward else boundary_row(cum_ref, r0, 0)
        mask = (col <= row) if forward else (col > row)
        refs = {}
        for I in (range(ns) if forward else range(ns - 1, -1, -1)):
            sl = slice(I * C, (I + 1) * C)
            rows = pl.ds(r0 + I * C, C)
            if forward:
                ref = boundary_row(cum_ref, r0 + I * C - 8, 7) if I > 0 else jnp.zeros_like(tot)
            else:
                ref = boundary_row(cum_ref, r0 + (I + 1) * C, 0) if I < ns - 1 else jnp.zeros_like(tot)
            d = cum_ref[rows, :] - ref
            q_loc = q_ref[rows, :] * (scale * jnp.exp(d))
            k_loc = k_ref[rows, :] * jnp.exp(-d)
            qb_ref[sl, :] = (q_loc * jnp.exp(ref)).astype(BF16)
            ke_ref[sl, :] = (k_loc * jnp.exp(tot - ref)).astype(BF16)
            k_loc = k_loc.astype(BF16)
            kl_ref[sl, :] = k_loc
            s_ref[sl, sl] = jnp.where(mask, _dot(q_loc, k_loc, NT), 0.0).astype(BF16)
            for J, ref_j in refs.items():
                sj = slice(J * C, (J + 1) * C)
                s_ref[sl, sj] = _dot(q_loc * jnp.exp(ref - ref_j), kl_ref[sj, :], NT).astype(BF16)
            refs[I] = ref
        rows = pl.ds(r0, T)
        vb = v_ref[rows, :].astype(BF16)
        st = st_ref[...]
        o = _dot(s_ref[...], vb) + _dot(qb_ref[...], st, NT)
        st_ref[...] = st * jnp.exp(tot) + _dot(vb, ke_ref[...], TN)
        if first_visit:
            o_acc[rows, :] = o
        else:
            o = o_acc[rows, :] + o
            o = o * lax.rsqrt(jnp.mean(o * o, axis=-1, keepdims=True) + LN_EPS) * g_ref[...]
            r = r_ref[rows, :].astype(F32)
            y_ref[rows, :] = (o * (r * jax.nn.sigmoid(r))).astype(y_ref.dtype)

    fwd = functools.partial(block_step, cum_ref=cf_ref, st_ref=stf_ref, s_ref=sf_ref, qb_ref=qbf_ref,
                            ke_ref=kef_ref, kl_ref=klf_ref, forward=True)
    bwd = functools.partial(block_step, cum_ref=cb_ref, st_ref=stb_ref, s_ref=sb_ref, qb_ref=qbb_ref,
                            ke_ref=keb_ref, kl_ref=klb_ref, forward=False)

    for ref in (stf_ref, stb_ref, sf_ref, sb_ref):
        ref[...] = jnp.zeros_like(ref)

    def sweep(lo, hi, step):
        def body(i, carry):
            step(i)
            return carry
        lax.fori_loop(lo, hi, body, 0)

    if n % 2 == 0:
        def first_half(i):
            fwd(i, first_visit=True)
            bwd(n - 1 - i, first_visit=True)

        def second_half(i):
            fwd(i, first_visit=False)
            bwd(n - 1 - i, first_visit=False)

        sweep(0, n // 2, first_half)
        sweep(n // 2, n, second_half)
    else:
        sweep(0, n, lambda i: fwd(i, first_visit=True))
        sweep(0, n, lambda i: bwd(n - 1 - i, first_visit=False))


def _gla(p_qkvr, cum_f, cum_b, norm_g, B, L, KEY, VAL, block):
    H = GLA_HEADS
    DK, DV = KEY // H, VAL // H
    M = B * L
    kq = KEY // DK
    vq = 2 * KEY // DV
    rq = (2 * KEY + VAL) // DV
    T = block
    direction_scratch = [pltpu.VMEM((DV, DK), F32), pltpu.VMEM((T, T), BF16), pltpu.VMEM((T, DK), BF16),
                         pltpu.VMEM((T, DK), BF16), pltpu.VMEM((T, DK), BF16)]
    return _call(
        functools.partial(_gla_body, block=T, sub=_blk(T, GLA_CHUNK)),
        grid=(B, H),
        in_specs=[pl.BlockSpec((L, DK), lambda b, h: (b, h)),
                  pl.BlockSpec((L, DK), lambda b, h: (b, kq + h)),
                  pl.BlockSpec((L, DV), lambda b, h: (b, vq + h)),
                  pl.BlockSpec((L, DV), lambda b, h: (b, rq + h)),
                  pl.BlockSpec((L, DK), lambda b, h: (b, h)),
                  pl.BlockSpec((L, DK), lambda b, h: (b, h)),
                  pl.BlockSpec((1, DV), lambda b, h: (0, h))],
        out_specs=pl.BlockSpec((L, DV), lambda b, h: (b, h)),
        out_shape=jax.ShapeDtypeStruct((M, VAL), BF16),
        scratch_shapes=[pltpu.VMEM((L, DV), F32)] + direction_scratch * 2,
        name="gla_bidir",
    )(p_qkvr, p_qkvr, p_qkvr, p_qkvr, cum_f, cum_b, norm_g.reshape(1, VAL))


def _filter_body(emb_ref, w1_ref, b1_ref, w2_ref, b2_ref, w3_ref, b3_ref, fr_ref,
                 w4f_ref, b4f_ref, w4b_ref, b4b_ref, dl_ref, hf_ref, hb_ref, h_ref):
    @pl.when((pl.program_id(0) == 0) & (pl.program_id(1) == 0))
    def _():
        fr = fr_ref[...]
        h = jnp.sin(fr * (_dot_f32(w1_ref[...], emb_ref[...]) + b1_ref[...]))
        h = jnp.sin(fr * (_dot_f32(w2_ref[...], h) + b2_ref[...]))
        h_ref[...] = jnp.sin(fr * (_dot_f32(w3_ref[...], h) + b3_ref[...]))

    h = h_ref[...]
    t_lin = emb_ref[0:1, :]
    decay = jnp.exp(-t_lin * dl_ref[...])
    hf_ref[...] = (_dot_f32(w4f_ref[...], h) + b4f_ref[...]) * decay
    hb_ref[...] = (_dot_f32(w4b_ref[...], h) + b4b_ref[...]) * decay


def _hyena_filters(emb_t, w1, b1, w2, b2, w3, b3, w4, b4, freq, deltas, W):
    L = emb_t.shape[1]
    HID = w1.shape[1]
    n_ord = w4.shape[1] // (2 * W)
    EMB = -(-emb_t.shape[0] // V7X_LANES) * V7X_LANES
    w1 = jnp.pad(w1, ((0, EMB - w1.shape[0]), (0, 0)))
    emb_t = jnp.pad(emb_t, ((0, EMB - emb_t.shape[0]), (0, 0)))
    cb = _blk(W, 512)
    ncb = W // cb
    w4t = w4.T
    b4c = b4.reshape(-1, 1)
    colv = lambda v: v.reshape(-1, 1)
    full = lambda shape: pl.BlockSpec(shape, lambda o, c: (0, 0))
    rows = n_ord * W
    return _call(
        _filter_body,
        grid=(n_ord, ncb),
        in_specs=[full((EMB, L)), full((HID, EMB)), full((HID, 1)), full((HID, HID)), full((HID, 1)),
                  full((HID, HID)), full((HID, 1)), full((HID, 1)),
                  pl.BlockSpec((cb, HID), lambda o, c: (o * 2 * ncb + c, 0)),
                  pl.BlockSpec((cb, 1), lambda o, c: (o * 2 * ncb + c, 0)),
                  pl.BlockSpec((cb, HID), lambda o, c: (o * 2 * ncb + ncb + c, 0)),
                  pl.BlockSpec((cb, 1), lambda o, c: (o * 2 * ncb + ncb + c, 0)),
                  pl.BlockSpec((cb, 1), lambda o, c: (c, 0))],
        out_specs=[pl.BlockSpec((cb, L), lambda o, c: (o * ncb + c, 0))] * 2,
        out_shape=[jax.ShapeDtypeStruct((rows, L), F32)] * 2,
        scratch_shapes=[pltpu.VMEM((HID, L), F32)],
        name="hy_filters",
    )(emb_t, w1.T, colv(b1), w2.T, colv(b2), w3.T, colv(b3), colv(freq), w4t, b4c, w4t, b4c, deltas)


def _lag_spectrum_body(hf_ref, hb_ref, c_ref, s_ref, w_ref, hr_ref, hi_ref, hn_ref, *, L, T):
    nb = L // T
    rows = hf_ref.shape[0]
    lane = lax.broadcasted_iota(jnp.int32, (rows, T), 1)
    alt = jnp.where((lane & 1) == 0, 1.0, -1.0)
    w = w_ref[...]

    def transforms(x_ref):
        blocks = [x_ref[:, n * T:(n + 1) * T] for n in range(nb)]
        return ([_dot(x, c_ref[...]) for x in blocks], [_dot(x, s_ref[...]) for x in blocks],
                [x[:, 0:1] for x in blocks])

    fc, fs, f0 = transforms(hf_ref)
    bc, bs, b0 = transforms(hb_ref)
    ny = lambda s: s[:, 0:1]
    for d in range(-(nb - 1), nb):
        if d >= 1:
            hr = fc[d] + alt * (fc[d - 1] - f0[d - 1])
            hi = fs[d] + alt * fs[d - 1]
            hn = ny(fs[d]) + ny(fs[d - 1]) - f0[d - 1]
        elif d == 0:
            hr = fc[0] + bc[0] - b0[0]
            hi = fs[0] - bs[0]
            hn = ny(fs[0]) + ny(bs[0]) - b0[0]
        else:
            e = -d
            hr = bc[e] + alt * (bc[e - 1] - b0[e - 1])
            hi = -(bs[e] + alt * bs[e - 1])
            hn = ny(bs[e]) + ny(bs[e - 1]) - b0[e - 1]
        li = d + nb - 1
        hr_ref[li] = hr * w
        hi_ref[li] = jnp.where(lane == 0, 0.0, hi * w)
        hn_ref[li] = hn * w[:, 0:1]


def _lag_spectra(hf2, hb2, cmat, smat, bin_w, L, T):
    R = hf2.shape[0]
    assert T % 2 == 0 and L % T == 0
    nl = 2 * (L // T) - 1
    rb = _blk(R, 256)
    full = lambda shape: pl.BlockSpec(shape, lambda r: (0, 0))
    return _call(
        functools.partial(_lag_spectrum_body, L=L, T=T),
        grid=(R // rb,),
        in_specs=[pl.BlockSpec((rb, L), lambda r: (r, 0)), pl.BlockSpec((rb, L), lambda r: (r, 0)),
                  full((T, T)), full((T, T)), full((1, T))],
        out_specs=[pl.BlockSpec((nl, rb, T), lambda r: (0, r, 0)),
                   pl.BlockSpec((nl, rb, T), lambda r: (0, r, 0)),
                   pl.BlockSpec((nl, rb, 1), lambda r: (0, r, 0))],
        out_shape=[jax.ShapeDtypeStruct((nl, R, T), F32), jax.ShapeDtypeStruct((nl, R, T), F32),
                   jax.ShapeDtypeStruct((nl, R, 1), F32)],
        name="hy_lag_spectra",
    )(hf2, hb2, cmat, smat, bin_w)


def _hy_conv_body(z_ref, gate_ref, skip_ref, c_ref, s_ref, st_ref, hr_ref, hi_ref, hn_ref, o_ref,
                  zr_ref, zi_ref, yr_ref, yi_ref, *, T, chunk):
    R, L = z_ref.shape
    nb = L // T
    for j in range(nb):
        zj = z_ref[:, j * T:(j + 1) * T]
        zr_ref[j] = _dot(zj, c_ref[...])
        zi_ref[j] = _dot(zj, s_ref[...])

    bin0 = lax.broadcasted_iota(jnp.int32, (chunk, T), 1) == 0

    def combine(r, carry):
        rows = pl.ds(pl.multiple_of(r * chunk, chunk), chunk)
        zr = [zr_ref[j, rows, :] for j in range(nb)]
        zi = [zi_ref[j, rows, :] for j in range(nb)]
        for i in range(nb):
            yr = yi = yn = 0.0
            for j in range(nb):
                lag = i - j + nb - 1
                hr, hi = hr_ref[lag, rows, :], hi_ref[lag, rows, :]
                yr = yr + (zr[j] * hr - zi[j] * hi)
                yi = yi + (zr[j] * hi + zi[j] * hr)
                yn = yn + zi[j][:, 0:1] * hn_ref[lag, rows, :]
            yr_ref[i, rows, :] = yr
            yi_ref[i, rows, :] = jnp.where(bin0, yn, yi)
        return carry

    lax.fori_loop(0, R // chunk, combine, 0)

    skip = skip_ref[...]
    for i in range(nb):
        sl = slice(i * T, (i + 1) * T)
        zc = _dot(yr_ref[i], c_ref[...]) + _dot(yi_ref[i], st_ref[...])
        o_ref[:, sl] = (gate_ref[:, sl] * (zc + skip * z_ref[:, sl])).astype(o_ref.dtype)


def _hy_conv(z2d, z_row0, z_rows_per_b, gates2d, g_row0, g_rows_per_b, skip_col, tables,
             hr, hi, hn, h_row0, B, W, L, T):
    R = _blk(W, 512)
    ncb = W // R
    nb, nl = L // T, hr.shape[0]
    zb, z0 = z_rows_per_b // R, z_row0 // R
    gb, g0 = g_rows_per_b // R, g_row0 // R
    h0 = h_row0 // R
    table = pl.BlockSpec((T, T), lambda c, b: (0, 0))
    spectrum = lambda last: pl.BlockSpec((nl, R, last), lambda c, b: (0, h0 + c, 0),
                                         pipeline_mode=pl.Buffered(1))
    return _call(
        functools.partial(_hy_conv_body, T=T, chunk=8),
        grid=(ncb, B),
        in_specs=[pl.BlockSpec((R, L), lambda c, b: (b * zb + z0 + c, 0)),
                  pl.BlockSpec((R, L), lambda c, b: (b * gb + g0 + c, 0)),
                  pl.BlockSpec((R, 1), lambda c, b: (h0 + c, 0)),
                  table, table, table, spectrum(T), spectrum(T), spectrum(1)],
        out_specs=pl.BlockSpec((R, L), lambda c, b: (b * ncb + c, 0)),
        out_shape=jax.ShapeDtypeStruct((B * W, L), BF16),
        scratch_shapes=[pltpu.VMEM((nb, R, T), F32)] * 4,
        name="hy_conv",
    )(z2d, gates2d, skip_col, *tables, hr, hi, hn)


def _merge_body(yg_ref, yh_ref, wg_ref, wh_ref, g0_ref, g1_ref, o_ref):
    tg = _dot(yg_ref[...], wg_ref[...])
    th = _dot(yh_ref[...], wh_ref[...], TN)
    o_ref[...] = (g0_ref[...] * tg + g1_ref[...] * th).astype(o_ref.dtype)


def _branch_merge(y_gla, y_hy_t, w_gla_o, w_hy_o, gates, B, L, D):
    VAL, W = y_gla.shape[1], w_hy_o.shape[0]
    bt = _blk(L, 512)
    nt = L // bt
    resident = lambda shape: pl.BlockSpec(shape, lambda b, t: (0, 0), pipeline_mode=pl.Buffered(1))
    return _call(
        _merge_body,
        grid=(B, nt),
        in_specs=[pl.BlockSpec((bt, VAL), lambda b, t: (b * nt + t, 0)),
                  pl.BlockSpec((W, bt), lambda b, t: (b, t)),
                  resident((VAL, D)), resident((W, D)),
                  pl.BlockSpec((bt, D), lambda b, t: (b * nt + t, 0)),
                  pl.BlockSpec((bt, D), lambda b, t: (b * nt + t, 1))],
        out_specs=pl.BlockSpec((bt, D), lambda b, t: (b * nt + t, 0)),
        out_shape=jax.ShapeDtypeStruct((B * L, D), BF16),
        name="branch_merge",
    )(y_gla, y_hy_t, w_gla_o, w_hy_o, gates, gates)


def _proj_ln_body(a_ref, w_ref, x_ref, g_ref, b_ref, o_ref, ob_ref, *, alpha):
    y = alpha * x_ref[...] + _dot(a_ref[...], w_ref[...])
    h = _layernorm_rows(y, g_ref[...], b_ref[...])
    o_ref[...] = h
    ob_ref[...] = h.astype(ob_ref.dtype)


def _proj_ln(a, w, x2d, ln_g, ln_b, alpha):
    M, K = a.shape
    D = w.shape[1]
    bm = _blk(M, 512)
    row = lambda i: (i, 0)
    full = lambda shape: pl.BlockSpec(shape, lambda i: (0, 0))
    return _call(
        functools.partial(_proj_ln_body, alpha=alpha),
        grid=(M // bm,),
        in_specs=[pl.BlockSpec((bm, K), row), full((K, D)), pl.BlockSpec((bm, D), row),
                  full((1, D)), full((1, D))],
        out_specs=[pl.BlockSpec((bm, D), row)] * 2,
        out_shape=[jax.ShapeDtypeStruct((M, D), F32), jax.ShapeDtypeStruct((M, D), BF16)],
        name="out_proj_ln1",
    )(a, w, x2d, ln_g.reshape(1, D), ln_b.reshape(1, D))


def _ff2_ln_body(a_ref, w_ref, h_ref, g_ref, b_ref, o_ref, *, alpha, ln_rows):
    kk = pl.program_id(1)

    @pl.when(kk == 0)
    def _():
        o_ref[...] = jnp.zeros_like(o_ref)

    o_ref[...] += _dot(a_ref[...], w_ref[...])

    @pl.when(kk == pl.num_programs(1) - 1)
    def _():
        g, b = g_ref[...], b_ref[...]

        def norm_rows(r, carry):
            rows = pl.ds(pl.multiple_of(r * ln_rows, ln_rows), ln_rows)
            y = alpha * h_ref[rows, :] + o_ref[rows, :]
            o_ref[rows, :] = _layernorm_rows(y, g, b)
            return carry

        lax.fori_loop(0, o_ref.shape[0] // ln_rows, norm_rows, 0)


def _ff2_ln(a, w, h, ln_g, ln_b, alpha):
    M, K = a.shape
    D = w.shape[1]
    bm, bk = _blk(M, 1024), _blk(K, 1024)
    return _call(
        functools.partial(_ff2_ln_body, alpha=alpha, ln_rows=_blk(bm, 128)),
        grid=(M // bm, K // bk),
        in_specs=[pl.BlockSpec((bm, bk), lambda i, k: (i, k)),
                  pl.BlockSpec((bk, D), lambda i, k: (k, 0)),
                  pl.BlockSpec((bm, D), lambda i, k: (i, 0)),
                  pl.BlockSpec((1, D), lambda i, k: (0, 0)),
                  pl.BlockSpec((1, D), lambda i, k: (0, 0))],
        out_specs=pl.BlockSpec((bm, D), lambda i, k: (i, 0)),
        out_shape=jax.ShapeDtypeStruct((M, D), F32),
        name="ff2_ln2",
    )(a, w, h, ln_g.reshape(1, D), ln_b.reshape(1, D))


def _dft_tables(L):
    LO = _blk(L, 32)
    k = jnp.arange(L, dtype=jnp.int32)
    ang = lambda m: ((m[:, None] * k[None, :]) % (2 * L)).astype(F32) * (math.pi / L)
    a_hi = ang(jnp.arange(L // LO, dtype=jnp.int32) * LO)[:, None, :]
    a_lo = ang(jnp.arange(LO, dtype=jnp.int32))[None, :, :]
    cos_t = (jnp.cos(a_hi) * jnp.cos(a_lo) - jnp.sin(a_hi) * jnp.sin(a_lo)).reshape(L, L)
    sin_t = (jnp.sin(a_hi) * jnp.cos(a_lo) + jnp.cos(a_hi) * jnp.sin(a_lo)).reshape(L, L)
    nyq = jnp.where(k % 2 == 0, 1.0, -1.0).astype(F32)
    smat = jnp.where(k[None, :] == 0, nyq[:, None], sin_t)
    smat_t = jnp.where(k[:, None] == 0, nyq[None, :], sin_t)
    return cos_t.astype(BF16), smat.astype(BF16), smat_t.astype(BF16)


def _position_features(L, emb_dim):
    t = jnp.linspace(0.0, 1.0, L, dtype=F32)[:, None]
    bands = (emb_dim - 1) // 2
    f = jnp.linspace(1e-4, bands - 1, bands, dtype=F32)
    wpos = 2.0 * math.pi * jnp.arange(L, dtype=F32) / L
    ang = wpos[:, None] * f[None, :]
    return jnp.concatenate([t, jnp.cos(ang), -jnp.sin(ang)], axis=-1).T


def _layer(h, w_in, gla_wa2_f, gla_ba_f, gla_wa2_b, gla_ba_b, gla_norm_g, w_gla_o,
           hy_conv_w, hy_conv_b, hy_w1, hy_b1, hy_w2, hy_b2, hy_w3, hy_b3, hy_w4, hy_b4,
           hy_freq, hy_skip, w_hy_o, w_out, ln1_g, ln1_b, w_ff1, w_ff2, ln2_g, ln2_b, alpha):
    B, L, D = h.shape
    M = B * L
    rank, KEY = gla_wa2_f.shape
    VAL = gla_norm_g.shape[0]
    n_ord, W = hy_skip.shape

    sizes = (KEY, KEY, VAL, VAL, rank, rank, (n_ord + 1) * W, 2 * D)
    offs = [int(v) for v in np.concatenate([[0], np.cumsum(sizes)])]
    w_in_t = w_in.T
    w_hy_t = w_in_t[offs[6]:offs[7]].astype(BF16)
    w_gate_t = w_in_t[offs[7]:offs[8]].astype(BF16)

    x2d = h.reshape(M, D)
    gla_block = _blk(L, GLA_BLOCK)
    cum_f, cum_b, xb = _gla_decays(x2d, w_in_t, offs[4], gla_wa2_f, gla_ba_f, gla_wa2_b, gla_ba_b, gla_block)

    p_qkvr = _matmul_f32w(xb, w_in_t, offs[4], out_dtype=BF16, name="in_proj_qkvr", bm=2048, transposed=True)
    gates = _matmul(xb, w_gate_t, out_dtype=BF16, act="sigmoid", name="in_proj_gates", bm=2048, transposed=True)
    conv_params = jnp.concatenate([hy_conv_w.T, hy_conv_b[:, None]], axis=1)
    u_t = _hy_proj(w_hy_t, xb, conv_params, B, L)

    y_gla = _gla(p_qkvr, cum_f, cum_b, gla_norm_g, B, L, KEY, VAL, gla_block)

    emb_t = _position_features(L, hy_w1.shape[0])
    min_decay = math.log(HY_DECAY_TARGET) / HY_SLOW_DECAY
    max_decay = math.log(HY_DECAY_TARGET) / HY_FAST_DECAY
    deltas = jnp.abs(jnp.linspace(min_decay, max_decay, W, dtype=F32)).reshape(W, 1)
    hf2, hb2 = _hyena_filters(emb_t, hy_w1, hy_b1, hy_w2, hy_b2, hy_w3, hy_b3,
                              hy_w4, hy_b4, hy_freq, deltas, W)
    T = _blk(L // 2, HY_BLOCK)
    tables = _dft_tables(T)
    bin_w = jnp.full((1, T), 1.0 / T, F32).at[0, 0].set(0.5 / T)
    h_re, h_im, h_ny = _lag_spectra(hf2, hb2, tables[0], tables[1], bin_w, L, T)

    skip_col = hy_skip.reshape(n_ord * W, 1)
    z, z_row0, z_rows = u_t, 0, (n_ord + 1) * W
    for o in range(n_ord):
        z = _hy_conv(z, z_row0, z_rows, u_t, (o + 1) * W, (n_ord + 1) * W, skip_col, tables,
                     h_re, h_im, h_ny, o * W, B, W, L, T)
        z_row0, z_rows = 0, W
    y_hy_t = z

    merged = _branch_merge(y_gla, y_hy_t, w_gla_o.astype(BF16), w_hy_o.astype(BF16), gates, B, L, D)
    h1, h1b = _proj_ln(merged, w_out.astype(BF16), x2d, ln1_g, ln1_b, alpha)
    act = _matmul_f32w(h1b, w_ff1, w_ff1.shape[1], out_dtype=BF16, act="relu2", name="ff1_relu2")
    out = _ff2_ln(act, w_ff2.astype(BF16), h1, ln2_g, ln2_b, alpha)
    return out.reshape(B, L, D)


def kernel(x, w_in, gla_wa2_f, gla_ba_f, gla_wa2_b, gla_ba_b, gla_norm_g, w_gla_o, hy_conv_w, hy_conv_b, hy_w1, hy_b1, hy_w2, hy_b2, hy_w3, hy_b3, hy_w4, hy_b4, hy_freq, hy_skip, w_hy_o, w_out, ln1_g, ln1_b, w_ff1, w_ff2, ln2_g, ln2_b):
    depth = w_in.shape[0]
    alpha = (2 * depth) ** 0.25
    params = (w_in, gla_wa2_f, gla_ba_f, gla_wa2_b, gla_ba_b, gla_norm_g, w_gla_o, hy_conv_w, hy_conv_b,
              hy_w1, hy_b1, hy_w2, hy_b2, hy_w3, hy_b3, hy_w4, hy_b4, hy_freq, hy_skip, w_hy_o, w_out,
              ln1_g, ln1_b, w_ff1, w_ff2, ln2_g, ln2_b)
    h = x
    for l in range(depth):
        h = _layer(h, *(p[l] for p in params), alpha)
    return h
```

```python
import functools
import math

import jax
import jax.numpy as jnp
import numpy as np
from jax import lax
from jax.experimental import pallas as pl
from jax.experimental.pallas import tpu as pltpu

F32 = jnp.float32
BF16 = jnp.bfloat16

GLA_HEADS = 4
GLA_TAU = 16.0
GLA_CHUNK = 64
GLA_BLOCK = 256
HY_BLOCK = 512
HY_FAST_DECAY = 0.3
HY_SLOW_DECAY = 1.5
HY_DECAY_TARGET = 1e-2
LN_EPS = 1e-5

V7X_VMEM_LIMIT_BYTES = 56 * 1024 * 1024
V7X_LANES = 128

NN = (((1,), (0,)), ((), ()))
NT = (((1,), (1,)), ((), ()))
TN = (((0,), (0,)), ((), ()))


def _dot(a, b, dims=NN):
    return lax.dot_general(a.astype(BF16), b.astype(BF16), dims, preferred_element_type=F32)


def _split2(a):
    h1 = a.astype(BF16)
    h2 = (a - h1.astype(F32)).astype(BF16)
    return h1, h2


def _dot_f32(a, b, dims=NN):
    a1, a2 = _split2(a)
    b1, b2 = _split2(b)
    d = lambda x, y: lax.dot_general(x, y, dims, preferred_element_type=F32)
    return (d(a2, b1) + d(a1, b2)) + d(a1, b1)


def _dot_exact_lhs(a_bf16, b, dims=NN):
    b1, b2 = _split2(b)
    d = lambda y: lax.dot_general(a_bf16, y, dims, preferred_element_type=F32)
    return d(b2) + d(b1)


def _call(body, *, grid, in_specs, out_specs, out_shape, name, scratch_shapes=()):
    return pl.pallas_call(
        body,
        grid=grid,
        in_specs=in_specs,
        out_specs=out_specs,
        out_shape=out_shape,
        scratch_shapes=scratch_shapes,
        compiler_params=pltpu.CompilerParams(
            dimension_semantics=("arbitrary",) * len(grid),
            vmem_limit_bytes=V7X_VMEM_LIMIT_BYTES,
        ),
        name=name,
    )


def _blk(n, want):
    b = min(n, want)
    while n % b:
        b //= 2
    return b


def _layernorm_rows(y, g, b):
    mu = jnp.mean(y, axis=-1, keepdims=True)
    d = y - mu
    var = jnp.mean(d * d, axis=-1, keepdims=True)
    return d * lax.rsqrt(var + LN_EPS) * g + b


def _sigmoid(x):
    return 0.5 * jnp.tanh(0.5 * x) + 0.5


def _act(acc, act):
    if act == "sigmoid":
        return _sigmoid(acc)
    if act == "relu2":
        return jnp.square(jnp.maximum(acc, 0.0))
    return acc


def _mm_body(a_ref, b_ref, o_ref, *, act, dims):
    o_ref[...] = _act(_dot(a_ref[...], b_ref[...], dims), act).astype(o_ref.dtype)


def _weight_spec(K, bn, transposed, index):
    if transposed:
        return pl.BlockSpec((bn, K), lambda *g: (index(*g), 0))
    return pl.BlockSpec((K, bn), lambda *g: (0, index(*g)))


def _matmul(a, b, *, out_dtype, act=None, name, bm=1024, bn=1024, transposed=False):
    M, K = a.shape
    N = b.shape[0] if transposed else b.shape[1]
    bm, bn = _blk(M, bm), _blk(N, bn)
    return _call(
        functools.partial(_mm_body, act=act, dims=NT if transposed else NN),
        grid=(M // bm, N // bn),
        in_specs=[pl.BlockSpec((bm, K), lambda i, j: (i, 0)),
                  _weight_spec(K, bn, transposed, lambda i, j: j)],
        out_specs=pl.BlockSpec((bm, bn), lambda i, j: (i, j)),
        out_shape=jax.ShapeDtypeStruct((M, N), out_dtype),
        name=name,
    )(a, b)


def _mm_f32w_body(a_ref, w_ref, o_ref, wb_ref, *, act, dims):
    @pl.when(pl.program_id(1) == 0)
    def _():
        wb_ref[...] = w_ref[...].astype(wb_ref.dtype)

    o_ref[...] = _act(_dot(a_ref[...], wb_ref[...], dims), act).astype(o_ref.dtype)


def _matmul_f32w(a, w, n_out, *, out_dtype, act=None, name, bm=1024, bn=1024, transposed=False):
    M, K = a.shape
    bm, bn = _blk(M, bm), _blk(n_out, bn)
    return _call(
        functools.partial(_mm_f32w_body, act=act, dims=NT if transposed else NN),
        grid=(n_out // bn, M // bm),
        in_specs=[pl.BlockSpec((bm, K), lambda j, i: (i, 0)),
                  _weight_spec(K, bn, transposed, lambda j, i: j)],
        out_specs=pl.BlockSpec((bm, bn), lambda j, i: (i, j)),
        out_shape=jax.ShapeDtypeStruct((M, n_out), out_dtype),
        scratch_shapes=[pltpu.VMEM((bn, K) if transposed else (K, bn), BF16)],
        name=name,
    )(a, w)


def _hy_proj_body(w_ref, x_ref, cp_ref, o_ref):
    u = _dot(w_ref[...], x_ref[...], NT)
    L = u.shape[1]
    t = lax.broadcasted_iota(jnp.int32, u.shape, 1)
    prev = jnp.where(t == 0, 0.0, pltpu.roll(u, 1, axis=1))
    nxt = jnp.where(t == L - 1, 0.0, pltpu.roll(u, L - 1, axis=1))
    cp = cp_ref[...]
    o_ref[...] = (cp[:, 0:1] * prev + cp[:, 1:2] * u + cp[:, 2:3] * nxt + cp[:, 3:4]).astype(o_ref.dtype)


def _hy_proj(w_t, xb, conv_params, B, L):
    C3, D = w_t.shape
    bc = _blk(C3, 1024)
    nc = C3 // bc
    return _call(
        _hy_proj_body,
        grid=(B, nc),
        in_specs=[pl.BlockSpec((bc, D), lambda b, c: (c, 0)),
                  pl.BlockSpec((L, D), lambda b, c: (b, 0)),
                  pl.BlockSpec((bc, 4), lambda b, c: (c, 0))],
        out_specs=pl.BlockSpec((bc, L), lambda b, c: (b * nc + c, 0)),
        out_shape=jax.ShapeDtypeStruct((B * C3, L), BF16),
        name="hy_proj_conv",
    )(w_t, xb, conv_params)


def _log_sigmoid(z):
    return -(jnp.maximum(-z, 0.0) + jnp.log(1.0 + jnp.exp(-jnp.abs(z))))


def _decay_body(x_ref, wab_ref, w2f_ref, bf_ref, w2b_ref, bb_ref, cf_ref, cb_ref, xb_ref, *, block):
    xb = x_ref[...].astype(xb_ref.dtype)
    xb_ref[...] = xb
    ab = _dot(xb, wab_ref[...], NT)
    zf = _dot_f32(ab, w2f_ref[...]) + bf_ref[...]
    zb = _dot_f32(ab, w2b_ref[...]) + bb_ref[...]
    laf = _log_sigmoid(zf) * (1.0 / GLA_TAU)
    lab = _log_sigmoid(zb) * (1.0 / GLA_TAU)
    T = block
    row = lax.broadcasted_iota(jnp.int32, (T, T), 0)
    col = lax.broadcasted_iota(jnp.int32, (T, T), 1)
    tri_f = (col <= row).astype(BF16)
    tri_b = (col >= row).astype(BF16)
    for s in range(xb.shape[0] // T):
        cf_ref[s * T:(s + 1) * T, :] = _dot_exact_lhs(tri_f, laf[s * T:(s + 1) * T, :])
        cb_ref[s * T:(s + 1) * T, :] = _dot_exact_lhs(tri_b, lab[s * T:(s + 1) * T, :])


def _gla_decays(x2d, w_in_t, ab_col, wa2_f, ba_f, wa2_b, ba_b, block):
    M, D = x2d.shape
    rank, KEY = wa2_f.shape
    P = V7X_LANES
    assert 2 * rank <= P and ab_col % P == 0
    w2f = jnp.pad(wa2_f, ((0, P - rank), (0, 0)))
    w2b = jnp.pad(wa2_b, ((rank, P - 2 * rank), (0, 0)))
    bm = max(_blk(M, 512), block)
    full = lambda shape: pl.BlockSpec(shape, lambda i: (0, 0))
    out = jax.ShapeDtypeStruct((M, KEY), F32)
    return _call(
        functools.partial(_decay_body, block=block),
        grid=(M // bm,),
        in_specs=[pl.BlockSpec((bm, D), lambda i: (i, 0)), pl.BlockSpec((P, D), lambda i: (ab_col // P, 0)),
                  full((P, KEY)), full((1, KEY)), full((P, KEY)), full((1, KEY))],
        out_specs=[pl.BlockSpec((bm, KEY), lambda i: (i, 0))] * 2 + [pl.BlockSpec((bm, D), lambda i: (i, 0))],
        out_shape=[out, out, jax.ShapeDtypeStruct((M, D), BF16)],
        name="gla_decays",
    )(x2d, w_in_t, w2f, ba_f.reshape(1, KEY), w2b, ba_b.reshape(1, KEY))


def _gla_body(q_ref, k_ref, v_ref, r_ref, cf_ref, cb_ref, g_ref, y_ref, o_acc,
              stf_ref, sf_ref, qbf_ref, kef_ref, klf_ref,
              stb_ref, sb_ref, qbb_ref, keb_ref, klb_ref, *, block, sub):
    L, DK = q_ref.shape
    T, C = block, sub
    n, ns = L // T, T // C
    scale = DK ** -0.5
    row = lax.broadcasted_iota(jnp.int32, (C, C), 0)
    col = lax.broadcasted_iota(jnp.int32, (C, C), 1)

    def boundary_row(cum_ref, start, pick):
        return cum_ref[pl.ds(pl.multiple_of(start, 8), 8), :][pick:pick + 1, :]

    def block_step(blk, cum_ref, st_ref, s_ref, qb_ref, ke_ref, kl_ref, forward, first_visit):
        r0 = pl.multiple_of(blk * T, T)
        tot = boundary_row(cum_ref, r0 + T - 8, 7) if forward else boundary_row(cum_ref, r0, 0)
        mask = (col <= row) if forward else (col > row)
        refs = {}
        for I in (range(ns) if forward else range(ns - 1, -1, -1)):
            sl = slice(I * C, (I + 1) * C)
            rows = pl.ds(r0 + I * C, C)
            if forward:
                ref = boundary_row(cum_ref, r0 + I * C - 8, 7) if I > 0 else jnp.zeros_like(tot)
            else:
                ref = boundary_row(cum_ref, r0 + (I + 1) * C, 0) if I < ns - 1 else jnp.zeros_like(tot)
            d = cum_ref[rows, :] - ref
            q_loc = q_ref[rows, :] * (scale * jnp.exp(d))
            k_loc = k_ref[rows, :] * jnp.exp(-d)
            qb_ref[sl, :] = (q_loc * jnp.exp(ref)).astype(BF16)
            ke_ref[sl, :] = (k_loc * jnp.exp(tot - ref)).astype(BF16)
            k_loc = k_loc.astype(BF16)
            kl_ref[sl, :] = k_loc
            s_ref[sl, sl] = jnp.where(mask, _dot(q_loc, k_loc, NT), 0.0).astype(BF16)
            for J, ref_j in refs.items():
                sj = slice(J * C, (J + 1) * C)
                s_ref[sl, sj] = _dot(q_loc * jnp.exp(ref - ref_j), kl_ref[sj, :], NT).astype(BF16)
            refs[I] = ref
        rows = pl.ds(r0, T)
        vb = v_ref[rows, :].astype(BF16)
        st = st_ref[...]
        o = _dot(s_ref[...], vb) + _dot(qb_ref[...], st, NT)
        st_ref[...] = st * jnp.exp(tot) + _dot(vb, ke_ref[...], TN)
        if first_visit:
            o_acc[rows, :] = o
        else:
            o = o_acc[rows, :] + o
            o = o * lax.rsqrt(jnp.mean(o * o, axis=-1, keepdims=True) + LN_EPS) * g_ref[...]
            r = r_ref[rows, :].astype(F32)
            y_ref[rows, :] = (o * (r * _sigmoid(r))).astype(y_ref.dtype)

    fwd = functools.partial(block_step, cum_ref=cf_ref, st_ref=stf_ref, s_ref=sf_ref, qb_ref=qbf_ref,
                            ke_ref=kef_ref, kl_ref=klf_ref, forward=True)
    bwd = functools.partial(block_step, cum_ref=cb_ref, st_ref=stb_ref, s_ref=sb_ref, qb_ref=qbb_ref,
                            ke_ref=keb_ref, kl_ref=klb_ref, forward=False)

    for ref in (stf_ref, stb_ref, sf_ref, sb_ref):
        ref[...] = jnp.zeros_like(ref)

    def sweep(lo, hi, step):
        def body(i, carry):
            step(i)
            return carry
        lax.fori_loop(lo, hi, body, 0)

    if n % 2 == 0:
        def first_half(i):
            fwd(i, first_visit=True)
            bwd(n - 1 - i, first_visit=True)

        def second_half(i):
            fwd(i, first_visit=False)
            bwd(n - 1 - i, first_visit=False)

        sweep(0, n // 2, first_half)
        sweep(n // 2, n, second_half)
    else:
        sweep(0, n, lambda i: fwd(i, first_visit=True))
        sweep(0, n, lambda i: bwd(n - 1 - i, first_visit=False))


def _gla(p_qkvr, cum_f, cum_b, norm_g, B, L, KEY, VAL, block):
    H = GLA_HEADS
    DK, DV = KEY // H, VAL // H
    M = B * L
    kq = KEY // DK
    vq = 2 * KEY // DV
    rq = (2 * KEY + VAL) // DV
    T = block
    direction_scratch = [pltpu.VMEM((DV, DK), F32), pltpu.VMEM((T, T), BF16), pltpu.VMEM((T, DK), BF16),
                         pltpu.VMEM((T, DK), BF16), pltpu.VMEM((T, DK), BF16)]
    return _call(
        functools.partial(_gla_body, block=T, sub=_blk(T, GLA_CHUNK)),
        grid=(B, H),
        in_specs=[pl.BlockSpec((L, DK), lambda b, h: (b, h)),
                  pl.BlockSpec((L, DK), lambda b, h: (b, kq + h)),
                  pl.BlockSpec((L, DV), lambda b, h: (b, vq + h)),
                  pl.BlockSpec((L, DV), lambda b, h: (b, rq + h)),
                  pl.BlockSpec((L, DK), lambda b, h: (b, h)),
                  pl.BlockSpec((L, DK), lambda b, h: (b, h)),
                  pl.BlockSpec((1, DV), lambda b, h: (0, h))],
        out_specs=pl.BlockSpec((L, DV), lambda b, h: (b, h)),
        out_shape=jax.ShapeDtypeStruct((M, VAL), BF16),
        scratch_shapes=[pltpu.VMEM((L, DV), F32)] + direction_scratch * 2,
        name="gla_bidir",
    )(p_qkvr, p_qkvr, p_qkvr, p_qkvr, cum_f, cum_b, norm_g.reshape(1, VAL))


def _filter_body(emb_ref, w1_ref, b1_ref, w2_ref, b2_ref, w3_ref, b3_ref, fr_ref,
                 w4f_ref, b4f_ref, w4b_ref, b4b_ref, dl_ref, hf_ref, hb_ref, h_ref):
    @pl.when((pl.program_id(0) == 0) & (pl.program_id(1) == 0))
    def _():
        fr = fr_ref[...]
        h = jnp.sin(fr * (_dot_f32(w1_ref[...], emb_ref[...]) + b1_ref[...]))
        h = jnp.sin(fr * (_dot_f32(w2_ref[...], h) + b2_ref[...]))
        h_ref[...] = jnp.sin(fr * (_dot_f32(w3_ref[...], h) + b3_ref[...]))

    h = h_ref[...]
    t_lin = emb_ref[0:1, :]
    decay = jnp.exp(-t_lin * dl_ref[...])
    hf_ref[...] = (_dot_f32(w4f_ref[...], h) + b4f_ref[...]) * decay
    hb_ref[...] = (_dot_f32(w4b_ref[...], h) + b4b_ref[...]) * decay


def _hyena_filters(emb_t, w1, b1, w2, b2, w3, b3, w4, b4, freq, deltas, W):
    L = emb_t.shape[1]
    HID = w1.shape[1]
    n_ord = w4.shape[1] // (2 * W)
    EMB = -(-emb_t.shape[0] // V7X_LANES) * V7X_LANES
    w1 = jnp.pad(w1, ((0, EMB - w1.shape[0]), (0, 0)))
    emb_t = jnp.pad(emb_t, ((0, EMB - emb_t.shape[0]), (0, 0)))
    cb = _blk(W, 512)
    ncb = W // cb
    w4t = w4.T
    b4c = b4.reshape(-1, 1)
    colv = lambda v: v.reshape(-1, 1)
    full = lambda shape: pl.BlockSpec(shape, lambda o, c: (0, 0))
    rows = n_ord * W
    return _call(
        _filter_body,
        grid=(n_ord, ncb),
        in_specs=[full((EMB, L)), full((HID, EMB)), full((HID, 1)), full((HID, HID)), full((HID, 1)),
                  full((HID, HID)), full((HID, 1)), full((HID, 1)),
                  pl.BlockSpec((cb, HID), lambda o, c: (o * 2 * ncb + c, 0)),
                  pl.BlockSpec((cb, 1), lambda o, c: (o * 2 * ncb + c, 0)),
                  pl.BlockSpec((cb, HID), lambda o, c: (o * 2 * ncb + ncb + c, 0)),
                  pl.BlockSpec((cb, 1), lambda o, c: (o * 2 * ncb + ncb + c, 0)),
                  pl.BlockSpec((cb, 1), lambda o, c: (c, 0))],
        out_specs=[pl.BlockSpec((cb, L), lambda o, c: (o * ncb + c, 0))] * 2,
        out_shape=[jax.ShapeDtypeStruct((rows, L), F32)] * 2,
        scratch_shapes=[pltpu.VMEM((HID, L), F32)],
        name="hy_filters",
    )(emb_t, w1.T, colv(b1), w2.T, colv(b2), w3.T, colv(b3), colv(freq), w4t, b4c, w4t, b4c, deltas)


def _lag_spectrum_body(hf_ref, hb_ref, c_ref, s_ref, w_ref, hr_ref, hi_ref, hn_ref, *, L, T):
    nb = L // T
    rows = hf_ref.shape[0]
    lane = lax.broadcasted_iota(jnp.int32, (rows, T), 1)
    alt = jnp.where((lane & 1) == 0, 1.0, -1.0)
    w = w_ref[...]

    def transforms(x_ref):
        blocks = [x_ref[:, n * T:(n + 1) * T] for n in range(nb)]
        return ([_dot(x, c_ref[...]) for x in blocks], [_dot(x, s_ref[...]) for x in blocks],
                [x[:, 0:1] for x in blocks])

    fc, fs, f0 = transforms(hf_ref)
    bc, bs, b0 = transforms(hb_ref)
    ny = lambda s: s[:, 0:1]
    for d in range(-(nb - 1), nb):
        if d >= 1:
            hr = fc[d] + alt * (fc[d - 1] - f0[d - 1])
            hi = fs[d] + alt * fs[d - 1]
            hn = ny(fs[d]) + ny(fs[d - 1]) - f0[d - 1]
        elif d == 0:
            hr = fc[0] + bc[0] - b0[0]
            hi = fs[0] - bs[0]
            hn = ny(fs[0]) + ny(bs[0]) - b0[0]
        else:
            e = -d
            hr = bc[e] + alt * (bc[e - 1] - b0[e - 1])
            hi = -(bs[e] + alt * bs[e - 1])
            hn = ny(bs[e]) + ny(bs[e - 1]) - b0[e - 1]
        li = d + nb - 1
        hr_ref[li] = (hr * w).astype(hr_ref.dtype)
        hi_ref[li] = jnp.where(lane == 0, 0.0, hi * w).astype(hi_ref.dtype)
        hn_ref[li] = hn * w[:, 0:1]


def _lag_spectra(hf2, hb2, cmat, smat, bin_w, L, T):
    R = hf2.shape[0]
    assert T % 2 == 0 and L % T == 0
    nl = 2 * (L // T) - 1
    rb = _blk(R, 256)
    full = lambda shape: pl.BlockSpec(shape, lambda r: (0, 0))
    return _call(
        functools.partial(_lag_spectrum_body, L=L, T=T),
        grid=(R // rb,),
        in_specs=[pl.BlockSpec((rb, L), lambda r: (r, 0)), pl.BlockSpec((rb, L), lambda r: (r, 0)),
                  full((T, T)), full((T, T)), full((1, T))],
        out_specs=[pl.BlockSpec((nl, rb, T), lambda r: (0, r, 0)),
                   pl.BlockSpec((nl, rb, T), lambda r: (0, r, 0)),
                   pl.BlockSpec((nl, rb, 1), lambda r: (0, r, 0))],
        out_shape=[jax.ShapeDtypeStruct((nl, R, T), BF16), jax.ShapeDtypeStruct((nl, R, T), BF16),
                   jax.ShapeDtypeStruct((nl, R, 1), F32)],
        name="hy_lag_spectra",
    )(hf2, hb2, cmat, smat, bin_w)


def _hy_conv_body(z_ref, gate_ref, skip_ref, c_ref, s_ref, st_ref, hr_ref, hi_ref, hn_ref, o_ref,
                  zr_ref, zi_ref, yr_ref, yi_ref, *, T, chunk):
    R, L = z_ref.shape
    nb = L // T
    for j in range(nb):
        zj = z_ref[:, j * T:(j + 1) * T]
        zr_ref[j] = _dot(zj, c_ref[...]).astype(zr_ref.dtype)
        zi_ref[j] = _dot(zj, s_ref[...]).astype(zi_ref.dtype)

    bin0 = lax.broadcasted_iota(jnp.int32, (chunk, T), 1) == 0

    def combine(r, carry):
        rows = pl.ds(pl.multiple_of(r * chunk, chunk), chunk)
        zr = [zr_ref[j, rows, :] for j in range(nb)]
        zi = [zi_ref[j, rows, :] for j in range(nb)]
        zn = [z[:, 0:1].astype(F32) for z in zi]
        for i in range(nb):
            yr = yi = yn = None
            for j in range(nb):
                lag = i - j + nb - 1
                hr, hi = hr_ref[lag, rows, :], hi_ref[lag, rows, :]
                tr = zr[j] * hr - zi[j] * hi
                ti = zr[j] * hi + zi[j] * hr
                tn = zn[j] * hn_ref[lag, rows, :]
                yr, yi, yn = (tr, ti, tn) if j == 0 else (yr + tr, yi + ti, yn + tn)
            yr_ref[i, rows, :] = yr
            yi_ref[i, rows, :] = jnp.where(bin0, yn.astype(yi.dtype), yi)
        return carry

    lax.fori_loop(0, R // chunk, combine, 0)

    skip = skip_ref[...]
    for i in range(nb):
        sl = slice(i * T, (i + 1) * T)
        zc = _dot(yr_ref[i], c_ref[...]) + _dot(yi_ref[i], st_ref[...])
        o_ref[:, sl] = (gate_ref[:, sl] * (zc + skip * z_ref[:, sl])).astype(o_ref.dtype)


def _hy_conv(z2d, z_row0, z_rows_per_b, gates2d, g_row0, g_rows_per_b, skip_col, tables,
             hr, hi, hn, h_row0, B, W, L, T):
    R = _blk(W, 512)
    ncb = W // R
    nb, nl = L // T, hr.shape[0]
    zb, z0 = z_rows_per_b // R, z_row0 // R
    gb, g0 = g_rows_per_b // R, g_row0 // R
    h0 = h_row0 // R
    table = pl.BlockSpec((T, T), lambda c, b: (0, 0))
    spectrum = lambda last: pl.BlockSpec((nl, R, last), lambda c, b: (0, h0 + c, 0),
                                         pipeline_mode=pl.Buffered(1))
    return _call(
        functools.partial(_hy_conv_body, T=T, chunk=16),
        grid=(ncb, B),
        in_specs=[pl.BlockSpec((R, L), lambda c, b: (b * zb + z0 + c, 0)),
                  pl.BlockSpec((R, L), lambda c, b: (b * gb + g0 + c, 0)),
                  pl.BlockSpec((R, 1), lambda c, b: (h0 + c, 0)),
                  table, table, table, spectrum(T), spectrum(T), spectrum(1)],
        out_specs=pl.BlockSpec((R, L), lambda c, b: (b * ncb + c, 0)),
        out_shape=jax.ShapeDtypeStruct((B * W, L), BF16),
        scratch_shapes=[pltpu.VMEM((nb, R, T), BF16)] * 4,
        name="hy_conv",
    )(z2d, gates2d, skip_col, *tables, hr, hi, hn)


def _merge_body(yg_ref, yh_ref, wg_ref, wh_ref, g0_ref, g1_ref, o_ref):
    tg = _dot(yg_ref[...], wg_ref[...])
    th = _dot(yh_ref[...], wh_ref[...], TN)
    o_ref[...] = (g0_ref[...] * tg + g1_ref[...] * th).astype(o_ref.dtype)


def _branch_merge(y_gla, y_hy_t, w_gla_o, w_hy_o, gates, B, L, D):
    VAL, W = y_gla.shape[1], w_hy_o.shape[0]
    bt = _blk(L, 512)
    nt = L // bt
    resident = lambda shape: pl.BlockSpec(shape, lambda b, t: (0, 0), pipeline_mode=pl.Buffered(1))
    return _call(
        _merge_body,
        grid=(B, nt),
        in_specs=[pl.BlockSpec((bt, VAL), lambda b, t: (b * nt + t, 0)),
                  pl.BlockSpec((W, bt), lambda b, t: (b, t)),
                  resident((VAL, D)), resident((W, D)),
                  pl.BlockSpec((bt, D), lambda b, t: (b * nt + t, 0)),
                  pl.BlockSpec((bt, D), lambda b, t: (b * nt + t, 1))],
        out_specs=pl.BlockSpec((bt, D), lambda b, t: (b * nt + t, 0)),
        out_shape=jax.ShapeDtypeStruct((B * L, D), BF16),
        name="branch_merge",
    )(y_gla, y_hy_t, w_gla_o, w_hy_o, gates, gates)


def _proj_ln_body(a_ref, w_ref, x_ref, g_ref, b_ref, o_ref, ob_ref, *, alpha):
    y = alpha * x_ref[...] + _dot(a_ref[...], w_ref[...])
    h = _layernorm_rows(y, g_ref[...], b_ref[...])
    o_ref[...] = h
    ob_ref[...] = h.astype(ob_ref.dtype)


def _proj_ln(a, w, x2d, ln_g, ln_b, alpha):
    M, K = a.shape
    D = w.shape[1]
    bm = _blk(M, 512)
    row = lambda i: (i, 0)
    full = lambda shape: pl.BlockSpec(shape, lambda i: (0, 0))
    return _call(
        functools.partial(_proj_ln_body, alpha=alpha),
        grid=(M // bm,),
        in_specs=[pl.BlockSpec((bm, K), row), full((K, D)), pl.BlockSpec((bm, D), row),
                  full((1, D)), full((1, D))],
        out_specs=[pl.BlockSpec((bm, D), row)] * 2,
        out_shape=[jax.ShapeDtypeStruct((M, D), F32), jax.ShapeDtypeStruct((M, D), BF16)],
        name="out_proj_ln1",
    )(a, w, x2d, ln_g.reshape(1, D), ln_b.reshape(1, D))


def _ff2_ln_body(a_ref, w_ref, h_ref, g_ref, b_ref, o_ref, *, alpha, ln_rows):
    kk = pl.program_id(1)

    @pl.when(kk == 0)
    def _():
        o_ref[...] = jnp.zeros_like(o_ref)

    o_ref[...] += _dot(a_ref[...], w_ref[...])

    @pl.when(kk == pl.num_programs(1) - 1)
    def _():
        g, b = g_ref[...], b_ref[...]

        def norm_rows(r, carry):
            rows = pl.ds(pl.multiple_of(r * ln_rows, ln_rows), ln_rows)
            y = alpha * h_ref[rows, :] + o_ref[rows, :]
            o_ref[rows, :] = _layernorm_rows(y, g, b)
            return carry

        lax.fori_loop(0, o_ref.shape[0] // ln_rows, norm_rows, 0)


def _ff2_ln(a, w, h, ln_g, ln_b, alpha):
    M, K = a.shape
    D = w.shape[1]
    bm, bk = _blk(M, 1024), _blk(K, 1024)
    return _call(
        functools.partial(_ff2_ln_body, alpha=alpha, ln_rows=_blk(bm, 128)),
        grid=(M // bm, K // bk),
        in_specs=[pl.BlockSpec((bm, bk), lambda i, k: (i, k)),
                  pl.BlockSpec((bk, D), lambda i, k: (k, 0)),
                  pl.BlockSpec((bm, D), lambda i, k: (i, 0)),
                  pl.BlockSpec((1, D), lambda i, k: (0, 0)),
                  pl.BlockSpec((1, D), lambda i, k: (0, 0))],
        out_specs=pl.BlockSpec((bm, D), lambda i, k: (i, 0)),
        out_shape=jax.ShapeDtypeStruct((M, D), F32),
        name="ff2_ln2",
    )(a, w, h, ln_g.reshape(1, D), ln_b.reshape(1, D))


def _dft_tables(L):
    LO = _blk(L, 32)
    k = jnp.arange(L, dtype=jnp.int32)
    ang = lambda m: ((m[:, None] * k[None, :]) % (2 * L)).astype(F32) * (math.pi / L)
    a_hi = ang(jnp.arange(L // LO, dtype=jnp.int32) * LO)[:, None, :]
    a_lo = ang(jnp.arange(LO, dtype=jnp.int32))[None, :, :]
    cos_t = (jnp.cos(a_hi) * jnp.cos(a_lo) - jnp.sin(a_hi) * jnp.sin(a_lo)).reshape(L, L)
    sin_t = (jnp.sin(a_hi) * jnp.cos(a_lo) + jnp.cos(a_hi) * jnp.sin(a_lo)).reshape(L, L)
    nyq = jnp.where(k % 2 == 0, 1.0, -1.0).astype(F32)
    smat = jnp.where(k[None, :] == 0, nyq[:, None], sin_t)
    smat_t = jnp.where(k[:, None] == 0, nyq[None, :], sin_t)
    return cos_t.astype(BF16), smat.astype(BF16), smat_t.astype(BF16)


def _position_features(L, emb_dim):
    t = jnp.linspace(0.0, 1.0, L, dtype=F32)[:, None]
    bands = (emb_dim - 1) // 2
    f = jnp.linspace(1e-4, bands - 1, bands, dtype=F32)
    wpos = 2.0 * math.pi * jnp.arange(L, dtype=F32) / L
    ang = wpos[:, None] * f[None, :]
    return jnp.concatenate([t, jnp.cos(ang), -jnp.sin(ang)], axis=-1).T


def _layer(h, w_in, gla_wa2_f, gla_ba_f, gla_wa2_b, gla_ba_b, gla_norm_g, w_gla_o,
           hy_conv_w, hy_conv_b, hy_w1, hy_b1, hy_w2, hy_b2, hy_w3, hy_b3, hy_w4, hy_b4,
           hy_freq, hy_skip, w_hy_o, w_out, ln1_g, ln1_b, w_ff1, w_ff2, ln2_g, ln2_b, alpha):
    B, L, D = h.shape
    M = B * L
    rank, KEY = gla_wa2_f.shape
    VAL = gla_norm_g.shape[0]
    n_ord, W = hy_skip.shape

    sizes = (KEY, KEY, VAL, VAL, rank, rank, (n_ord + 1) * W, 2 * D)
    offs = [int(v) for v in np.concatenate([[0], np.cumsum(sizes)])]
    w_in_t = w_in.T
    w_hy_t = w_in_t[offs[6]:offs[7]].astype(BF16)
    w_gate_t = w_in_t[offs[7]:offs[8]].astype(BF16)

    x2d = h.reshape(M, D)
    gla_block = _blk(L, GLA_BLOCK)
    cum_f, cum_b, xb = _gla_decays(x2d, w_in_t, offs[4], gla_wa2_f, gla_ba_f, gla_wa2_b, gla_ba_b, gla_block)

    p_qkvr = _matmul_f32w(xb, w_in_t, offs[4], out_dtype=BF16, name="in_proj_qkvr", bm=2048, transposed=True)
    gates = _matmul(xb, w_gate_t, out_dtype=BF16, act="sigmoid", name="in_proj_gates", bm=2048, transposed=True)
    conv_params = jnp.concatenate([hy_conv_w.T, hy_conv_b[:, None]], axis=1)
    u_t = _hy_proj(w_hy_t, xb, conv_params, B, L)

    y_gla = _gla(p_qkvr, cum_f, cum_b, gla_norm_g, B, L, KEY, VAL, gla_block)

    emb_t = _position_features(L, hy_w1.shape[0])
    min_decay = math.log(HY_DECAY_TARGET) / HY_SLOW_DECAY
    max_decay = math.log(HY_DECAY_TARGET) / HY_FAST_DECAY
    deltas = jnp.abs(jnp.linspace(min_decay, max_decay, W, dtype=F32)).reshape(W, 1)
    hf2, hb2 = _hyena_filters(emb_t, hy_w1, hy_b1, hy_w2, hy_b2, hy_w3, hy_b3,
                              hy_w4, hy_b4, hy_freq, deltas, W)
    T = _blk(L // 2, HY_BLOCK)
    tables = _dft_tables(T)
    bin_w = jnp.full((1, T), 1.0 / T, F32).at[0, 0].set(0.5 / T)
    h_re, h_im, h_ny = _lag_spectra(hf2, hb2, tables[0], tables[1], bin_w, L, T)

    skip_col = hy_skip.reshape(n_ord * W, 1)
    z, z_row0, z_rows = u_t, 0, (n_ord + 1) * W
    for o in range(n_ord):
        z = _hy_conv(z, z_row0, z_rows, u_t, (o + 1) * W, (n_ord + 1) * W, skip_col, tables,
                     h_re, h_im, h_ny, o * W, B, W, L, T)
        z_row0, z_rows = 0, W
    y_hy_t = z

    merged = _branch_merge(y_gla, y_hy_t, w_gla_o.astype(BF16), w_hy_o.astype(BF16), gates, B, L, D)
    h1, h1b = _proj_ln(merged, w_out.astype(BF16), x2d, ln1_g, ln1_b, alpha)
    act = _matmul_f32w(h1b, w_ff1, w_ff1.shape[1], out_dtype=BF16, act="relu2", name="ff1_relu2")
    out = _ff2_ln(act, w_ff2.astype(BF16), h1, ln2_g, ln2_b, alpha)
    return out.reshape(B, L, D)


def kernel(x, w_in, gla_wa2_f, gla_ba_f, gla_wa2_b, gla_ba_b, gla_norm_g, w_gla_o, hy_conv_w, hy_conv_b, hy_w1, hy_b1, hy_w2, hy_b2, hy_w3, hy_b3, hy_w4, hy_b4, hy_freq, hy_skip, w_hy_o, w_out, ln1_g, ln1_b, w_ff1, w_ff2, ln2_g, ln2_b):
    depth = w_in.shape[0]
    alpha = (2 * depth) ** 0.25
    params = (w_in, gla_wa2_f, gla_ba_f, gla_wa2_b, gla_ba_b, gla_norm_g, w_gla_o, hy_conv_w, hy_conv_b,
              hy_w1, hy_b1, hy_w2, hy_b2, hy_w3, hy_b3, hy_w4, hy_b4, hy_freq, hy_skip, w_hy_o, w_out,
              ln1_g, ln1_b, w_ff1, w_ff2, ln2_g, ln2_b)
    h = x
    for l in range(depth):
        h = _layer(h, *(p[l] for p in params), alpha)
    return h
```

```python
import functools
import math

import jax
import jax.numpy as jnp
import numpy as np
from jax import lax
from jax.experimental import pallas as pl
from jax.experimental.pallas import tpu as pltpu

F32 = jnp.float32
BF16 = jnp.bfloat16

GLA_HEADS = 4
GLA_TAU = 16.0
GLA_CHUNK = 64
GLA_BLOCK = 256
HY_BLOCK = 512
HY_FAST_DECAY = 0.3
HY_SLOW_DECAY = 1.5
HY_DECAY_TARGET = 1e-2
LN_EPS = 1e-5

V7X_VMEM_LIMIT_BYTES = 56 * 1024 * 1024
V7X_LANES = 128
V7X_SUBLANES = 8

NN = (((1,), (0,)), ((), ()))
NT = (((1,), (1,)), ((), ()))
TN = (((0,), (0,)), ((), ()))


def _dot(a, b, dims=NN):
    return lax.dot_general(a.astype(BF16), b.astype(BF16), dims, preferred_element_type=F32)


def _split2(a):
    h1 = a.astype(BF16)
    h2 = (a - h1.astype(F32)).astype(BF16)
    return h1, h2


def _dot_f32(a, b, dims=NN):
    a1, a2 = _split2(a)
    b1, b2 = _split2(b)
    d = lambda x, y: lax.dot_general(x, y, dims, preferred_element_type=F32)
    return (d(a2, b1) + d(a1, b2)) + d(a1, b1)


def _dot_exact_lhs(a_bf16, b, dims=NN):
    b1, b2 = _split2(b)
    d = lambda y: lax.dot_general(a_bf16, y, dims, preferred_element_type=F32)
    return d(b2) + d(b1)


def _call(body, *, grid, in_specs, out_specs, out_shape, name, scratch_shapes=()):
    return pl.pallas_call(
        body,
        grid=grid,
        in_specs=in_specs,
        out_specs=out_specs,
        out_shape=out_shape,
        scratch_shapes=scratch_shapes,
        compiler_params=pltpu.CompilerParams(
            dimension_semantics=("arbitrary",) * len(grid),
            vmem_limit_bytes=V7X_VMEM_LIMIT_BYTES,
        ),
        name=name,
    )


def _blk(n, want):
    b = min(n, want)
    while n % b:
        b //= 2
    return b


def _layernorm_rows(y, g, b):
    mu = jnp.mean(y, axis=-1, keepdims=True)
    d = y - mu
    var = jnp.mean(d * d, axis=-1, keepdims=True)
    return d * lax.rsqrt(var + LN_EPS) * g + b


def _sigmoid(x):
    return 0.5 * jnp.tanh(0.5 * x) + 0.5


def _act(acc, act):
    if act == "sigmoid":
        return _sigmoid(acc)
    if act == "relu2":
        return jnp.square(jnp.maximum(acc, 0.0))
    return acc


def _weight_spec(K, bn, transposed, index, row0=0):
    if transposed and row0:
        assert row0 % V7X_SUBLANES == 0
        return pl.BlockSpec((pl.Element(bn), pl.Element(K)),
                            lambda *g: (pl.multiple_of(row0 + index(*g) * bn, V7X_SUBLANES), 0))
    if transposed:
        return pl.BlockSpec((bn, K), lambda *g: (index(*g), 0))
    assert row0 == 0
    return pl.BlockSpec((K, bn), lambda *g: (0, index(*g)))


def _mm_f32w_body(a_ref, w_ref, o_ref, wb_ref, *, act, dims):
    @pl.when(pl.program_id(1) == 0)
    def _():
        wb_ref[...] = w_ref[...].astype(wb_ref.dtype)

    o_ref[...] = _act(_dot(a_ref[...], wb_ref[...], dims), act).astype(o_ref.dtype)


def _matmul_f32w(a, w, n_out, *, out_dtype, act=None, name, bm=1024, bn=1024, transposed=False, row0=0):
    M, K = a.shape
    bm, bn = _blk(M, bm), _blk(n_out, bn)
    return _call(
        functools.partial(_mm_f32w_body, act=act, dims=NT if transposed else NN),
        grid=(n_out // bn, M // bm),
        in_specs=[pl.BlockSpec((bm, K), lambda j, i: (i, 0)),
                  _weight_spec(K, bn, transposed, lambda j, i: j, row0)],
        out_specs=pl.BlockSpec((bm, bn), lambda j, i: (i, j)),
        out_shape=jax.ShapeDtypeStruct((M, n_out), out_dtype),
        scratch_shapes=[pltpu.VMEM((bn, K) if transposed else (K, bn), BF16)],
        name=name,
    )(a, w)


def _hy_proj_body(w_ref, x_ref, cp_ref, o_ref):
    u = _dot(w_ref[...], x_ref[...], NT)
    L = u.shape[1]
    t = lax.broadcasted_iota(jnp.int32, u.shape, 1)
    prev = jnp.where(t == 0, 0.0, pltpu.roll(u, 1, axis=1))
    nxt = jnp.where(t == L - 1, 0.0, pltpu.roll(u, L - 1, axis=1))
    cp = cp_ref[...]
    o_ref[...] = (cp[:, 0:1] * prev + cp[:, 1:2] * u + cp[:, 2:3] * nxt + cp[:, 3:4]).astype(o_ref.dtype)


def _hy_proj(w_t, xb, conv_params, B, L):
    C3, D = w_t.shape
    bc = _blk(C3, 1024)
    nc = C3 // bc
    return _call(
        _hy_proj_body,
        grid=(B, nc),
        in_specs=[pl.BlockSpec((bc, D), lambda b, c: (c, 0)),
                  pl.BlockSpec((L, D), lambda b, c: (b, 0)),
                  pl.BlockSpec((bc, 4), lambda b, c: (c, 0))],
        out_specs=pl.BlockSpec((bc, L), lambda b, c: (b * nc + c, 0)),
        out_shape=jax.ShapeDtypeStruct((B * C3, L), BF16),
        name="hy_proj_conv",
    )(w_t, xb, conv_params)


def _log_sigmoid(z):
    return -(jnp.maximum(-z, 0.0) + jnp.log(1.0 + jnp.exp(-jnp.abs(z))))


def _decay_body(x_ref, wab_ref, w2f_ref, bf_ref, w2b_ref, bb_ref, cf_ref, cb_ref, xb_ref, *, block):
    xb = x_ref[...].astype(xb_ref.dtype)
    xb_ref[...] = xb
    ab = _dot(xb, wab_ref[...], NT)
    zf = _dot(ab, w2f_ref[...]) + bf_ref[...]
    zb = _dot(ab, w2b_ref[...]) + bb_ref[...]
    laf = _log_sigmoid(zf) * (1.0 / GLA_TAU)
    lab = _log_sigmoid(zb) * (1.0 / GLA_TAU)
    T = block
    row = lax.broadcasted_iota(jnp.int32, (T, T), 0)
    col = lax.broadcasted_iota(jnp.int32, (T, T), 1)
    tri_f = (col <= row).astype(BF16)
    tri_b = (col >= row).astype(BF16)
    for s in range(xb.shape[0] // T):
        cf_ref[s * T:(s + 1) * T, :] = _dot_exact_lhs(tri_f, laf[s * T:(s + 1) * T, :])
        cb_ref[s * T:(s + 1) * T, :] = _dot_exact_lhs(tri_b, lab[s * T:(s + 1) * T, :])


def _gla_decays(x2d, w_in_t, ab_col, wa2_f, ba_f, wa2_b, ba_b, block):
    M, D = x2d.shape
    rank, KEY = wa2_f.shape
    P = V7X_LANES
    assert 2 * rank <= P and ab_col % P == 0
    w2f = jnp.pad(wa2_f, ((0, P - rank), (0, 0)))
    w2b = jnp.pad(wa2_b, ((rank, P - 2 * rank), (0, 0)))
    bm = max(_blk(M, 512), block)
    full = lambda shape: pl.BlockSpec(shape, lambda i: (0, 0))
    out = jax.ShapeDtypeStruct((M, KEY), F32)
    return _call(
        functools.partial(_decay_body, block=block),
        grid=(M // bm,),
        in_specs=[pl.BlockSpec((bm, D), lambda i: (i, 0)), pl.BlockSpec((P, D), lambda i: (ab_col // P, 0)),
                  full((P, KEY)), full((1, KEY)), full((P, KEY)), full((1, KEY))],
        out_specs=[pl.BlockSpec((bm, KEY), lambda i: (i, 0))] * 2 + [pl.BlockSpec((bm, D), lambda i: (i, 0))],
        out_shape=[out, out, jax.ShapeDtypeStruct((M, D), BF16)],
        name="gla_decays",
    )(x2d, w_in_t, w2f, ba_f.reshape(1, KEY), w2b, ba_b.reshape(1, KEY))


def _gla_body(q_ref, k_ref, v_ref, r_ref, cf_ref, cb_ref, g_ref, y_ref, o_acc,
              stf_ref, sf_ref, qbf_ref, kef_ref, klf_ref,
              stb_ref, sb_ref, qbb_ref, keb_ref, klb_ref, *, block, sub):
    L, DK = q_ref.shape
    T, C = block, sub
    n, ns = L // T, T // C
    scale = DK ** -0.5
    row = lax.broadcasted_iota(jnp.int32, (C, C), 0)
    col = lax.broadcasted_iota(jnp.int32, (C, C), 1)

    def boundary_row(cum_ref, start, pick):
        return cum_ref[pl.ds(pl.multiple_of(start, 8), 8), :][pick:pick + 1, :]

    def block_step(blk, cum_ref, st_ref, s_ref, qb_ref, ke_ref, kl_ref, forward, first_visit):
        r0 = pl.multiple_of(blk * T, T)
        tot = boundary_row(cum_ref, r0 + T - 8, 7) if forward else boundary_row(cum_ref, r0, 0)
        mask = (col <= row) if forward else (col > row)
        refs = {}
        for I in (range(ns) if forward else range(ns - 1, -1, -1)):
            sl = slice(I * C, (I + 1) * C)
            rows = pl.ds(r0 + I * C, C)
            if forward:
                ref = boundary_row(cum_ref, r0 + I * C - 8, 7) if I > 0 else jnp.zeros_like(tot)
            else:
                ref = boundary_row(cum_ref, r0 + (I + 1) * C, 0) if I < ns - 1 else jnp.zeros_like(tot)
            d = cum_ref[rows, :] - ref
            q_loc = q_ref[rows, :] * (scale * jnp.exp(d))
            k_loc = k_ref[rows, :] * jnp.exp(-d)
            qb_ref[sl, :] = (q_loc * jnp.exp(ref)).astype(BF16)
            ke_ref[sl, :] = (k_loc * jnp.exp(tot - ref)).astype(BF16)
            k_loc = k_loc.astype(BF16)
            kl_ref[sl, :] = k_loc
            s_ref[sl, sl] = jnp.where(mask, _dot(q_loc, k_loc, NT), 0.0).astype(BF16)
            for J, ref_j in refs.items():
                sj = slice(J * C, (J + 1) * C)
                s_ref[sl, sj] = _dot(q_loc * jnp.exp(ref - ref_j), kl_ref[sj, :], NT).astype(BF16)
            refs[I] = ref
        rows = pl.ds(r0, T)
        vb = v_ref[rows, :].astype(BF16)
        st = st_ref[...]
        o = _dot(s_ref[...], vb) + _dot(qb_ref[...], st, NT)
        st_ref[...] = st * jnp.exp(tot) + _dot(vb, ke_ref[...], TN)
        if first_visit:
            o_acc[rows, :] = o
        else:
            o = o_acc[rows, :] + o
            o = o * lax.rsqrt(jnp.mean(o * o, axis=-1, keepdims=True) + LN_EPS) * g_ref[...]
            r = r_ref[rows, :].astype(F32)
            y_ref[rows, :] = (o * (r * _sigmoid(r))).astype(y_ref.dtype)

    fwd = functools.partial(block_step, cum_ref=cf_ref, st_ref=stf_ref, s_ref=sf_ref, qb_ref=qbf_ref,
                            ke_ref=kef_ref, kl_ref=klf_ref, forward=True)
    bwd = functools.partial(block_step, cum_ref=cb_ref, st_ref=stb_ref, s_ref=sb_ref, qb_ref=qbb_ref,
                            ke_ref=keb_ref, kl_ref=klb_ref, forward=False)

    for ref in (stf_ref, stb_ref, sf_ref, sb_ref):
        ref[...] = jnp.zeros_like(ref)

    def sweep(lo, hi, step):
        def body(i, carry):
            step(i)
            return carry
        lax.fori_loop(lo, hi, body, 0)

    if n % 2 == 0:
        def first_half(i):
            fwd(i, first_visit=True)
            bwd(n - 1 - i, first_visit=True)

        def second_half(i):
            fwd(i, first_visit=False)
            bwd(n - 1 - i, first_visit=False)

        sweep(0, n // 2, first_half)
        sweep(n // 2, n, second_half)
    else:
        sweep(0, n, lambda i: fwd(i, first_visit=True))
        sweep(0, n, lambda i: bwd(n - 1 - i, first_visit=False))


def _gla(p_qkvr, cum_f, cum_b, norm_g, B, L, KEY, VAL, block):
    H = GLA_HEADS
    DK, DV = KEY // H, VAL // H
    M = B * L
    kq = KEY // DK
    vq = 2 * KEY // DV
    rq = (2 * KEY + VAL) // DV
    T = block
    direction_scratch = [pltpu.VMEM((DV, DK), F32), pltpu.VMEM((T, T), BF16), pltpu.VMEM((T, DK), BF16),
                         pltpu.VMEM((T, DK), BF16), pltpu.VMEM((T, DK), BF16)]
    return _call(
        functools.partial(_gla_body, block=T, sub=_blk(T, GLA_CHUNK)),
        grid=(B, H),
        in_specs=[pl.BlockSpec((L, DK), lambda b, h: (b, h)),
                  pl.BlockSpec((L, DK), lambda b, h: (b, kq + h)),
                  pl.BlockSpec((L, DV), lambda b, h: (b, vq + h)),
                  pl.BlockSpec((L, DV), lambda b, h: (b, rq + h)),
                  pl.BlockSpec((L, DK), lambda b, h: (b, h)),
                  pl.BlockSpec((L, DK), lambda b, h: (b, h)),
                  pl.BlockSpec((1, DV), lambda b, h: (0, h))],
        out_specs=pl.BlockSpec((L, DV), lambda b, h: (b, h)),
        out_shape=jax.ShapeDtypeStruct((M, VAL), BF16),
        scratch_shapes=[pltpu.VMEM((L, DV), F32)] + direction_scratch * 2,
        name="gla_bidir",
    )(p_qkvr, p_qkvr, p_qkvr, p_qkvr, cum_f, cum_b, norm_g.reshape(1, VAL))


def _filter_body(emb_ref, w1_ref, b1_ref, w2_ref, b2_ref, w3_ref, b3_ref, fr_ref,
                 w4f_ref, b4f_ref, w4b_ref, b4b_ref, dl_ref, hf_ref, hb_ref, h_ref):
    @pl.when((pl.program_id(0) == 0) & (pl.program_id(1) == 0))
    def _():
        fr = fr_ref[...]
        h = jnp.sin(fr * (_dot_f32(w1_ref[...], emb_ref[...]) + b1_ref[...]))
        h = jnp.sin(fr * (_dot_f32(w2_ref[...], h) + b2_ref[...]))
        h_ref[...] = jnp.sin(fr * (_dot_f32(w3_ref[...], h) + b3_ref[...]))

    h = h_ref[...]
    t_lin = emb_ref[0:1, :]
    decay = jnp.exp(-t_lin * dl_ref[...])
    hf_ref[...] = (_dot_f32(w4f_ref[...], h) + b4f_ref[...]) * decay
    hb_ref[...] = (_dot_f32(w4b_ref[...], h) + b4b_ref[...]) * decay


def _hyena_filters(emb_t, w1, b1, w2, b2, w3, b3, w4, b4, freq, deltas, W):
    L = emb_t.shape[1]
    HID = w1.shape[1]
    n_ord = w4.shape[1] // (2 * W)
    EMB = -(-emb_t.shape[0] // V7X_LANES) * V7X_LANES
    w1 = jnp.pad(w1, ((0, EMB - w1.shape[0]), (0, 0)))
    emb_t = jnp.pad(emb_t, ((0, EMB - emb_t.shape[0]), (0, 0)))
    cb = _blk(W, 512)
    ncb = W // cb
    w4t = w4.T
    b4c = b4.reshape(-1, 1)
    colv = lambda v: v.reshape(-1, 1)
    full = lambda shape: pl.BlockSpec(shape, lambda o, c: (0, 0))
    rows = n_ord * W
    return _call(
        _filter_body,
        grid=(n_ord, ncb),
        in_specs=[full((EMB, L)), full((HID, EMB)), full((HID, 1)), full((HID, HID)), full((HID, 1)),
                  full((HID, HID)), full((HID, 1)), full((HID, 1)),
                  pl.BlockSpec((cb, HID), lambda o, c: (o * 2 * ncb + c, 0)),
                  pl.BlockSpec((cb, 1), lambda o, c: (o * 2 * ncb + c, 0)),
                  pl.BlockSpec((cb, HID), lambda o, c: (o * 2 * ncb + ncb + c, 0)),
                  pl.BlockSpec((cb, 1), lambda o, c: (o * 2 * ncb + ncb + c, 0)),
                  pl.BlockSpec((cb, 1), lambda o, c: (c, 0))],
        out_specs=[pl.BlockSpec((cb, L), lambda o, c: (o * ncb + c, 0))] * 2,
        out_shape=[jax.ShapeDtypeStruct((rows, L), F32)] * 2,
        scratch_shapes=[pltpu.VMEM((HID, L), F32)],
        name="hy_filters",
    )(emb_t, w1.T, colv(b1), w2.T, colv(b2), w3.T, colv(b3), colv(freq), w4t, b4c, w4t, b4c, deltas)


def _lag_spectrum_body(hf_ref, hb_ref, c_ref, s_ref, w_ref, hr_ref, hi_ref, hn_ref, *, L, T):
    nb = L // T
    rows = hf_ref.shape[0]
    lane = lax.broadcasted_iota(jnp.int32, (rows, T), 1)
    alt = jnp.where((lane & 1) == 0, 1.0, -1.0)
    w = w_ref[...]

    def transforms(x_ref):
        blocks = [x_ref[:, n * T:(n + 1) * T] for n in range(nb)]
        return ([_dot(x, c_ref[...]) for x in blocks], [_dot(x, s_ref[...]) for x in blocks],
                [x[:, 0:1] for x in blocks])

    fc, fs, f0 = transforms(hf_ref)
    bc, bs, b0 = transforms(hb_ref)
    ny = lambda s: s[:, 0:1]
    for d in range(-(nb - 1), nb):
        if d >= 1:
            hr = fc[d] + alt * (fc[d - 1] - f0[d - 1])
            hi = fs[d] + alt * fs[d - 1]
            hn = ny(fs[d]) + ny(fs[d - 1]) - f0[d - 1]
        elif d == 0:
            hr = fc[0] + bc[0] - b0[0]
            hi = fs[0] - bs[0]
            hn = ny(fs[0]) + ny(bs[0]) - b0[0]
        else:
            e = -d
            hr = bc[e] + alt * (bc[e - 1] - b0[e - 1])
            hi = -(bs[e] + alt * bs[e - 1])
            hn = ny(bs[e]) + ny(bs[e - 1]) - b0[e - 1]
        li = d + nb - 1
        hr_ref[li] = (hr * w).astype(hr_ref.dtype)
        hi_ref[li] = jnp.where(lane == 0, 0.0, hi * w).astype(hi_ref.dtype)
        hn_ref[li] = hn * w[:, 0:1]


def _lag_spectra(hf2, hb2, cmat, smat, bin_w, L, T):
    R = hf2.shape[0]
    assert T % 2 == 0 and L % T == 0
    nl = 2 * (L // T) - 1
    rb = _blk(R, 256)
    full = lambda shape: pl.BlockSpec(shape, lambda r: (0, 0))
    return _call(
        functools.partial(_lag_spectrum_body, L=L, T=T),
        grid=(R // rb,),
        in_specs=[pl.BlockSpec((rb, L), lambda r: (r, 0)), pl.BlockSpec((rb, L), lambda r: (r, 0)),
                  full((T, T)), full((T, T)), full((1, T))],
        out_specs=[pl.BlockSpec((nl, rb, T), lambda r: (0, r, 0)),
                   pl.BlockSpec((nl, rb, T), lambda r: (0, r, 0)),
                   pl.BlockSpec((nl, rb, 1), lambda r: (0, r, 0))],
        out_shape=[jax.ShapeDtypeStruct((nl, R, T), BF16), jax.ShapeDtypeStruct((nl, R, T), BF16),
                   jax.ShapeDtypeStruct((nl, R, 1), F32)],
        name="hy_lag_spectra",
    )(hf2, hb2, cmat, smat, bin_w)


def _hy_conv_body(z_ref, gate_ref, skip_ref, c_ref, s_ref, st_ref, hr_ref, hi_ref, hn_ref, o_ref,
                  zr_ref, zi_ref, yr_ref, yi_ref, *, T, chunk):
    R, L = z_ref.shape
    nb = L // T
    for j in range(nb):
        zj = z_ref[:, j * T:(j + 1) * T]
        zr_ref[j] = _dot(zj, c_ref[...]).astype(zr_ref.dtype)
        zi_ref[j] = _dot(zj, s_ref[...]).astype(zi_ref.dtype)

    bin0 = lax.broadcasted_iota(jnp.int32, (chunk, T), 1) == 0

    def combine(r, carry):
        rows = pl.ds(pl.multiple_of(r * chunk, chunk), chunk)
        zr = [zr_ref[j, rows, :] for j in range(nb)]
        zi = [zi_ref[j, rows, :] for j in range(nb)]
        zn = [z[:, 0:1].astype(F32) for z in zi]
        for i in range(nb):
            yr = yi = yn = None
            for j in range(nb):
                lag = i - j + nb - 1
                hr, hi = hr_ref[lag, rows, :], hi_ref[lag, rows, :]
                tr = zr[j] * hr - zi[j] * hi
                ti = zr[j] * hi + zi[j] * hr
                tn = zn[j] * hn_ref[lag, rows, :]
                yr, yi, yn = (tr, ti, tn) if j == 0 else (yr + tr, yi + ti, yn + tn)
            yr_ref[i, rows, :] = yr
            yi_ref[i, rows, :] = jnp.where(bin0, yn.astype(yi.dtype), yi)
        return carry

    lax.fori_loop(0, R // chunk, combine, 0)

    skip = skip_ref[...]
    for i in range(nb):
        sl = slice(i * T, (i + 1) * T)
        zc = _dot(yr_ref[i], c_ref[...]) + _dot(yi_ref[i], st_ref[...])
        o_ref[:, sl] = (gate_ref[:, sl] * (zc + skip * z_ref[:, sl])).astype(o_ref.dtype)


def _hy_conv(z2d, z_row0, z_rows_per_b, gates2d, g_row0, g_rows_per_b, skip_col, tables,
             hr, hi, hn, h_row0, B, W, L, T):
    R = _blk(W, 512)
    ncb = W // R
    nb, nl = L // T, hr.shape[0]
    zb, z0 = z_rows_per_b // R, z_row0 // R
    gb, g0 = g_rows_per_b // R, g_row0 // R
    h0 = h_row0 // R
    table = pl.BlockSpec((T, T), lambda c, b: (0, 0))
    spectrum = lambda last: pl.BlockSpec((nl, R, last), lambda c, b: (0, h0 + c, 0),
                                         pipeline_mode=pl.Buffered(1))
    return _call(
        functools.partial(_hy_conv_body, T=T, chunk=16),
        grid=(ncb, B),
        in_specs=[pl.BlockSpec((R, L), lambda c, b: (b * zb + z0 + c, 0)),
                  pl.BlockSpec((R, L), lambda c, b: (b * gb + g0 + c, 0)),
                  pl.BlockSpec((R, 1), lambda c, b: (h0 + c, 0)),
                  table, table, table, spectrum(T), spectrum(T), spectrum(1)],
        out_specs=pl.BlockSpec((R, L), lambda c, b: (b * ncb + c, 0)),
        out_shape=jax.ShapeDtypeStruct((B * W, L), BF16),
        scratch_shapes=[pltpu.VMEM((nb, R, T), BF16)] * 4,
        name="hy_conv",
    )(z2d, gates2d, skip_col, *tables, hr, hi, hn)


def _merge_body(yg_ref, yh_ref, wg_ref, wh_ref, g0_ref, g1_ref, o_ref):
    tg = _dot(yg_ref[...], wg_ref[...])
    th = _dot(yh_ref[...], wh_ref[...], TN)
    o_ref[...] = (g0_ref[...] * tg + g1_ref[...] * th).astype(o_ref.dtype)


def _branch_merge(y_gla, y_hy_t, w_gla_o, w_hy_o, gates, B, L, D):
    VAL, W = y_gla.shape[1], w_hy_o.shape[0]
    bt = _blk(L, 512)
    nt = L // bt
    resident = lambda shape: pl.BlockSpec(shape, lambda b, t: (0, 0), pipeline_mode=pl.Buffered(1))
    return _call(
        _merge_body,
        grid=(B, nt),
        in_specs=[pl.BlockSpec((bt, VAL), lambda b, t: (b * nt + t, 0)),
                  pl.BlockSpec((W, bt), lambda b, t: (b, t)),
                  resident((VAL, D)), resident((W, D)),
                  pl.BlockSpec((bt, D), lambda b, t: (b * nt + t, 0)),
                  pl.BlockSpec((bt, D), lambda b, t: (b * nt + t, 1))],
        out_specs=pl.BlockSpec((bt, D), lambda b, t: (b * nt + t, 0)),
        out_shape=jax.ShapeDtypeStruct((B * L, D), BF16),
        name="branch_merge",
    )(y_gla, y_hy_t, w_gla_o, w_hy_o, gates, gates)


def _proj_ln_body(a_ref, w_ref, x_ref, g_ref, b_ref, o_ref, ob_ref, *, alpha):
    y = alpha * x_ref[...] + _dot(a_ref[...], w_ref[...])
    h = _layernorm_rows(y, g_ref[...], b_ref[...])
    o_ref[...] = h
    ob_ref[...] = h.astype(ob_ref.dtype)


def _proj_ln(a, w, x2d, ln_g, ln_b, alpha):
    M, K = a.shape
    D = w.shape[1]
    bm = _blk(M, 512)
    row = lambda i: (i, 0)
    full = lambda shape: pl.BlockSpec(shape, lambda i: (0, 0))
    return _call(
        functools.partial(_proj_ln_body, alpha=alpha),
        grid=(M // bm,),
        in_specs=[pl.BlockSpec((bm, K), row), full((K, D)), pl.BlockSpec((bm, D), row),
                  full((1, D)), full((1, D))],
        out_specs=[pl.BlockSpec((bm, D), row)] * 2,
        out_shape=[jax.ShapeDtypeStruct((M, D), F32), jax.ShapeDtypeStruct((M, D), BF16)],
        name="out_proj_ln1",
    )(a, w, x2d, ln_g.reshape(1, D), ln_b.reshape(1, D))


def _ff2_ln_body(a_ref, w_ref, h_ref, g_ref, b_ref, o_ref, *, alpha, ln_rows):
    kk = pl.program_id(1)

    @pl.when(kk == 0)
    def _():
        o_ref[...] = jnp.zeros_like(o_ref)

    o_ref[...] += _dot(a_ref[...], w_ref[...])

    @pl.when(kk == pl.num_programs(1) - 1)
    def _():
        g, b = g_ref[...], b_ref[...]

        def norm_rows(r, carry):
            rows = pl.ds(pl.multiple_of(r * ln_rows, ln_rows), ln_rows)
            y = alpha * h_ref[rows, :] + o_ref[rows, :]
            o_ref[rows, :] = _layernorm_rows(y, g, b)
            return carry

        lax.fori_loop(0, o_ref.shape[0] // ln_rows, norm_rows, 0)


def _ff2_ln(a, w, h, ln_g, ln_b, alpha):
    M, K = a.shape
    D = w.shape[1]
    bm, bk = _blk(M, 1024), _blk(K, 1024)
    return _call(
        functools.partial(_ff2_ln_body, alpha=alpha, ln_rows=_blk(bm, 128)),
        grid=(M // bm, K // bk),
        in_specs=[pl.BlockSpec((bm, bk), lambda i, k: (i, k)),
                  pl.BlockSpec((bk, D), lambda i, k: (k, 0)),
                  pl.BlockSpec((bm, D), lambda i, k: (i, 0)),
                  pl.BlockSpec((1, D), lambda i, k: (0, 0)),
                  pl.BlockSpec((1, D), lambda i, k: (0, 0))],
        out_specs=pl.BlockSpec((bm, D), lambda i, k: (i, 0)),
        out_shape=jax.ShapeDtypeStruct((M, D), F32),
        name="ff2_ln2",
    )(a, w, h, ln_g.reshape(1, D), ln_b.reshape(1, D))


def _dft_tables(L):
    LO = _blk(L, 32)
    k = jnp.arange(L, dtype=jnp.int32)
    ang = lambda m: ((m[:, None] * k[None, :]) % (2 * L)).astype(F32) * (math.pi / L)
    a_hi = ang(jnp.arange(L // LO, dtype=jnp.int32) * LO)[:, None, :]
    a_lo = ang(jnp.arange(LO, dtype=jnp.int32))[None, :, :]
    cos_t = (jnp.cos(a_hi) * jnp.cos(a_lo) - jnp.sin(a_hi) * jnp.sin(a_lo)).reshape(L, L)
    sin_t = (jnp.sin(a_hi) * jnp.cos(a_lo) + jnp.cos(a_hi) * jnp.sin(a_lo)).reshape(L, L)
    nyq = jnp.where(k % 2 == 0, 1.0, -1.0).astype(F32)
    smat = jnp.where(k[None, :] == 0, nyq[:, None], sin_t)
    smat_t = jnp.where(k[:, None] == 0, nyq[None, :], sin_t)
    return cos_t.astype(BF16), smat.astype(BF16), smat_t.astype(BF16)


def _position_features(L, emb_dim):
    t = jnp.linspace(0.0, 1.0, L, dtype=F32)[:, None]
    bands = (emb_dim - 1) // 2
    f = jnp.linspace(1e-4, bands - 1, bands, dtype=F32)
    wpos = 2.0 * math.pi * jnp.arange(L, dtype=F32) / L
    ang = wpos[:, None] * f[None, :]
    return jnp.concatenate([t, jnp.cos(ang), -jnp.sin(ang)], axis=-1).T


def _layer(h, w_in, gla_wa2_f, gla_ba_f, gla_wa2_b, gla_ba_b, gla_norm_g, w_gla_o,
           hy_conv_w, hy_conv_b, hy_w1, hy_b1, hy_w2, hy_b2, hy_w3, hy_b3, hy_w4, hy_b4,
           hy_freq, hy_skip, w_hy_o, w_out, ln1_g, ln1_b, w_ff1, w_ff2, ln2_g, ln2_b, alpha):
    B, L, D = h.shape
    M = B * L
    rank, KEY = gla_wa2_f.shape
    VAL = gla_norm_g.shape[0]
    n_ord, W = hy_skip.shape

    sizes = (KEY, KEY, VAL, VAL, rank, rank, (n_ord + 1) * W, 2 * D)
    offs = [int(v) for v in np.concatenate([[0], np.cumsum(sizes)])]
    w_in_t = w_in.T
    w_hy_t = w_in_t[offs[6]:offs[7]].astype(BF16)

    x2d = h.reshape(M, D)
    gla_block = _blk(L, GLA_BLOCK)
    cum_f, cum_b, xb = _gla_decays(x2d, w_in_t, offs[4], gla_wa2_f, gla_ba_f, gla_wa2_b, gla_ba_b, gla_block)

    p_qkvr = _matmul_f32w(xb, w_in_t, offs[4], out_dtype=BF16, name="in_proj_qkvr", bm=2048, transposed=True)
    gates = _matmul_f32w(xb, w_in_t, sizes[7], out_dtype=BF16, act="sigmoid", name="in_proj_gates", bm=2048,
                         transposed=True, row0=offs[7])
    conv_params = jnp.concatenate([hy_conv_w.T, hy_conv_b[:, None]], axis=1)
    u_t = _hy_proj(w_hy_t, xb, conv_params, B, L)

    y_gla = _gla(p_qkvr, cum_f, cum_b, gla_norm_g, B, L, KEY, VAL, gla_block)

    emb_t = _position_features(L, hy_w1.shape[0])
    min_decay = math.log(HY_DECAY_TARGET) / HY_SLOW_DECAY
    max_decay = math.log(HY_DECAY_TARGET) / HY_FAST_DECAY
    deltas = jnp.abs(jnp.linspace(min_decay, max_decay, W, dtype=F32)).reshape(W, 1)
    hf2, hb2 = _hyena_filters(emb_t, hy_w1, hy_b1, hy_w2, hy_b2, hy_w3, hy_b3,
                              hy_w4, hy_b4, hy_freq, deltas, W)
    T = _blk(L // 2, HY_BLOCK)
    tables = _dft_tables(T)
    bin_w = jnp.full((1, T), 1.0 / T, F32).at[0, 0].set(0.5 / T)
    h_re, h_im, h_ny = _lag_spectra(hf2, hb2, tables[0], tables[1], bin_w, L, T)

    skip_col = hy_skip.reshape(n_ord * W, 1)
    z, z_row0, z_rows = u_t, 0, (n_ord + 1) * W
    for o in range(n_ord):
        z = _hy_conv(z, z_row0, z_rows, u_t, (o + 1) * W, (n_ord + 1) * W, skip_col, tables,
                     h_re, h_im, h_ny, o * W, B, W, L, T)
        z_row0, z_rows = 0, W
    y_hy_t = z

    merged = _branch_merge(y_gla, y_hy_t, w_gla_o.astype(BF16), w_hy_o.astype(BF16), gates, B, L, D)
    h1, h1b = _proj_ln(merged, w_out.astype(BF16), x2d, ln1_g, ln1_b, alpha)
    act = _matmul_f32w(h1b, w_ff1, w_ff1.shape[1], out_dtype=BF16, act="relu2", name="ff1_relu2")
    out = _ff2_ln(act, w_ff2.astype(BF16), h1, ln2_g, ln2_b, alpha)
    return out.reshape(B, L, D)


def kernel(x, w_in, gla_wa2_f, gla_ba_f, gla_wa2_b, gla_ba_b, gla_norm_g, w_gla_o, hy_conv_w, hy_conv_b, hy_w1, hy_b1, hy_w2, hy_b2, hy_w3, hy_b3, hy_w4, hy_b4, hy_freq, hy_skip, w_hy_o, w_out, ln1_g, ln1_b, w_ff1, w_ff2, ln2_g, ln2_b):
    depth = w_in.shape[0]
    alpha = (2 * depth) ** 0.25
    params = (w_in, gla_wa2_f, gla_ba_f, gla_wa2_b, gla_ba_b, gla_norm_g, w_gla_o, hy_conv_w, hy_conv_b,
              hy_w1, hy_b1, hy_w2, hy_b2, hy_w3, hy_b3, hy_w4, hy_b4, hy_freq, hy_skip, w_hy_o, w_out,
              ln1_g, ln1_b, w_ff1, w_ff2, ln2_g, ln2_b)
    h = x
    for l in range(depth):
        h = _layer(h, *(p[l] for p in params), alpha)
    return h
```

```python
import functools
import math

import jax
import jax.numpy as jnp
import numpy as np
from jax import lax
from jax.experimental import pallas as pl
from jax.experimental.pallas import tpu as pltpu

F32 = jnp.float32
BF16 = jnp.bfloat16

GLA_HEADS = 4
GLA_TAU = 16.0
GLA_CHUNK = 64
GLA_BLOCK = 256
HY_BLOCK = 512
HY_FAST_DECAY = 0.3
HY_SLOW_DECAY = 1.5
HY_DECAY_TARGET = 1e-2
LN_EPS = 1e-5

V7X_VMEM_LIMIT_BYTES = 56 * 1024 * 1024
V7X_LANES = 128
V7X_SUBLANES = 8

NN = (((1,), (0,)), ((), ()))
NT = (((1,), (1,)), ((), ()))
TN = (((0,), (0,)), ((), ()))


def _dot(a, b, dims=NN):
    return lax.dot_general(a.astype(BF16), b.astype(BF16), dims, preferred_element_type=F32)


def _split2(a):
    h1 = a.astype(BF16)
    h2 = (a - h1.astype(F32)).astype(BF16)
    return h1, h2


def _dot_f32(a, b, dims=NN):
    a1, a2 = _split2(a)
    b1, b2 = _split2(b)
    d = lambda x, y: lax.dot_general(x, y, dims, preferred_element_type=F32)
    return (d(a2, b1) + d(a1, b2)) + d(a1, b1)


def _dot_exact_lhs(a_bf16, b, dims=NN):
    b1, b2 = _split2(b)
    d = lambda y: lax.dot_general(a_bf16, y, dims, preferred_element_type=F32)
    return d(b2) + d(b1)


def _call(body, *, grid, in_specs, out_specs, out_shape, name, scratch_shapes=()):
    return pl.pallas_call(
        body,
        grid=grid,
        in_specs=in_specs,
        out_specs=out_specs,
        out_shape=out_shape,
        scratch_shapes=scratch_shapes,
        compiler_params=pltpu.CompilerParams(
            dimension_semantics=("arbitrary",) * len(grid),
            vmem_limit_bytes=V7X_VMEM_LIMIT_BYTES,
        ),
        name=name,
    )


def _blk(n, want):
    b = min(n, want)
    while n % b:
        b //= 2
    return b


def _layernorm_rows(y, g, b):
    mu = jnp.mean(y, axis=-1, keepdims=True)
    d = y - mu
    var = jnp.mean(d * d, axis=-1, keepdims=True)
    return d * lax.rsqrt(var + LN_EPS) * g + b


def _sigmoid(x):
    return 0.5 * jnp.tanh(0.5 * x) + 0.5


def _act(acc, act):
    if act == "sigmoid":
        return _sigmoid(acc)
    if act == "relu2":
        return jnp.square(jnp.maximum(acc, 0.0))
    return acc


def _weight_spec(K, bn, transposed, index, row0=0):
    if transposed and row0:
        assert row0 % V7X_SUBLANES == 0
        return pl.BlockSpec((pl.Element(bn), pl.Element(K)),
                            lambda *g: (pl.multiple_of(row0 + index(*g) * bn, V7X_SUBLANES), 0))
    if transposed:
        return pl.BlockSpec((bn, K), lambda *g: (index(*g), 0))
    assert row0 == 0
    return pl.BlockSpec((K, bn), lambda *g: (0, index(*g)))


def _mm_f32w_body(a_ref, w_ref, o_ref, wb_ref, *, act, dims):
    @pl.when(pl.program_id(1) == 0)
    def _():
        wb_ref[...] = w_ref[...].astype(wb_ref.dtype)

    o_ref[...] = _act(_dot(a_ref[...], wb_ref[...], dims), act).astype(o_ref.dtype)


def _matmul_f32w(a, w, n_out, *, out_dtype, act=None, name, bm=1024, bn=1024, transposed=False, row0=0):
    M, K = a.shape
    bm, bn = _blk(M, bm), _blk(n_out, bn)
    return _call(
        functools.partial(_mm_f32w_body, act=act, dims=NT if transposed else NN),
        grid=(n_out // bn, M // bm),
        in_specs=[pl.BlockSpec((bm, K), lambda j, i: (i, 0)),
                  _weight_spec(K, bn, transposed, lambda j, i: j, row0)],
        out_specs=pl.BlockSpec((bm, bn), lambda j, i: (i, j)),
        out_shape=jax.ShapeDtypeStruct((M, n_out), out_dtype),
        scratch_shapes=[pltpu.VMEM((bn, K) if transposed else (K, bn), BF16)],
        name=name,
    )(a, w)


def _hy_proj_body(w_ref, x_ref, cp_ref, o_ref):
    u = _dot(w_ref[...], x_ref[...], NT)
    L = u.shape[1]
    t = lax.broadcasted_iota(jnp.int32, u.shape, 1)
    prev = jnp.where(t == 0, 0.0, pltpu.roll(u, 1, axis=1))
    nxt = jnp.where(t == L - 1, 0.0, pltpu.roll(u, L - 1, axis=1))
    cp = cp_ref[...]
    o_ref[...] = (cp[:, 0:1] * prev + cp[:, 1:2] * u + cp[:, 2:3] * nxt + cp[:, 3:4]).astype(o_ref.dtype)


def _hy_proj(w_t, xb, conv_params, B, L):
    C3, D = w_t.shape
    bc = _blk(C3, 1024)
    nc = C3 // bc
    return _call(
        _hy_proj_body,
        grid=(B, nc),
        in_specs=[pl.BlockSpec((bc, D), lambda b, c: (c, 0)),
                  pl.BlockSpec((L, D), lambda b, c: (b, 0)),
                  pl.BlockSpec((bc, 4), lambda b, c: (c, 0))],
        out_specs=pl.BlockSpec((bc, L), lambda b, c: (b * nc + c, 0)),
        out_shape=jax.ShapeDtypeStruct((B * C3, L), BF16),
        name="hy_proj_conv",
    )(w_t, xb, conv_params)


def _log_sigmoid(z):
    return -(jnp.maximum(-z, 0.0) + jnp.log(1.0 + jnp.exp(-jnp.abs(z))))


def _decay_body(x_ref, wab_ref, w2f_ref, bf_ref, w2b_ref, bb_ref, cf_ref, cb_ref, xb_ref, *, block):
    xb = x_ref[...].astype(xb_ref.dtype)
    xb_ref[...] = xb
    ab = _dot(xb, wab_ref[...], NT)
    zf = _dot(ab, w2f_ref[...]) + bf_ref[...]
    zb = _dot(ab, w2b_ref[...]) + bb_ref[...]
    laf = _log_sigmoid(zf) * (1.0 / GLA_TAU)
    lab = _log_sigmoid(zb) * (1.0 / GLA_TAU)
    T = block
    row = lax.broadcasted_iota(jnp.int32, (T, T), 0)
    col = lax.broadcasted_iota(jnp.int32, (T, T), 1)
    tri_f = (col <= row).astype(BF16)
    tri_b = (col >= row).astype(BF16)
    for s in range(xb.shape[0] // T):
        cf_ref[s * T:(s + 1) * T, :] = _dot_exact_lhs(tri_f, laf[s * T:(s + 1) * T, :])
        cb_ref[s * T:(s + 1) * T, :] = _dot_exact_lhs(tri_b, lab[s * T:(s + 1) * T, :])


def _gla_decays(x2d, w_in_t, ab_col, wa2_f, ba_f, wa2_b, ba_b, block):
    M, D = x2d.shape
    rank, KEY = wa2_f.shape
    P = V7X_LANES
    assert 2 * rank <= P and ab_col % P == 0
    w2f = jnp.pad(wa2_f, ((0, P - rank), (0, 0)))
    w2b = jnp.pad(wa2_b, ((rank, P - 2 * rank), (0, 0)))
    bm = max(_blk(M, 512), block)
    full = lambda shape: pl.BlockSpec(shape, lambda i: (0, 0))
    out = jax.ShapeDtypeStruct((M, KEY), F32)
    return _call(
        functools.partial(_decay_body, block=block),
        grid=(M // bm,),
        in_specs=[pl.BlockSpec((bm, D), lambda i: (i, 0)), pl.BlockSpec((P, D), lambda i: (ab_col // P, 0)),
                  full((P, KEY)), full((1, KEY)), full((P, KEY)), full((1, KEY))],
        out_specs=[pl.BlockSpec((bm, KEY), lambda i: (i, 0))] * 2 + [pl.BlockSpec((bm, D), lambda i: (i, 0))],
        out_shape=[out, out, jax.ShapeDtypeStruct((M, D), BF16)],
        name="gla_decays",
    )(x2d, w_in_t, w2f, ba_f.reshape(1, KEY), w2b, ba_b.reshape(1, KEY))


def _gla_body(q_ref, k_ref, v_ref, r_ref, cf_ref, cb_ref, g_ref, y_ref, o_acc,
              stf_ref, sf_ref, qbf_ref, kef_ref, klf_ref,
              stb_ref, sb_ref, qbb_ref, keb_ref, klb_ref, *, block, sub):
    L, DK = q_ref.shape
    T, C = block, sub
    n, ns = L // T, T // C
    scale = DK ** -0.5
    row = lax.broadcasted_iota(jnp.int32, (C, C), 0)
    col = lax.broadcasted_iota(jnp.int32, (C, C), 1)

    def boundary_row(cum_ref, start, pick):
        return cum_ref[pl.ds(pl.multiple_of(start, 8), 8), :][pick:pick + 1, :]

    def block_step(blk, cum_ref, st_ref, s_ref, qb_ref, ke_ref, kl_ref, forward, first_visit):
        r0 = pl.multiple_of(blk * T, T)
        tot = boundary_row(cum_ref, r0 + T - 8, 7) if forward else boundary_row(cum_ref, r0, 0)
        mask = (col <= row) if forward else (col > row)
        refs = {}
        for I in (range(ns) if forward else range(ns - 1, -1, -1)):
            sl = slice(I * C, (I + 1) * C)
            rows = pl.ds(r0 + I * C, C)
            if forward:
                ref = boundary_row(cum_ref, r0 + I * C - 8, 7) if I > 0 else jnp.zeros_like(tot)
            else:
                ref = boundary_row(cum_ref, r0 + (I + 1) * C, 0) if I < ns - 1 else jnp.zeros_like(tot)
            d = cum_ref[rows, :] - ref
            q_loc = q_ref[rows, :] * (scale * jnp.exp(d))
            k_loc = k_ref[rows, :] * jnp.exp(-d)
            qb_ref[sl, :] = (q_loc * jnp.exp(ref)).astype(BF16)
            ke_ref[sl, :] = (k_loc * jnp.exp(tot - ref)).astype(BF16)
            k_loc = k_loc.astype(BF16)
            kl_ref[sl, :] = k_loc
            s_ref[sl, sl] = jnp.where(mask, _dot(q_loc, k_loc, NT), 0.0).astype(BF16)
            for J, ref_j in refs.items():
                sj = slice(J * C, (J + 1) * C)
                s_ref[sl, sj] = _dot(q_loc * jnp.exp(ref - ref_j), kl_ref[sj, :], NT).astype(BF16)
            refs[I] = ref
        rows = pl.ds(r0, T)
        vb = v_ref[rows, :].astype(BF16)
        st = st_ref[...]
        o = _dot(s_ref[...], vb) + _dot(qb_ref[...], st, NT)
        st_ref[...] = st * jnp.exp(tot) + _dot(vb, ke_ref[...], TN)
        if first_visit:
            o_acc[rows, :] = o
        else:
            o = o_acc[rows, :] + o
            o = o * lax.rsqrt(jnp.mean(o * o, axis=-1, keepdims=True) + LN_EPS) * g_ref[...]
            r = r_ref[rows, :].astype(F32)
            y_ref[rows, :] = (o * (r * _sigmoid(r))).astype(y_ref.dtype)

    fwd = functools.partial(block_step, cum_ref=cf_ref, st_ref=stf_ref, s_ref=sf_ref, qb_ref=qbf_ref,
                            ke_ref=kef_ref, kl_ref=klf_ref, forward=True)
    bwd = functools.partial(block_step, cum_ref=cb_ref, st_ref=stb_ref, s_ref=sb_ref, qb_ref=qbb_ref,
                            ke_ref=keb_ref, kl_ref=klb_ref, forward=False)

    for ref in (stf_ref, stb_ref, sf_ref, sb_ref):
        ref[...] = jnp.zeros_like(ref)

    def sweep(lo, hi, step, unroll=1):
        def body(i, carry):
            step(i)
            return carry
        lax.fori_loop(lo, hi, body, 0, unroll=unroll)

    if n % 2 == 0:
        def first_half(i):
            fwd(i, first_visit=True)
            bwd(n - 1 - i, first_visit=True)

        def second_half(i):
            fwd(i, first_visit=False)
            bwd(n - 1 - i, first_visit=False)

        sweep(0, n // 2, first_half, unroll=True)
        sweep(n // 2, n, second_half)
    else:
        sweep(0, n, lambda i: fwd(i, first_visit=True))
        sweep(0, n, lambda i: bwd(n - 1 - i, first_visit=False))


def _gla(p_qkvr, cum_f, cum_b, norm_g, B, L, KEY, VAL, block):
    H = GLA_HEADS
    DK, DV = KEY // H, VAL // H
    M = B * L
    kq = KEY // DK
    vq = 2 * KEY // DV
    rq = (2 * KEY + VAL) // DV
    T = block
    direction_scratch = [pltpu.VMEM((DV, DK), F32), pltpu.VMEM((T, T), BF16), pltpu.VMEM((T, DK), BF16),
                         pltpu.VMEM((T, DK), BF16), pltpu.VMEM((T, DK), BF16)]
    return _call(
        functools.partial(_gla_body, block=T, sub=_blk(T, GLA_CHUNK)),
        grid=(B, H),
        in_specs=[pl.BlockSpec((L, DK), lambda b, h: (b, h)),
                  pl.BlockSpec((L, DK), lambda b, h: (b, kq + h)),
                  pl.BlockSpec((L, DV), lambda b, h: (b, vq + h)),
                  pl.BlockSpec((L, DV), lambda b, h: (b, rq + h)),
                  pl.BlockSpec((L, DK), lambda b, h: (b, h)),
                  pl.BlockSpec((L, DK), lambda b, h: (b, h)),
                  pl.BlockSpec((1, DV), lambda b, h: (0, h))],
        out_specs=pl.BlockSpec((L, DV), lambda b, h: (b, h)),
        out_shape=jax.ShapeDtypeStruct((M, VAL), BF16),
        scratch_shapes=[pltpu.VMEM((L, DV), F32)] + direction_scratch * 2,
        name="gla_bidir",
    )(p_qkvr, p_qkvr, p_qkvr, p_qkvr, cum_f, cum_b, norm_g.reshape(1, VAL))


def _filter_body(emb_ref, w1_ref, b1_ref, w2_ref, b2_ref, w3_ref, b3_ref, fr_ref,
                 w4f_ref, b4f_ref, w4b_ref, b4b_ref, dl_ref, hf_ref, hb_ref, h_ref):
    @pl.when((pl.program_id(0) == 0) & (pl.program_id(1) == 0))
    def _():
        fr = fr_ref[...]
        h = jnp.sin(fr * (_dot_f32(w1_ref[...], emb_ref[...]) + b1_ref[...]))
        h = jnp.sin(fr * (_dot_f32(w2_ref[...], h) + b2_ref[...]))
        h_ref[...] = jnp.sin(fr * (_dot_f32(w3_ref[...], h) + b3_ref[...]))

    h = h_ref[...]
    t_lin = emb_ref[0:1, :]
    decay = jnp.exp(-t_lin * dl_ref[...])
    hf_ref[...] = (_dot_f32(w4f_ref[...], h) + b4f_ref[...]) * decay
    hb_ref[...] = (_dot_f32(w4b_ref[...], h) + b4b_ref[...]) * decay


def _hyena_filters(emb_t, w1, b1, w2, b2, w3, b3, w4, b4, freq, deltas, W):
    L = emb_t.shape[1]
    HID = w1.shape[1]
    n_ord = w4.shape[1] // (2 * W)
    EMB = -(-emb_t.shape[0] // V7X_LANES) * V7X_LANES
    w1 = jnp.pad(w1, ((0, EMB - w1.shape[0]), (0, 0)))
    emb_t = jnp.pad(emb_t, ((0, EMB - emb_t.shape[0]), (0, 0)))
    cb = _blk(W, 512)
    ncb = W // cb
    w4t = w4.T
    b4c = b4.reshape(-1, 1)
    colv = lambda v: v.reshape(-1, 1)
    full = lambda shape: pl.BlockSpec(shape, lambda o, c: (0, 0))
    rows = n_ord * W
    return _call(
        _filter_body,
        grid=(n_ord, ncb),
        in_specs=[full((EMB, L)), full((HID, EMB)), full((HID, 1)), full((HID, HID)), full((HID, 1)),
                  full((HID, HID)), full((HID, 1)), full((HID, 1)),
                  pl.BlockSpec((cb, HID), lambda o, c: (o * 2 * ncb + c, 0)),
                  pl.BlockSpec((cb, 1), lambda o, c: (o * 2 * ncb + c, 0)),
                  pl.BlockSpec((cb, HID), lambda o, c: (o * 2 * ncb + ncb + c, 0)),
                  pl.BlockSpec((cb, 1), lambda o, c: (o * 2 * ncb + ncb + c, 0)),
                  pl.BlockSpec((cb, 1), lambda o, c: (c, 0))],
        out_specs=[pl.BlockSpec((cb, L), lambda o, c: (o * ncb + c, 0))] * 2,
        out_shape=[jax.ShapeDtypeStruct((rows, L), F32)] * 2,
        scratch_shapes=[pltpu.VMEM((HID, L), F32)],
        name="hy_filters",
    )(emb_t, w1.T, colv(b1), w2.T, colv(b2), w3.T, colv(b3), colv(freq), w4t, b4c, w4t, b4c, deltas)


def _lag_spectrum_body(hf_ref, hb_ref, c_ref, s_ref, w_ref, hr_ref, hi_ref, hn_ref, *, L, T):
    nb = L // T
    rows = hf_ref.shape[0]
    lane = lax.broadcasted_iota(jnp.int32, (rows, T), 1)
    alt = jnp.where((lane & 1) == 0, 1.0, -1.0)
    w = w_ref[...]

    def transforms(x_ref):
        blocks = [x_ref[:, n * T:(n + 1) * T] for n in range(nb)]
        return ([_dot(x, c_ref[...]) for x in blocks], [_dot(x, s_ref[...]) for x in blocks],
                [x[:, 0:1] for x in blocks])

    fc, fs, f0 = transforms(hf_ref)
    bc, bs, b0 = transforms(hb_ref)
    ny = lambda s: s[:, 0:1]
    for d in range(-(nb - 1), nb):
        if d >= 1:
            hr = fc[d] + alt * (fc[d - 1] - f0[d - 1])
            hi = fs[d] + alt * fs[d - 1]
            hn = ny(fs[d]) + ny(fs[d - 1]) - f0[d - 1]
        elif d == 0:
            hr = fc[0] + bc[0] - b0[0]
            hi = fs[0] - bs[0]
            hn = ny(fs[0]) + ny(bs[0]) - b0[0]
        else:
            e = -d
            hr = bc[e] + alt * (bc[e - 1] - b0[e - 1])
            hi = -(bs[e] + alt * bs[e - 1])
            hn = ny(bs[e]) + ny(bs[e - 1]) - b0[e - 1]
        li = d + nb - 1
        hr_ref[li] = (hr * w).astype(hr_ref.dtype)
        hi_ref[li] = jnp.where(lane == 0, 0.0, hi * w).astype(hi_ref.dtype)
        hn_ref[li] = hn * w[:, 0:1]


def _lag_spectra(hf2, hb2, cmat, smat, bin_w, L, T):
    R = hf2.shape[0]
    assert T % 2 == 0 and L % T == 0
    nl = 2 * (L // T) - 1
    rb = _blk(R, 256)
    full = lambda shape: pl.BlockSpec(shape, lambda r: (0, 0))
    return _call(
        functools.partial(_lag_spectrum_body, L=L, T=T),
        grid=(R // rb,),
        in_specs=[pl.BlockSpec((rb, L), lambda r: (r, 0)), pl.BlockSpec((rb, L), lambda r: (r, 0)),
                  full((T, T)), full((T, T)), full((1, T))],
        out_specs=[pl.BlockSpec((nl, rb, T), lambda r: (0, r, 0)),
                   pl.BlockSpec((nl, rb, T), lambda r: (0, r, 0)),
                   pl.BlockSpec((nl, rb, 1), lambda r: (0, r, 0))],
        out_shape=[jax.ShapeDtypeStruct((nl, R, T), BF16), jax.ShapeDtypeStruct((nl, R, T), BF16),
                   jax.ShapeDtypeStruct((nl, R, 1), F32)],
        name="hy_lag_spectra",
    )(hf2, hb2, cmat, smat, bin_w)


def _hy_conv_body(z_ref, gate_ref, skip_ref, c_ref, s_ref, st_ref, hr_ref, hi_ref, hn_ref, o_ref,
                  zr_ref, zi_ref, yr_ref, yi_ref, *, T, chunk):
    R, L = z_ref.shape
    nb = L // T
    for j in range(nb):
        zj = z_ref[:, j * T:(j + 1) * T]
        zr_ref[j] = _dot(zj, c_ref[...]).astype(zr_ref.dtype)
        zi_ref[j] = _dot(zj, s_ref[...]).astype(zi_ref.dtype)

    bin0 = lax.broadcasted_iota(jnp.int32, (chunk, T), 1) == 0

    def combine(r, carry):
        rows = pl.ds(pl.multiple_of(r * chunk, chunk), chunk)
        zr = [zr_ref[j, rows, :] for j in range(nb)]
        zi = [zi_ref[j, rows, :] for j in range(nb)]
        zn = [z[:, 0:1].astype(F32) for z in zi]
        for i in range(nb):
            yr = yi = yn = None
            for j in range(nb):
                lag = i - j + nb - 1
                hr, hi = hr_ref[lag, rows, :], hi_ref[lag, rows, :]
                tr = zr[j] * hr - zi[j] * hi
                ti = zr[j] * hi + zi[j] * hr
                tn = zn[j] * hn_ref[lag, rows, :]
                yr, yi, yn = (tr, ti, tn) if j == 0 else (yr + tr, yi + ti, yn + tn)
            yr_ref[i, rows, :] = yr
            yi_ref[i, rows, :] = jnp.where(bin0, yn.astype(yi.dtype), yi)
        return carry

    lax.fori_loop(0, R // chunk, combine, 0)

    skip = skip_ref[...]
    for i in range(nb):
        sl = slice(i * T, (i + 1) * T)
        zc = _dot(yr_ref[i], c_ref[...]) + _dot(yi_ref[i], st_ref[...])
        o_ref[:, sl] = (gate_ref[:, sl] * (zc + skip * z_ref[:, sl])).astype(o_ref.dtype)


def _hy_conv(z2d, z_row0, z_rows_per_b, gates2d, g_row0, g_rows_per_b, skip_col, tables,
             hr, hi, hn, h_row0, B, W, L, T):
    R = _blk(W, 512)
    ncb = W // R
    nb, nl = L // T, hr.shape[0]
    zb, z0 = z_rows_per_b // R, z_row0 // R
    gb, g0 = g_rows_per_b // R, g_row0 // R
    h0 = h_row0 // R
    table = pl.BlockSpec((T, T), lambda c, b: (0, 0))
    spectrum = lambda last: pl.BlockSpec((nl, R, last), lambda c, b: (0, h0 + c, 0),
                                         pipeline_mode=pl.Buffered(1))
    return _call(
        functools.partial(_hy_conv_body, T=T, chunk=16),
        grid=(ncb, B),
        in_specs=[pl.BlockSpec((R, L), lambda c, b: (b * zb + z0 + c, 0)),
                  pl.BlockSpec((R, L), lambda c, b: (b * gb + g0 + c, 0)),
                  pl.BlockSpec((R, 1), lambda c, b: (h0 + c, 0)),
                  table, table, table, spectrum(T), spectrum(T), spectrum(1)],
        out_specs=pl.BlockSpec((R, L), lambda c, b: (b * ncb + c, 0)),
        out_shape=jax.ShapeDtypeStruct((B * W, L), BF16),
        scratch_shapes=[pltpu.VMEM((nb, R, T), BF16)] * 4,
        name="hy_conv",
    )(z2d, gates2d, skip_col, *tables, hr, hi, hn)


def _merge_body(yg_ref, yh_ref, wg_ref, wh_ref, g0_ref, g1_ref, o_ref):
    tg = _dot(yg_ref[...], wg_ref[...])
    th = _dot(yh_ref[...], wh_ref[...], TN)
    o_ref[...] = (g0_ref[...] * tg + g1_ref[...] * th).astype(o_ref.dtype)


def _branch_merge(y_gla, y_hy_t, w_gla_o, w_hy_o, gates, B, L, D):
    VAL, W = y_gla.shape[1], w_hy_o.shape[0]
    bt = _blk(L, 512)
    nt = L // bt
    resident = lambda shape: pl.BlockSpec(shape, lambda b, t: (0, 0), pipeline_mode=pl.Buffered(1))
    return _call(
        _merge_body,
        grid=(B, nt),
        in_specs=[pl.BlockSpec((bt, VAL), lambda b, t: (b * nt + t, 0)),
                  pl.BlockSpec((W, bt), lambda b, t: (b, t)),
                  resident((VAL, D)), resident((W, D)),
                  pl.BlockSpec((bt, D), lambda b, t: (b * nt + t, 0)),
                  pl.BlockSpec((bt, D), lambda b, t: (b * nt + t, 1))],
        out_specs=pl.BlockSpec((bt, D), lambda b, t: (b * nt + t, 0)),
        out_shape=jax.ShapeDtypeStruct((B * L, D), BF16),
        name="branch_merge",
    )(y_gla, y_hy_t, w_gla_o, w_hy_o, gates, gates)


def _proj_ln_body(a_ref, w_ref, x_ref, g_ref, b_ref, o_ref, ob_ref, *, alpha):
    y = alpha * x_ref[...] + _dot(a_ref[...], w_ref[...])
    h = _layernorm_rows(y, g_ref[...], b_ref[...])
    o_ref[...] = h
    ob_ref[...] = h.astype(ob_ref.dtype)


def _proj_ln(a, w, x2d, ln_g, ln_b, alpha):
    M, K = a.shape
    D = w.shape[1]
    bm = _blk(M, 512)
    row = lambda i: (i, 0)
    full = lambda shape: pl.BlockSpec(shape, lambda i: (0, 0))
    return _call(
        functools.partial(_proj_ln_body, alpha=alpha),
        grid=(M // bm,),
        in_specs=[pl.BlockSpec((bm, K), row), full((K, D)), pl.BlockSpec((bm, D), row),
                  full((1, D)), full((1, D))],
        out_specs=[pl.BlockSpec((bm, D), row)] * 2,
        out_shape=[jax.ShapeDtypeStruct((M, D), F32), jax.ShapeDtypeStruct((M, D), BF16)],
        name="out_proj_ln1",
    )(a, w, x2d, ln_g.reshape(1, D), ln_b.reshape(1, D))


def _ff2_ln_body(a_ref, w_ref, h_ref, g_ref, b_ref, o_ref, *, alpha, ln_rows):
    kk = pl.program_id(1)

    @pl.when(kk == 0)
    def _():
        o_ref[...] = jnp.zeros_like(o_ref)

    o_ref[...] += _dot(a_ref[...], w_ref[...])

    @pl.when(kk == pl.num_programs(1) - 1)
    def _():
        g, b = g_ref[...], b_ref[...]

        def norm_rows(r, carry):
            rows = pl.ds(pl.multiple_of(r * ln_rows, ln_rows), ln_rows)
            y = alpha * h_ref[rows, :] + o_ref[rows, :]
            o_ref[rows, :] = _layernorm_rows(y, g, b)
            return carry

        lax.fori_loop(0, o_ref.shape[0] // ln_rows, norm_rows, 0)


def _ff2_ln(a, w, h, ln_g, ln_b, alpha):
    M, K = a.shape
    D = w.shape[1]
    bm, bk = _blk(M, 1024), _blk(K, 1024)
    return _call(
        functools.partial(_ff2_ln_body, alpha=alpha, ln_rows=_blk(bm, 128)),
        grid=(M // bm, K // bk),
        in_specs=[pl.BlockSpec((bm, bk), lambda i, k: (i, k)),
                  pl.BlockSpec((bk, D), lambda i, k: (k, 0)),
                  pl.BlockSpec((bm, D), lambda i, k: (i, 0)),
                  pl.BlockSpec((1, D), lambda i, k: (0, 0)),
                  pl.BlockSpec((1, D), lambda i, k: (0, 0))],
        out_specs=pl.BlockSpec((bm, D), lambda i, k: (i, 0)),
        out_shape=jax.ShapeDtypeStruct((M, D), F32),
        name="ff2_ln2",
    )(a, w, h, ln_g.reshape(1, D), ln_b.reshape(1, D))


def _dft_tables(L):
    LO = _blk(L, 32)
    k = jnp.arange(L, dtype=jnp.int32)
    ang = lambda m: ((m[:, None] * k[None, :]) % (2 * L)).astype(F32) * (math.pi / L)
    a_hi = ang(jnp.arange(L // LO, dtype=jnp.int32) * LO)[:, None, :]
    a_lo = ang(jnp.arange(LO, dtype=jnp.int32))[None, :, :]
    cos_t = (jnp.cos(a_hi) * jnp.cos(a_lo) - jnp.sin(a_hi) * jnp.sin(a_lo)).reshape(L, L)
    sin_t = (jnp.sin(a_hi) * jnp.cos(a_lo) + jnp.cos(a_hi) * jnp.sin(a_lo)).reshape(L, L)
    nyq = jnp.where(k % 2 == 0, 1.0, -1.0).astype(F32)
    smat = jnp.where(k[None, :] == 0, nyq[:, None], sin_t)
    smat_t = jnp.where(k[:, None] == 0, nyq[None, :], sin_t)
    return cos_t.astype(BF16), smat.astype(BF16), smat_t.astype(BF16)


def _position_features(L, emb_dim):
    t = jnp.linspace(0.0, 1.0, L, dtype=F32)[:, None]
    bands = (emb_dim - 1) // 2
    f = jnp.linspace(1e-4, bands - 1, bands, dtype=F32)
    wpos = 2.0 * math.pi * jnp.arange(L, dtype=F32) / L
    ang = wpos[:, None] * f[None, :]
    return jnp.concatenate([t, jnp.cos(ang), -jnp.sin(ang)], axis=-1).T


def _layer(h, w_in, gla_wa2_f, gla_ba_f, gla_wa2_b, gla_ba_b, gla_norm_g, w_gla_o,
           hy_conv_w, hy_conv_b, hy_w1, hy_b1, hy_w2, hy_b2, hy_w3, hy_b3, hy_w4, hy_b4,
           hy_freq, hy_skip, w_hy_o, w_out, ln1_g, ln1_b, w_ff1, w_ff2, ln2_g, ln2_b, alpha):
    B, L, D = h.shape
    M = B * L
    rank, KEY = gla_wa2_f.shape
    VAL = gla_norm_g.shape[0]
    n_ord, W = hy_skip.shape

    sizes = (KEY, KEY, VAL, VAL, rank, rank, (n_ord + 1) * W, 2 * D)
    offs = [int(v) for v in np.concatenate([[0], np.cumsum(sizes)])]
    w_in_t = w_in.T
    w_hy_t = w_in_t[offs[6]:offs[7]].astype(BF16)

    x2d = h.reshape(M, D)
    gla_block = _blk(L, GLA_BLOCK)
    cum_f, cum_b, xb = _gla_decays(x2d, w_in_t, offs[4], gla_wa2_f, gla_ba_f, gla_wa2_b, gla_ba_b, gla_block)

    p_qkvr = _matmul_f32w(xb, w_in_t, offs[4], out_dtype=BF16, name="in_proj_qkvr", bm=2048, transposed=True)
    gates = _matmul_f32w(xb, w_in_t, sizes[7], out_dtype=BF16, act="sigmoid", name="in_proj_gates", bm=2048,
                         transposed=True, row0=offs[7])
    conv_params = jnp.concatenate([hy_conv_w.T, hy_conv_b[:, None]], axis=1)
    u_t = _hy_proj(w_hy_t, xb, conv_params, B, L)

    y_gla = _gla(p_qkvr, cum_f, cum_b, gla_norm_g, B, L, KEY, VAL, gla_block)

    emb_t = _position_features(L, hy_w1.shape[0])
    min_decay = math.log(HY_DECAY_TARGET) / HY_SLOW_DECAY
    max_decay = math.log(HY_DECAY_TARGET) / HY_FAST_DECAY
    deltas = jnp.abs(jnp.linspace(min_decay, max_decay, W, dtype=F32)).reshape(W, 1)
    hf2, hb2 = _hyena_filters(emb_t, hy_w1, hy_b1, hy_w2, hy_b2, hy_w3, hy_b3,
                              hy_w4, hy_b4, hy_freq, deltas, W)
    T = _blk(L // 2, HY_BLOCK)
    tables = _dft_tables(T)
    bin_w = jnp.full((1, T), 1.0 / T, F32).at[0, 0].set(0.5 / T)
    h_re, h_im, h_ny = _lag_spectra(hf2, hb2, tables[0], tables[1], bin_w, L, T)

    skip_col = hy_skip.reshape(n_ord * W, 1)
    z, z_row0, z_rows = u_t, 0, (n_ord + 1) * W
    for o in range(n_ord):
        z = _hy_conv(z, z_row0, z_rows, u_t, (o + 1) * W, (n_ord + 1) * W, skip_col, tables,
                     h_re, h_im, h_ny, o * W, B, W, L, T)
        z_row0, z_rows = 0, W
    y_hy_t = z

    merged = _branch_merge(y_gla, y_hy_t, w_gla_o.astype(BF16), w_hy_o.astype(BF16), gates, B, L, D)
    h1, h1b = _proj_ln(merged, w_out.astype(BF16), x2d, ln1_g, ln1_b, alpha)
    act = _matmul_f32w(h1b, w_ff1, w_ff1.shape[1], out_dtype=BF16, act="relu2", name="ff1_relu2")
    out = _ff2_ln(act, w_ff2.astype(BF16), h1, ln2_g, ln2_b, alpha)
    return out.reshape(B, L, D)


def kernel(x, w_in, gla_wa2_f, gla_ba_f, gla_wa2_b, gla_ba_b, gla_norm_g, w_gla_o, hy_conv_w, hy_conv_b, hy_w1, hy_b1, hy_w2, hy_b2, hy_w3, hy_b3, hy_w4, hy_b4, hy_freq, hy_skip, w_hy_o, w_out, ln1_g, ln1_b, w_ff1, w_ff2, ln2_g, ln2_b):
    depth = w_in.shape[0]
    alpha = (2 * depth) ** 0.25
    params = (w_in, gla_wa2_f, gla_ba_f, gla_wa2_b, gla_ba_b, gla_norm_g, w_gla_o, hy_conv_w, hy_conv_b,
              hy_w1, hy_b1, hy_w2, hy_b2, hy_w3, hy_b3, hy_w4, hy_b4, hy_freq, hy_skip, w_hy_o, w_out,
              ln1_g, ln1_b, w_ff1, w_ff2, ln2_g, ln2_b)
    h = x
    for l in range(depth):
        h = _layer(h, *(p[l] for p in params), alpha)
    return h
```

```python
import functools
import math

import jax
import jax.numpy as jnp
import numpy as np
from jax import lax
from jax.experimental import pallas as pl
from jax.experimental.pallas import tpu as pltpu

F32 = jnp.float32
BF16 = jnp.bfloat16

GLA_HEADS = 4
GLA_TAU = 16.0
GLA_CHUNK = 64
GLA_BLOCK = 256
HY_BLOCK = 512
HY_FAST_DECAY = 0.3
HY_SLOW_DECAY = 1.5
HY_DECAY_TARGET = 1e-2
LN_EPS = 1e-5

V7X_VMEM_LIMIT_BYTES = 56 * 1024 * 1024
V7X_LANES = 128
V7X_SUBLANES = 8

NN = (((1,), (0,)), ((), ()))
NT = (((1,), (1,)), ((), ()))
TN = (((0,), (0,)), ((), ()))


def _dot(a, b, dims=NN):
    return lax.dot_general(a.astype(BF16), b.astype(BF16), dims, preferred_element_type=F32)


def _split2(a):
    h1 = a.astype(BF16)
    h2 = (a - h1.astype(F32)).astype(BF16)
    return h1, h2


def _dot_f32(a, b, dims=NN):
    a1, a2 = _split2(a)
    b1, b2 = _split2(b)
    d = lambda x, y: lax.dot_general(x, y, dims, preferred_element_type=F32)
    return (d(a2, b1) + d(a1, b2)) + d(a1, b1)


def _dot_exact_lhs(a_bf16, b, dims=NN):
    b1, b2 = _split2(b)
    d = lambda y: lax.dot_general(a_bf16, y, dims, preferred_element_type=F32)
    return d(b2) + d(b1)


def _call(body, *, grid, in_specs, out_specs, out_shape, name, scratch_shapes=()):
    return pl.pallas_call(
        body,
        grid=grid,
        in_specs=in_specs,
        out_specs=out_specs,
        out_shape=out_shape,
        scratch_shapes=scratch_shapes,
        compiler_params=pltpu.CompilerParams(
            dimension_semantics=("arbitrary",) * len(grid),
            vmem_limit_bytes=V7X_VMEM_LIMIT_BYTES,
        ),
        name=name,
    )


def _blk(n, want):
    b = min(n, want)
    while n % b:
        b //= 2
    return b


def _layernorm_rows(y, g, b):
    mu = jnp.mean(y, axis=-1, keepdims=True)
    d = y - mu
    var = jnp.mean(d * d, axis=-1, keepdims=True)
    return d * lax.rsqrt(var + LN_EPS) * g + b


def _sigmoid(x):
    return 0.5 * jnp.tanh(0.5 * x) + 0.5


def _act(acc, act):
    if act == "sigmoid":
        return _sigmoid(acc)
    if act == "relu2":
        return jnp.square(jnp.maximum(acc, 0.0))
    return acc


def _weight_spec(K, bn, transposed, index, row0=0):
    if transposed and row0:
        assert row0 % V7X_SUBLANES == 0
        return pl.BlockSpec((pl.Element(bn), pl.Element(K)),
                            lambda *g: (pl.multiple_of(row0 + index(*g) * bn, V7X_SUBLANES), 0))
    if transposed:
        return pl.BlockSpec((bn, K), lambda *g: (index(*g), 0))
    assert row0 == 0
    return pl.BlockSpec((K, bn), lambda *g: (0, index(*g)))


def _mm_f32w_body(a_ref, w_ref, o_ref, wb_ref, *, act, dims):
    @pl.when(pl.program_id(1) == 0)
    def _():
        wb_ref[...] = w_ref[...].astype(wb_ref.dtype)

    o_ref[...] = _act(_dot(a_ref[...], wb_ref[...], dims), act).astype(o_ref.dtype)


def _matmul_f32w(a, w, n_out, *, out_dtype, act=None, name, bm=1024, bn=1024, transposed=False, row0=0):
    M, K = a.shape
    bm, bn = _blk(M, bm), _blk(n_out, bn)
    return _call(
        functools.partial(_mm_f32w_body, act=act, dims=NT if transposed else NN),
        grid=(n_out // bn, M // bm),
        in_specs=[pl.BlockSpec((bm, K), lambda j, i: (i, 0)),
                  _weight_spec(K, bn, transposed, lambda j, i: j, row0)],
        out_specs=pl.BlockSpec((bm, bn), lambda j, i: (i, j)),
        out_shape=jax.ShapeDtypeStruct((M, n_out), out_dtype),
        scratch_shapes=[pltpu.VMEM((bn, K) if transposed else (K, bn), BF16)],
        name=name,
    )(a, w)


def _hy_proj_body(w_ref, x_ref, cp_ref, o_ref):
    u = _dot(w_ref[...], x_ref[...], NT)
    L = u.shape[1]
    t = lax.broadcasted_iota(jnp.int32, u.shape, 1)
    prev = jnp.where(t == 0, 0.0, pltpu.roll(u, 1, axis=1))
    nxt = jnp.where(t == L - 1, 0.0, pltpu.roll(u, L - 1, axis=1))
    cp = cp_ref[...]
    o_ref[...] = (cp[:, 0:1] * prev + cp[:, 1:2] * u + cp[:, 2:3] * nxt + cp[:, 3:4]).astype(o_ref.dtype)


def _hy_proj(w_t, xb, conv_params, B, L):
    C3, D = w_t.shape
    bc = _blk(C3, 1024)
    nc = C3 // bc
    return _call(
        _hy_proj_body,
        grid=(B, nc),
        in_specs=[pl.BlockSpec((bc, D), lambda b, c: (c, 0)),
                  pl.BlockSpec((L, D), lambda b, c: (b, 0)),
                  pl.BlockSpec((bc, 4), lambda b, c: (c, 0))],
        out_specs=pl.BlockSpec((bc, L), lambda b, c: (b * nc + c, 0)),
        out_shape=jax.ShapeDtypeStruct((B * C3, L), BF16),
        name="hy_proj_conv",
    )(w_t, xb, conv_params)


def _log_sigmoid(z):
    return -(jnp.maximum(-z, 0.0) + jnp.log(1.0 + jnp.exp(-jnp.abs(z))))


def _decay_body(x_ref, wab_ref, w2f_ref, bf_ref, w2b_ref, bb_ref, cf_ref, cb_ref, xb_ref, *, block):
    xb = x_ref[...].astype(xb_ref.dtype)
    xb_ref[...] = xb
    ab = _dot(xb, wab_ref[...], NT)
    zf = _dot(ab, w2f_ref[...]) + bf_ref[...]
    zb = _dot(ab, w2b_ref[...]) + bb_ref[...]
    laf = _log_sigmoid(zf) * (1.0 / GLA_TAU)
    lab = _log_sigmoid(zb) * (1.0 / GLA_TAU)
    T = block
    row = lax.broadcasted_iota(jnp.int32, (T, T), 0)
    col = lax.broadcasted_iota(jnp.int32, (T, T), 1)
    tri_f = (col <= row).astype(BF16)
    tri_b = (col >= row).astype(BF16)
    for s in range(xb.shape[0] // T):
        cf_ref[s * T:(s + 1) * T, :] = _dot_exact_lhs(tri_f, laf[s * T:(s + 1) * T, :])
        cb_ref[s * T:(s + 1) * T, :] = _dot_exact_lhs(tri_b, lab[s * T:(s + 1) * T, :])


def _gla_decays(x2d, w_in_t, ab_col, wa2_f, ba_f, wa2_b, ba_b, block):
    M, D = x2d.shape
    rank, KEY = wa2_f.shape
    P = V7X_LANES
    assert 2 * rank <= P and ab_col % P == 0
    w2f = jnp.pad(wa2_f, ((0, P - rank), (0, 0)))
    w2b = jnp.pad(wa2_b, ((rank, P - 2 * rank), (0, 0)))
    bm = max(_blk(M, 512), block)
    full = lambda shape: pl.BlockSpec(shape, lambda i: (0, 0))
    out = jax.ShapeDtypeStruct((M, KEY), F32)
    return _call(
        functools.partial(_decay_body, block=block),
        grid=(M // bm,),
        in_specs=[pl.BlockSpec((bm, D), lambda i: (i, 0)), pl.BlockSpec((P, D), lambda i: (ab_col // P, 0)),
                  full((P, KEY)), full((1, KEY)), full((P, KEY)), full((1, KEY))],
        out_specs=[pl.BlockSpec((bm, KEY), lambda i: (i, 0))] * 2 + [pl.BlockSpec((bm, D), lambda i: (i, 0))],
        out_shape=[out, out, jax.ShapeDtypeStruct((M, D), BF16)],
        name="gla_decays",
    )(x2d, w_in_t, w2f, ba_f.reshape(1, KEY), w2b, ba_b.reshape(1, KEY))


def _gla_body(q_ref, k_ref, v_ref, r_ref, cf_ref, cb_ref, g_ref, y_ref, o_acc,
              stf_ref, sf_ref, qbf_ref, kef_ref, klf_ref,
              stb_ref, sb_ref, qbb_ref, keb_ref, klb_ref, *, block, sub):
    L, DK = q_ref.shape
    T, C = block, sub
    n, ns = L // T, T // C
    scale = DK ** -0.5
    row = lax.broadcasted_iota(jnp.int32, (C, C), 0)
    col = lax.broadcasted_iota(jnp.int32, (C, C), 1)

    def boundary_row(cum_ref, start, pick):
        return cum_ref[pl.ds(pl.multiple_of(start, 8), 8), :][pick:pick + 1, :]

    def block_step(blk, cum_ref, st_ref, s_ref, qb_ref, ke_ref, kl_ref, forward, first_visit):
        r0 = pl.multiple_of(blk * T, T)
        tot = boundary_row(cum_ref, r0 + T - 8, 7) if forward else boundary_row(cum_ref, r0, 0)
        mask = (col <= row) if forward else (col > row)
        refs = {}
        for I in (range(ns) if forward else range(ns - 1, -1, -1)):
            sl = slice(I * C, (I + 1) * C)
            rows = pl.ds(r0 + I * C, C)
            if forward:
                ref = boundary_row(cum_ref, r0 + I * C - 8, 7) if I > 0 else jnp.zeros_like(tot)
            else:
                ref = boundary_row(cum_ref, r0 + (I + 1) * C, 0) if I < ns - 1 else jnp.zeros_like(tot)
            d = cum_ref[rows, :] - ref
            q_loc = q_ref[rows, :] * (scale * jnp.exp(d))
            k_loc = k_ref[rows, :] * jnp.exp(-d)
            qb_ref[sl, :] = (q_loc * jnp.exp(ref)).astype(BF16)
            ke_ref[sl, :] = (k_loc * jnp.exp(tot - ref)).astype(BF16)
            k_loc = k_loc.astype(BF16)
            kl_ref[sl, :] = k_loc
            s_ref[sl, sl] = jnp.where(mask, _dot(q_loc, k_loc, NT), 0.0).astype(BF16)
            for J, ref_j in refs.items():
                sj = slice(J * C, (J + 1) * C)
                s_ref[sl, sj] = _dot(q_loc * jnp.exp(ref - ref_j), kl_ref[sj, :], NT).astype(BF16)
            refs[I] = ref
        rows = pl.ds(r0, T)
        vb = v_ref[rows, :].astype(BF16)
        st = st_ref[...]
        o = _dot(s_ref[...], vb) + _dot(qb_ref[...], st, NT)
        st_ref[...] = st * jnp.exp(tot) + _dot(vb, ke_ref[...], TN)
        if first_visit:
            o_acc[rows, :] = o
        else:
            o = o_acc[rows, :] + o
            o = o * lax.rsqrt(jnp.mean(o * o, axis=-1, keepdims=True) + LN_EPS) * g_ref[...]
            r = r_ref[rows, :].astype(F32)
            y_ref[rows, :] = (o * (r * _sigmoid(r))).astype(y_ref.dtype)

    fwd = functools.partial(block_step, cum_ref=cf_ref, st_ref=stf_ref, s_ref=sf_ref, qb_ref=qbf_ref,
                            ke_ref=kef_ref, kl_ref=klf_ref, forward=True)
    bwd = functools.partial(block_step, cum_ref=cb_ref, st_ref=stb_ref, s_ref=sb_ref, qb_ref=qbb_ref,
                            ke_ref=keb_ref, kl_ref=klb_ref, forward=False)

    for ref in (stf_ref, stb_ref, sf_ref, sb_ref):
        ref[...] = jnp.zeros_like(ref)

    def sweep(lo, hi, step, unroll=1):
        def body(i, carry):
            step(i)
            return carry
        lax.fori_loop(lo, hi, body, 0, unroll=unroll)

    if n % 2 == 0:
        def first_half(i):
            fwd(i, first_visit=True)
            bwd(n - 1 - i, first_visit=True)

        def second_half(i):
            fwd(i, first_visit=False)
            bwd(n - 1 - i, first_visit=False)

        sweep(0, n // 2, first_half, unroll=True)
        sweep(n // 2, n, second_half, unroll=True)
    else:
        sweep(0, n, lambda i: fwd(i, first_visit=True))
        sweep(0, n, lambda i: bwd(n - 1 - i, first_visit=False))


def _gla(p_qkvr, cum_f, cum_b, norm_g, B, L, KEY, VAL, block):
    H = GLA_HEADS
    DK, DV = KEY // H, VAL // H
    M = B * L
    kq = KEY // DK
    vq = 2 * KEY // DV
    rq = (2 * KEY + VAL) // DV
    T = block
    direction_scratch = [pltpu.VMEM((DV, DK), F32), pltpu.VMEM((T, T), BF16), pltpu.VMEM((T, DK), BF16),
                         pltpu.VMEM((T, DK), BF16), pltpu.VMEM((T, DK), BF16)]
    return _call(
        functools.partial(_gla_body, block=T, sub=_blk(T, GLA_CHUNK)),
        grid=(B, H),
        in_specs=[pl.BlockSpec((L, DK), lambda b, h: (b, h)),
                  pl.BlockSpec((L, DK), lambda b, h: (b, kq + h)),
                  pl.BlockSpec((L, DV), lambda b, h: (b, vq + h)),
                  pl.BlockSpec((L, DV), lambda b, h: (b, rq + h)),
                  pl.BlockSpec((L, DK), lambda b, h: (b, h)),
                  pl.BlockSpec((L, DK), lambda b, h: (b, h)),
                  pl.BlockSpec((1, DV), lambda b, h: (0, h))],
        out_specs=pl.BlockSpec((L, DV), lambda b, h: (b, h)),
        out_shape=jax.ShapeDtypeStruct((M, VAL), BF16),
        scratch_shapes=[pltpu.VMEM((L, DV), F32)] + direction_scratch * 2,
        name="gla_bidir",
    )(p_qkvr, p_qkvr, p_qkvr, p_qkvr, cum_f, cum_b, norm_g.reshape(1, VAL))


def _filter_body(emb_ref, w1_ref, b1_ref, w2_ref, b2_ref, w3_ref, b3_ref, fr_ref,
                 w4f_ref, b4f_ref, w4b_ref, b4b_ref, dl_ref, hf_ref, hb_ref, h_ref):
    @pl.when((pl.program_id(0) == 0) & (pl.program_id(1) == 0))
    def _():
        fr = fr_ref[...]
        h = jnp.sin(fr * (_dot_f32(w1_ref[...], emb_ref[...]) + b1_ref[...]))
        h = jnp.sin(fr * (_dot_f32(w2_ref[...], h) + b2_ref[...]))
        h_ref[...] = jnp.sin(fr * (_dot_f32(w3_ref[...], h) + b3_ref[...]))

    h = h_ref[...]
    t_lin = emb_ref[0:1, :]
    decay = jnp.exp(-t_lin * dl_ref[...])
    hf_ref[...] = (_dot_f32(w4f_ref[...], h) + b4f_ref[...]) * decay
    hb_ref[...] = (_dot_f32(w4b_ref[...], h) + b4b_ref[...]) * decay


def _hyena_filters(emb_t, w1, b1, w2, b2, w3, b3, w4, b4, freq, deltas, W):
    L = emb_t.shape[1]
    HID = w1.shape[1]
    n_ord = w4.shape[1] // (2 * W)
    EMB = -(-emb_t.shape[0] // V7X_LANES) * V7X_LANES
    w1 = jnp.pad(w1, ((0, EMB - w1.shape[0]), (0, 0)))
    emb_t = jnp.pad(emb_t, ((0, EMB - emb_t.shape[0]), (0, 0)))
    cb = _blk(W, 512)
    ncb = W // cb
    w4t = w4.T
    b4c = b4.reshape(-1, 1)
    colv = lambda v: v.reshape(-1, 1)
    full = lambda shape: pl.BlockSpec(shape, lambda o, c: (0, 0))
    rows = n_ord * W
    return _call(
        _filter_body,
        grid=(n_ord, ncb),
        in_specs=[full((EMB, L)), full((HID, EMB)), full((HID, 1)), full((HID, HID)), full((HID, 1)),
                  full((HID, HID)), full((HID, 1)), full((HID, 1)),
                  pl.BlockSpec((cb, HID), lambda o, c: (o * 2 * ncb + c, 0)),
                  pl.BlockSpec((cb, 1), lambda o, c: (o * 2 * ncb + c, 0)),
                  pl.BlockSpec((cb, HID), lambda o, c: (o * 2 * ncb + ncb + c, 0)),
                  pl.BlockSpec((cb, 1), lambda o, c: (o * 2 * ncb + ncb + c, 0)),
                  pl.BlockSpec((cb, 1), lambda o, c: (c, 0))],
        out_specs=[pl.BlockSpec((cb, L), lambda o, c: (o * ncb + c, 0))] * 2,
        out_shape=[jax.ShapeDtypeStruct((rows, L), F32)] * 2,
        scratch_shapes=[pltpu.VMEM((HID, L), F32)],
        name="hy_filters",
    )(emb_t, w1.T, colv(b1), w2.T, colv(b2), w3.T, colv(b3), colv(freq), w4t, b4c, w4t, b4c, deltas)


def _lag_spectrum_body(hf_ref, hb_ref, c_ref, s_ref, w_ref, hr_ref, hi_ref, hn_ref, *, L, T):
    nb = L // T
    rows = hf_ref.shape[0]
    lane = lax.broadcasted_iota(jnp.int32, (rows, T), 1)
    alt = jnp.where((lane & 1) == 0, 1.0, -1.0)
    w = w_ref[...]

    def transforms(x_ref):
        blocks = [x_ref[:, n * T:(n + 1) * T] for n in range(nb)]
        return ([_dot(x, c_ref[...]) for x in blocks], [_dot(x, s_ref[...]) for x in blocks],
                [x[:, 0:1] for x in blocks])

    fc, fs, f0 = transforms(hf_ref)
    bc, bs, b0 = transforms(hb_ref)
    ny = lambda s: s[:, 0:1]
    for d in range(-(nb - 1), nb):
        if d >= 1:
            hr = fc[d] + alt * (fc[d - 1] - f0[d - 1])
            hi = fs[d] + alt * fs[d - 1]
            hn = ny(fs[d]) + ny(fs[d - 1]) - f0[d - 1]
        elif d == 0:
            hr = fc[0] + bc[0] - b0[0]
            hi = fs[0] - bs[0]
            hn = ny(fs[0]) + ny(bs[0]) - b0[0]
        else:
            e = -d
            hr = bc[e] + alt * (bc[e - 1] - b0[e - 1])
            hi = -(bs[e] + alt * bs[e - 1])
            hn = ny(bs[e]) + ny(bs[e - 1]) - b0[e - 1]
        li = d + nb - 1
        hr_ref[li] = (hr * w).astype(hr_ref.dtype)
        hi_ref[li] = jnp.where(lane == 0, 0.0, hi * w).astype(hi_ref.dtype)
        hn_ref[li] = hn * w[:, 0:1]


def _lag_spectra(hf2, hb2, cmat, smat, bin_w, L, T):
    R = hf2.shape[0]
    assert T % 2 == 0 and L % T == 0
    nl = 2 * (L // T) - 1
    rb = _blk(R, 256)
    full = lambda shape: pl.BlockSpec(shape, lambda r: (0, 0))
    return _call(
        functools.partial(_lag_spectrum_body, L=L, T=T),
        grid=(R // rb,),
        in_specs=[pl.BlockSpec((rb, L), lambda r: (r, 0)), pl.BlockSpec((rb, L), lambda r: (r, 0)),
                  full((T, T)), full((T, T)), full((1, T))],
        out_specs=[pl.BlockSpec((nl, rb, T), lambda r: (0, r, 0)),
                   pl.BlockSpec((nl, rb, T), lambda r: (0, r, 0)),
                   pl.BlockSpec((nl, rb, 1), lambda r: (0, r, 0))],
        out_shape=[jax.ShapeDtypeStruct((nl, R, T), BF16), jax.ShapeDtypeStruct((nl, R, T), BF16),
                   jax.ShapeDtypeStruct((nl, R, 1), F32)],
        name="hy_lag_spectra",
    )(hf2, hb2, cmat, smat, bin_w)


def _hy_conv_body(z_ref, gate_ref, skip_ref, c_ref, s_ref, st_ref, hr_ref, hi_ref, hn_ref, o_ref,
                  zr_ref, zi_ref, yr_ref, yi_ref, *, T, chunk):
    R, L = z_ref.shape
    nb = L // T
    for j in range(nb):
        zj = z_ref[:, j * T:(j + 1) * T]
        zr_ref[j] = _dot(zj, c_ref[...]).astype(zr_ref.dtype)
        zi_ref[j] = _dot(zj, s_ref[...]).astype(zi_ref.dtype)

    bin0 = lax.broadcasted_iota(jnp.int32, (chunk, T), 1) == 0

    def combine(r, carry):
        rows = pl.ds(pl.multiple_of(r * chunk, chunk), chunk)
        zr = [zr_ref[j, rows, :] for j in range(nb)]
        zi = [zi_ref[j, rows, :] for j in range(nb)]
        zn = [z[:, 0:1].astype(F32) for z in zi]
        for i in range(nb):
            yr = yi = yn = None
            for j in range(nb):
                lag = i - j + nb - 1
                hr, hi = hr_ref[lag, rows, :], hi_ref[lag, rows, :]
                tr = zr[j] * hr - zi[j] * hi
                ti = zr[j] * hi + zi[j] * hr
                tn = zn[j] * hn_ref[lag, rows, :]
                yr, yi, yn = (tr, ti, tn) if j == 0 else (yr + tr, yi + ti, yn + tn)
            yr_ref[i, rows, :] = yr
            yi_ref[i, rows, :] = jnp.where(bin0, yn.astype(yi.dtype), yi)
        return carry

    lax.fori_loop(0, R // chunk, combine, 0)

    skip = skip_ref[...]
    for i in range(nb):
        sl = slice(i * T, (i + 1) * T)
        zc = _dot(yr_ref[i], c_ref[...]) + _dot(yi_ref[i], st_ref[...])
        o_ref[:, sl] = (gate_ref[:, sl] * (zc + skip * z_ref[:, sl])).astype(o_ref.dtype)


def _hy_conv(z2d, z_row0, z_rows_per_b, gates2d, g_row0, g_rows_per_b, skip_col, tables,
             hr, hi, hn, h_row0, B, W, L, T):
    R = _blk(W, 512)
    ncb = W // R
    nb, nl = L // T, hr.shape[0]
    zb, z0 = z_rows_per_b // R, z_row0 // R
    gb, g0 = g_rows_per_b // R, g_row0 // R
    h0 = h_row0 // R
    table = pl.BlockSpec((T, T), lambda c, b: (0, 0))
    spectrum = lambda last: pl.BlockSpec((nl, R, last), lambda c, b: (0, h0 + c, 0),
                                         pipeline_mode=pl.Buffered(1))
    return _call(
        functools.partial(_hy_conv_body, T=T, chunk=16),
        grid=(ncb, B),
        in_specs=[pl.BlockSpec((R, L), lambda c, b: (b * zb + z0 + c, 0)),
                  pl.BlockSpec((R, L), lambda c, b: (b * gb + g0 + c, 0)),
                  pl.BlockSpec((R, 1), lambda c, b: (h0 + c, 0)),
                  table, table, table, spectrum(T), spectrum(T), spectrum(1)],
        out_specs=pl.BlockSpec((R, L), lambda c, b: (b * ncb + c, 0)),
        out_shape=jax.ShapeDtypeStruct((B * W, L), BF16),
        scratch_shapes=[pltpu.VMEM((nb, R, T), BF16)] * 4,
        name="hy_conv",
    )(z2d, gates2d, skip_col, *tables, hr, hi, hn)


def _merge_body(yg_ref, yh_ref, wg_ref, wh_ref, g0_ref, g1_ref, o_ref):
    tg = _dot(yg_ref[...], wg_ref[...])
    th = _dot(yh_ref[...], wh_ref[...], TN)
    o_ref[...] = (g0_ref[...] * tg + g1_ref[...] * th).astype(o_ref.dtype)


def _branch_merge(y_gla, y_hy_t, w_gla_o, w_hy_o, gates, B, L, D):
    VAL, W = y_gla.shape[1], w_hy_o.shape[0]
    bt = _blk(L, 512)
    nt = L // bt
    resident = lambda shape: pl.BlockSpec(shape, lambda b, t: (0, 0), pipeline_mode=pl.Buffered(1))
    return _call(
        _merge_body,
        grid=(B, nt),
        in_specs=[pl.BlockSpec((bt, VAL), lambda b, t: (b * nt + t, 0)),
                  pl.BlockSpec((W, bt), lambda b, t: (b, t)),
                  resident((VAL, D)), resident((W, D)),
                  pl.BlockSpec((bt, D), lambda b, t: (b * nt + t, 0)),
                  pl.BlockSpec((bt, D), lambda b, t: (b * nt + t, 1))],
        out_specs=pl.BlockSpec((bt, D), lambda b, t: (b * nt + t, 0)),
        out_shape=jax.ShapeDtypeStruct((B * L, D), BF16),
        name="branch_merge",
    )(y_gla, y_hy_t, w_gla_o, w_hy_o, gates, gates)


def _proj_ln_body(a_ref, w_ref, x_ref, g_ref, b_ref, o_ref, ob_ref, *, alpha):
    y = alpha * x_ref[...] + _dot(a_ref[...], w_ref[...])
    h = _layernorm_rows(y, g_ref[...], b_ref[...])
    o_ref[...] = h
    ob_ref[...] = h.astype(ob_ref.dtype)


def _proj_ln(a, w, x2d, ln_g, ln_b, alpha):
    M, K = a.shape
    D = w.shape[1]
    bm = _blk(M, 512)
    row = lambda i: (i, 0)
    full = lambda shape: pl.BlockSpec(shape, lambda i: (0, 0))
    return _call(
        functools.partial(_proj_ln_body, alpha=alpha),
        grid=(M // bm,),
        in_specs=[pl.BlockSpec((bm, K), row), full((K, D)), pl.BlockSpec((bm, D), row),
                  full((1, D)), full((1, D))],
        out_specs=[pl.BlockSpec((bm, D), row)] * 2,
        out_shape=[jax.ShapeDtypeStruct((M, D), F32), jax.ShapeDtypeStruct((M, D), BF16)],
        name="out_proj_ln1",
    )(a, w, x2d, ln_g.reshape(1, D), ln_b.reshape(1, D))


def _ff2_ln_body(a_ref, w_ref, h_ref, g_ref, b_ref, o_ref, *, alpha, ln_rows):
    kk = pl.program_id(1)

    @pl.when(kk == 0)
    def _():
        o_ref[...] = jnp.zeros_like(o_ref)

    o_ref[...] += _dot(a_ref[...], w_ref[...])

    @pl.when(kk == pl.num_programs(1) - 1)
    def _():
        g, b = g_ref[...], b_ref[...]

        def norm_rows(r, carry):
            rows = pl.ds(pl.multiple_of(r * ln_rows, ln_rows), ln_rows)
            y = alpha * h_ref[rows, :] + o_ref[rows, :]
            o_ref[rows, :] = _layernorm_rows(y, g, b)
            return carry

        lax.fori_loop(0, o_ref.shape[0] // ln_rows, norm_rows, 0)


def _ff2_ln(a, w, h, ln_g, ln_b, alpha):
    M, K = a.shape
    D = w.shape[1]
    bm, bk = _blk(M, 1024), _blk(K, 1024)
    return _call(
        functools.partial(_ff2_ln_body, alpha=alpha, ln_rows=_blk(bm, 128)),
        grid=(M // bm, K // bk),
        in_specs=[pl.BlockSpec((bm, bk), lambda i, k: (i, k)),
                  pl.BlockSpec((bk, D), lambda i, k: (k, 0)),
                  pl.BlockSpec((bm, D), lambda i, k: (i, 0)),
                  pl.BlockSpec((1, D), lambda i, k: (0, 0)),
                  pl.BlockSpec((1, D), lambda i, k: (0, 0))],
        out_specs=pl.BlockSpec((bm, D), lambda i, k: (i, 0)),
        out_shape=jax.ShapeDtypeStruct((M, D), F32),
        name="ff2_ln2",
    )(a, w, h, ln_g.reshape(1, D), ln_b.reshape(1, D))


def _dft_tables(L):
    LO = _blk(L, 32)
    k = jnp.arange(L, dtype=jnp.int32)
    ang = lambda m: ((m[:, None] * k[None, :]) % (2 * L)).astype(F32) * (math.pi / L)
    a_hi = ang(jnp.arange(L // LO, dtype=jnp.int32) * LO)[:, None, :]
    a_lo = ang(jnp.arange(LO, dtype=jnp.int32))[None, :, :]
    cos_t = (jnp.cos(a_hi) * jnp.cos(a_lo) - jnp.sin(a_hi) * jnp.sin(a_lo)).reshape(L, L)
    sin_t = (jnp.sin(a_hi) * jnp.cos(a_lo) + jnp.cos(a_hi) * jnp.sin(a_lo)).reshape(L, L)
    nyq = jnp.where(k % 2 == 0, 1.0, -1.0).astype(F32)
    smat = jnp.where(k[None, :] == 0, nyq[:, None], sin_t)
    smat_t = jnp.where(k[:, None] == 0, nyq[None, :], sin_t)
    return cos_t.astype(BF16), smat.astype(BF16), smat_t.astype(BF16)


def _position_features(L, emb_dim):
    t = jnp.linspace(0.0, 1.0, L, dtype=F32)[:, None]
    bands = (emb_dim - 1) // 2
    f = jnp.linspace(1e-4, bands - 1, bands, dtype=F32)
    wpos = 2.0 * math.pi * jnp.arange(L, dtype=F32) / L
    ang = wpos[:, None] * f[None, :]
    return jnp.concatenate([t, jnp.cos(ang), -jnp.sin(ang)], axis=-1).T


def _layer(h, w_in, gla_wa2_f, gla_ba_f, gla_wa2_b, gla_ba_b, gla_norm_g, w_gla_o,
           hy_conv_w, hy_conv_b, hy_w1, hy_b1, hy_w2, hy_b2, hy_w3, hy_b3, hy_w4, hy_b4,
           hy_freq, hy_skip, w_hy_o, w_out, ln1_g, ln1_b, w_ff1, w_ff2, ln2_g, ln2_b, alpha):
    B, L, D = h.shape
    M = B * L
    rank, KEY = gla_wa2_f.shape
    VAL = gla_norm_g.shape[0]
    n_ord, W = hy_skip.shape

    sizes = (KEY, KEY, VAL, VAL, rank, rank, (n_ord + 1) * W, 2 * D)
    offs = [int(v) for v in np.concatenate([[0], np.cumsum(sizes)])]
    w_in_t = w_in.T
    w_hy_t = w_in_t[offs[6]:offs[7]].astype(BF16)

    x2d = h.reshape(M, D)
    gla_block = _blk(L, GLA_BLOCK)
    cum_f, cum_b, xb = _gla_decays(x2d, w_in_t, offs[4], gla_wa2_f, gla_ba_f, gla_wa2_b, gla_ba_b, gla_block)

    p_qkvr = _matmul_f32w(xb, w_in_t, offs[4], out_dtype=BF16, name="in_proj_qkvr", bm=2048, transposed=True)
    gates = _matmul_f32w(xb, w_in_t, sizes[7], out_dtype=BF16, act="sigmoid", name="in_proj_gates", bm=2048,
                         transposed=True, row0=offs[7])
    conv_params = jnp.concatenate([hy_conv_w.T, hy_conv_b[:, None]], axis=1)
    u_t = _hy_proj(w_hy_t, xb, conv_params, B, L)

    y_gla = _gla(p_qkvr, cum_f, cum_b, gla_norm_g, B, L, KEY, VAL, gla_block)

    emb_t = _position_features(L, hy_w1.shape[0])
    min_decay = math.log(HY_DECAY_TARGET) / HY_SLOW_DECAY
    max_decay = math.log(HY_DECAY_TARGET) / HY_FAST_DECAY
    deltas = jnp.abs(jnp.linspace(min_decay, max_decay, W, dtype=F32)).reshape(W, 1)
    hf2, hb2 = _hyena_filters(emb_t, hy_w1, hy_b1, hy_w2, hy_b2, hy_w3, hy_b3,
                              hy_w4, hy_b4, hy_freq, deltas, W)
    T = _blk(L // 2, HY_BLOCK)
    tables = _dft_tables(T)
    bin_w = jnp.full((1, T), 1.0 / T, F32).at[0, 0].set(0.5 / T)
    h_re, h_im, h_ny = _lag_spectra(hf2, hb2, tables[0], tables[1], bin_w, L, T)

    skip_col = hy_skip.reshape(n_ord * W, 1)
    z, z_row0, z_rows = u_t, 0, (n_ord + 1) * W
    for o in range(n_ord):
        z = _hy_conv(z, z_row0, z_rows, u_t, (o + 1) * W, (n_ord + 1) * W, skip_col, tables,
                     h_re, h_im, h_ny, o * W, B, W, L, T)
        z_row0, z_rows = 0, W
    y_hy_t = z

    merged = _branch_merge(y_gla, y_hy_t, w_gla_o.astype(BF16), w_hy_o.astype(BF16), gates, B, L, D)
    h1, h1b = _proj_ln(merged, w_out.astype(BF16), x2d, ln1_g, ln1_b, alpha)
    act = _matmul_f32w(h1b, w_ff1, w_ff1.shape[1], out_dtype=BF16, act="relu2", name="ff1_relu2")
    out = _ff2_ln(act, w_ff2.astype(BF16), h1, ln2_g, ln2_b, alpha)
    return out.reshape(B, L, D)


def kernel(x, w_in, gla_wa2_f, gla_ba_f, gla_wa2_b, gla_ba_b, gla_norm_g, w_gla_o, hy_conv_w, hy_conv_b, hy_w1, hy_b1, hy_w2, hy_b2, hy_w3, hy_b3, hy_w4, hy_b4, hy_freq, hy_skip, w_hy_o, w_out, ln1_g, ln1_b, w_ff1, w_ff2, ln2_g, ln2_b):
    depth = w_in.shape[0]
    alpha = (2 * depth) ** 0.25
    params = (w_in, gla_wa2_f, gla_ba_f, gla_wa2_b, gla_ba_b, gla_norm_g, w_gla_o, hy_conv_w, hy_conv_b,
              hy_w1, hy_b1, hy_w2, hy_b2, hy_w3, hy_b3, hy_w4, hy_b4, hy_freq, hy_skip, w_hy_o, w_out,
              ln1_g, ln1_b, w_ff1, w_ff2, ln2_g, ln2_b)
    h = x
    for l in range(depth):
        h = _layer(h, *(p[l] for p in params), alpha)
    return h
```

```python
import functools
import math

import jax
import jax.numpy as jnp
import numpy as np
from jax import lax
from jax.experimental import pallas as pl
from jax.experimental.pallas import tpu as pltpu

F32 = jnp.float32
BF16 = jnp.bfloat16

GLA_HEADS = 4
GLA_TAU = 16.0
GLA_CHUNK = 64
GLA_BLOCK = 256
HY_BLOCK = 512
HY_FAST_DECAY = 0.3
HY_SLOW_DECAY = 1.5
HY_DECAY_TARGET = 1e-2
LN_EPS = 1e-5

V7X_VMEM_LIMIT_BYTES = 56 * 1024 * 1024
V7X_LANES = 128
V7X_SUBLANES = 8

NN = (((1,), (0,)), ((), ()))
NT = (((1,), (1,)), ((), ()))
TN = (((0,), (0,)), ((), ()))


def _dot(a, b, dims=NN):
    return lax.dot_general(a.astype(BF16), b.astype(BF16), dims, preferred_element_type=F32)


def _split2(a):
    h1 = a.astype(BF16)
    h2 = (a - h1.astype(F32)).astype(BF16)
    return h1, h2


def _dot_f32(a, b, dims=NN):
    a1, a2 = _split2(a)
    b1, b2 = _split2(b)
    d = lambda x, y: lax.dot_general(x, y, dims, preferred_element_type=F32)
    return (d(a2, b1) + d(a1, b2)) + d(a1, b1)


def _dot_exact_lhs(a_bf16, b, dims=NN):
    b1, b2 = _split2(b)
    d = lambda y: lax.dot_general(a_bf16, y, dims, preferred_element_type=F32)
    return d(b2) + d(b1)


def _call(body, *, grid, in_specs, out_specs, out_shape, name, scratch_shapes=()):
    return pl.pallas_call(
        body,
        grid=grid,
        in_specs=in_specs,
        out_specs=out_specs,
        out_shape=out_shape,
        scratch_shapes=scratch_shapes,
        compiler_params=pltpu.CompilerParams(
            dimension_semantics=("arbitrary",) * len(grid),
            vmem_limit_bytes=V7X_VMEM_LIMIT_BYTES,
        ),
        name=name,
    )


def _blk(n, want):
    b = min(n, want)
    while n % b:
        b //= 2
    return b


def _layernorm_rows(y, g, b):
    mu = jnp.mean(y, axis=-1, keepdims=True)
    d = y - mu
    var = jnp.mean(d * d, axis=-1, keepdims=True)
    return d * lax.rsqrt(var + LN_EPS) * g + b


def _sigmoid(x):
    return 0.5 * jnp.tanh(0.5 * x) + 0.5


def _act(acc, act):
    if act == "sigmoid":
        return _sigmoid(acc)
    if act == "relu2":
        return jnp.square(jnp.maximum(acc, 0.0))
    return acc


def _weight_spec(K, bn, transposed, index, row0=0):
    if transposed and row0:
        assert row0 % V7X_SUBLANES == 0
        return pl.BlockSpec((pl.Element(bn), pl.Element(K)),
                            lambda *g: (pl.multiple_of(row0 + index(*g) * bn, V7X_SUBLANES), 0))
    if transposed:
        return pl.BlockSpec((bn, K), lambda *g: (index(*g), 0))
    assert row0 == 0
    return pl.BlockSpec((K, bn), lambda *g: (0, index(*g)))


def _mm_f32w_body(a_ref, w_ref, o_ref, wb_ref, *, act, dims):
    @pl.when(pl.program_id(1) == 0)
    def _():
        wb_ref[...] = w_ref[...].astype(wb_ref.dtype)

    o_ref[...] = _act(_dot(a_ref[...], wb_ref[...], dims), act).astype(o_ref.dtype)


def _matmul_f32w(a, w, n_out, *, out_dtype, act=None, name, bm=1024, bn=1024, transposed=False, row0=0):
    M, K = a.shape
    bm, bn = _blk(M, bm), _blk(n_out, bn)
    return _call(
        functools.partial(_mm_f32w_body, act=act, dims=NT if transposed else NN),
        grid=(n_out // bn, M // bm),
        in_specs=[pl.BlockSpec((bm, K), lambda j, i: (i, 0)),
                  _weight_spec(K, bn, transposed, lambda j, i: j, row0)],
        out_specs=pl.BlockSpec((bm, bn), lambda j, i: (i, j)),
        out_shape=jax.ShapeDtypeStruct((M, n_out), out_dtype),
        scratch_shapes=[pltpu.VMEM((bn, K) if transposed else (K, bn), BF16)],
        name=name,
    )(a, w)


def _hy_proj_body(w_ref, x_ref, cp_ref, o_ref):
    u = _dot(w_ref[...], x_ref[...], NT)
    L = u.shape[1]
    t = lax.broadcasted_iota(jnp.int32, u.shape, 1)
    prev = jnp.where(t == 0, 0.0, pltpu.roll(u, 1, axis=1))
    nxt = jnp.where(t == L - 1, 0.0, pltpu.roll(u, L - 1, axis=1))
    cp = cp_ref[...]
    o_ref[...] = (cp[:, 0:1] * prev + cp[:, 1:2] * u + cp[:, 2:3] * nxt + cp[:, 3:4]).astype(o_ref.dtype)


def _hy_proj(w_t, xb, conv_params, B, L):
    C3, D = w_t.shape
    bc = _blk(C3, 1024)
    nc = C3 // bc
    return _call(
        _hy_proj_body,
        grid=(B, nc),
        in_specs=[pl.BlockSpec((bc, D), lambda b, c: (c, 0)),
                  pl.BlockSpec((L, D), lambda b, c: (b, 0)),
                  pl.BlockSpec((bc, 4), lambda b, c: (c, 0))],
        out_specs=pl.BlockSpec((bc, L), lambda b, c: (b * nc + c, 0)),
        out_shape=jax.ShapeDtypeStruct((B * C3, L), BF16),
        name="hy_proj_conv",
    )(w_t, xb, conv_params)


def _log_sigmoid(z):
    return -(jnp.maximum(-z, 0.0) + jnp.log(1.0 + jnp.exp(-jnp.abs(z))))


def _decay_body(x_ref, wab_ref, w2f_ref, bf_ref, w2b_ref, bb_ref, cf_ref, cb_ref, xb_ref, *, block):
    xb = x_ref[...].astype(xb_ref.dtype)
    xb_ref[...] = xb
    ab = _dot(xb, wab_ref[...], NT)
    zf = _dot(ab, w2f_ref[...]) + bf_ref[...]
    zb = _dot(ab, w2b_ref[...]) + bb_ref[...]
    laf = _log_sigmoid(zf) * (1.0 / GLA_TAU)
    lab = _log_sigmoid(zb) * (1.0 / GLA_TAU)
    T = block
    row = lax.broadcasted_iota(jnp.int32, (T, T), 0)
    col = lax.broadcasted_iota(jnp.int32, (T, T), 1)
    tri_f = (col <= row).astype(BF16)
    tri_b = (col >= row).astype(BF16)
    for s in range(xb.shape[0] // T):
        cf_ref[s * T:(s + 1) * T, :] = _dot_exact_lhs(tri_f, laf[s * T:(s + 1) * T, :])
        cb_ref[s * T:(s + 1) * T, :] = _dot_exact_lhs(tri_b, lab[s * T:(s + 1) * T, :])


def _gla_decays(x2d, w_in_t, ab_col, wa2_f, ba_f, wa2_b, ba_b, block):
    M, D = x2d.shape
    rank, KEY = wa2_f.shape
    P = V7X_LANES
    assert 2 * rank <= P and ab_col % P == 0
    w2f = jnp.pad(wa2_f, ((0, P - rank), (0, 0)))
    w2b = jnp.pad(wa2_b, ((rank, P - 2 * rank), (0, 0)))
    bm = max(_blk(M, 512), block)
    full = lambda shape: pl.BlockSpec(shape, lambda i: (0, 0))
    out = jax.ShapeDtypeStruct((M, KEY), F32)
    return _call(
        functools.partial(_decay_body, block=block),
        grid=(M // bm,),
        in_specs=[pl.BlockSpec((bm, D), lambda i: (i, 0)), pl.BlockSpec((P, D), lambda i: (ab_col // P, 0)),
                  full((P, KEY)), full((1, KEY)), full((P, KEY)), full((1, KEY))],
        out_specs=[pl.BlockSpec((bm, KEY), lambda i: (i, 0))] * 2 + [pl.BlockSpec((bm, D), lambda i: (i, 0))],
        out_shape=[out, out, jax.ShapeDtypeStruct((M, D), BF16)],
        name="gla_decays",
    )(x2d, w_in_t, w2f, ba_f.reshape(1, KEY), w2b, ba_b.reshape(1, KEY))


def _gla_body(q_ref, k_ref, v_ref, r_ref, cf_ref, cb_ref, g_ref, y_ref, o_acc,
              stf_ref, sf_ref, qbf_ref, kef_ref, klf_ref,
              stb_ref, sb_ref, qbb_ref, keb_ref, klb_ref, *, block, sub):
    L, DK = q_ref.shape
    T, C = block, sub
    n, ns = L // T, T // C
    scale = DK ** -0.5
    row = lax.broadcasted_iota(jnp.int32, (C, C), 0)
    col = lax.broadcasted_iota(jnp.int32, (C, C), 1)

    def boundary_row(cum_ref, start, pick):
        return cum_ref[pl.ds(pl.multiple_of(start, 8), 8), :][pick:pick + 1, :]

    def block_step(blk, cum_ref, st_ref, s_ref, qb_ref, ke_ref, kl_ref, forward, first_visit):
        r0 = pl.multiple_of(blk * T, T)
        tot = boundary_row(cum_ref, r0 + T - 8, 7) if forward else boundary_row(cum_ref, r0, 0)
        mask = (col <= row) if forward else (col > row)
        refs = {}
        for I in (range(ns) if forward else range(ns - 1, -1, -1)):
            sl = slice(I * C, (I + 1) * C)
            rows = pl.ds(r0 + I * C, C)
            if forward:
                ref = boundary_row(cum_ref, r0 + I * C - 8, 7) if I > 0 else jnp.zeros_like(tot)
            else:
                ref = boundary_row(cum_ref, r0 + (I + 1) * C, 0) if I < ns - 1 else jnp.zeros_like(tot)
            d = cum_ref[rows, :] - ref
            q_loc = q_ref[rows, :] * (scale * jnp.exp(d))
            k_loc = k_ref[rows, :] * jnp.exp(-d)
            qb_ref[sl, :] = (q_loc * jnp.exp(ref)).astype(BF16)
            ke_ref[sl, :] = (k_loc * jnp.exp(tot - ref)).astype(BF16)
            k_loc = k_loc.astype(BF16)
            kl_ref[sl, :] = k_loc
            s_ref[sl, sl] = jnp.where(mask, _dot(q_loc, k_loc, NT), 0.0).astype(BF16)
            for J, ref_j in refs.items():
                sj = slice(J * C, (J + 1) * C)
                s_ref[sl, sj] = _dot(q_loc * jnp.exp(ref - ref_j), kl_ref[sj, :], NT).astype(BF16)
            refs[I] = ref
        rows = pl.ds(r0, T)
        vb = v_ref[rows, :].astype(BF16)
        st = st_ref[...]
        o = _dot(s_ref[...], vb) + _dot(qb_ref[...], st, NT)
        st_ref[...] = st * jnp.exp(tot) + _dot(vb, ke_ref[...], TN)
        if first_visit:
            o_acc[rows, :] = o
        else:
            o = o_acc[rows, :] + o
            o = o * lax.rsqrt(jnp.mean(o * o, axis=-1, keepdims=True) + LN_EPS) * g_ref[...]
            r = r_ref[rows, :].astype(F32)
            y_ref[rows, :] = (o * (r * _sigmoid(r))).astype(y_ref.dtype)

    fwd = functools.partial(block_step, cum_ref=cf_ref, st_ref=stf_ref, s_ref=sf_ref, qb_ref=qbf_ref,
                            ke_ref=kef_ref, kl_ref=klf_ref, forward=True)
    bwd = functools.partial(block_step, cum_ref=cb_ref, st_ref=stb_ref, s_ref=sb_ref, qb_ref=qbb_ref,
                            ke_ref=keb_ref, kl_ref=klb_ref, forward=False)

    for ref in (stf_ref, stb_ref, sf_ref, sb_ref):
        ref[...] = jnp.zeros_like(ref)

    def sweep(lo, hi, step, unroll=1):
        def body(i, carry):
            step(i)
            return carry
        lax.fori_loop(lo, hi, body, 0, unroll=unroll)

    if n % 2 == 0:
        def first_half(i):
            fwd(i, first_visit=True)
            bwd(n - 1 - i, first_visit=True)

        def second_half(i):
            fwd(i, first_visit=False)
            bwd(n - 1 - i, first_visit=False)

        sweep(0, n // 2, first_half, unroll=True)
        sweep(n // 2, n, second_half, unroll=True)
    else:
        sweep(0, n, lambda i: fwd(i, first_visit=True))
        sweep(0, n, lambda i: bwd(n - 1 - i, first_visit=False))


def _gla(p_qkvr, cum_f, cum_b, norm_g, B, L, KEY, VAL, block):
    H = GLA_HEADS
    DK, DV = KEY // H, VAL // H
    M = B * L
    kq = KEY // DK
    vq = 2 * KEY // DV
    rq = (2 * KEY + VAL) // DV
    T = block
    direction_scratch = [pltpu.VMEM((DV, DK), F32), pltpu.VMEM((T, T), BF16), pltpu.VMEM((T, DK), BF16),
                         pltpu.VMEM((T, DK), BF16), pltpu.VMEM((T, DK), BF16)]
    return _call(
        functools.partial(_gla_body, block=T, sub=_blk(T, GLA_CHUNK)),
        grid=(B, H),
        in_specs=[pl.BlockSpec((L, DK), lambda b, h: (b, h)),
                  pl.BlockSpec((L, DK), lambda b, h: (b, kq + h)),
                  pl.BlockSpec((L, DV), lambda b, h: (b, vq + h)),
                  pl.BlockSpec((L, DV), lambda b, h: (b, rq + h)),
                  pl.BlockSpec((L, DK), lambda b, h: (b, h)),
                  pl.BlockSpec((L, DK), lambda b, h: (b, h)),
                  pl.BlockSpec((1, DV), lambda b, h: (0, h))],
        out_specs=pl.BlockSpec((L, DV), lambda b, h: (b, h)),
        out_shape=jax.ShapeDtypeStruct((M, VAL), BF16),
        scratch_shapes=[pltpu.VMEM((L, DV), F32)] + direction_scratch * 2,
        name="gla_bidir",
    )(p_qkvr, p_qkvr, p_qkvr, p_qkvr, cum_f, cum_b, norm_g.reshape(1, VAL))


def _filter_body(emb_ref, w1_ref, b1_ref, w2_ref, b2_ref, w3_ref, b3_ref, fr_ref,
                 w4f_ref, b4f_ref, w4b_ref, b4b_ref, dl_ref, hf_ref, hb_ref, h_ref):
    @pl.when((pl.program_id(0) == 0) & (pl.program_id(1) == 0))
    def _():
        fr = fr_ref[...]
        h = jnp.sin(fr * (_dot_f32(w1_ref[...], emb_ref[...]) + b1_ref[...]))
        h = jnp.sin(fr * (_dot_f32(w2_ref[...], h) + b2_ref[...]))
        h_ref[...] = jnp.sin(fr * (_dot_f32(w3_ref[...], h) + b3_ref[...]))

    h = h_ref[...]
    t_lin = emb_ref[0:1, :]
    decay = jnp.exp(-t_lin * dl_ref[...])
    hf_ref[...] = (_dot_f32(w4f_ref[...], h) + b4f_ref[...]) * decay
    hb_ref[...] = (_dot_f32(w4b_ref[...], h) + b4b_ref[...]) * decay


def _hyena_filters(emb_t, w1, b1, w2, b2, w3, b3, w4, b4, freq, deltas, W):
    L = emb_t.shape[1]
    HID = w1.shape[1]
    n_ord = w4.shape[1] // (2 * W)
    EMB = -(-emb_t.shape[0] // V7X_LANES) * V7X_LANES
    w1 = jnp.pad(w1, ((0, EMB - w1.shape[0]), (0, 0)))
    emb_t = jnp.pad(emb_t, ((0, EMB - emb_t.shape[0]), (0, 0)))
    cb = _blk(W, 512)
    ncb = W // cb
    w4t = w4.T
    b4c = b4.reshape(-1, 1)
    colv = lambda v: v.reshape(-1, 1)
    full = lambda shape: pl.BlockSpec(shape, lambda o, c: (0, 0))
    rows = n_ord * W
    return _call(
        _filter_body,
        grid=(n_ord, ncb),
        in_specs=[full((EMB, L)), full((HID, EMB)), full((HID, 1)), full((HID, HID)), full((HID, 1)),
                  full((HID, HID)), full((HID, 1)), full((HID, 1)),
                  pl.BlockSpec((cb, HID), lambda o, c: (o * 2 * ncb + c, 0)),
                  pl.BlockSpec((cb, 1), lambda o, c: (o * 2 * ncb + c, 0)),
                  pl.BlockSpec((cb, HID), lambda o, c: (o * 2 * ncb + ncb + c, 0)),
                  pl.BlockSpec((cb, 1), lambda o, c: (o * 2 * ncb + ncb + c, 0)),
                  pl.BlockSpec((cb, 1), lambda o, c: (c, 0))],
        out_specs=[pl.BlockSpec((cb, L), lambda o, c: (o * ncb + c, 0))] * 2,
        out_shape=[jax.ShapeDtypeStruct((rows, L), F32)] * 2,
        scratch_shapes=[pltpu.VMEM((HID, L), F32)],
        name="hy_filters",
    )(emb_t, w1.T, colv(b1), w2.T, colv(b2), w3.T, colv(b3), colv(freq), w4t, b4c, w4t, b4c, deltas)


def _lag_spectrum_body(hf_ref, hb_ref, c_ref, s_ref, w_ref, hr_ref, hi_ref, hn_ref, *, L, T):
    nb = L // T
    rows = hf_ref.shape[0]
    lane = lax.broadcasted_iota(jnp.int32, (rows, T), 1)
    alt = jnp.where((lane & 1) == 0, 1.0, -1.0)
    w = w_ref[...]

    def transforms(x_ref):
        blocks = [x_ref[:, n * T:(n + 1) * T] for n in range(nb)]
        return ([_dot(x, c_ref[...]) for x in blocks], [_dot(x, s_ref[...]) for x in blocks],
                [x[:, 0:1] for x in blocks])

    fc, fs, f0 = transforms(hf_ref)
    bc, bs, b0 = transforms(hb_ref)
    ny = lambda s: s[:, 0:1]
    for d in range(-(nb - 1), nb):
        if d >= 1:
            hr = fc[d] + alt * (fc[d - 1] - f0[d - 1])
            hi = fs[d] + alt * fs[d - 1]
            hn = ny(fs[d]) + ny(fs[d - 1]) - f0[d - 1]
        elif d == 0:
            hr = fc[0] + bc[0] - b0[0]
            hi = fs[0] - bs[0]
            hn = ny(fs[0]) + ny(bs[0]) - b0[0]
        else:
            e = -d
            hr = bc[e] + alt * (bc[e - 1] - b0[e - 1])
            hi = -(bs[e] + alt * bs[e - 1])
            hn = ny(bs[e]) + ny(bs[e - 1]) - b0[e - 1]
        li = d + nb - 1
        hr_ref[li] = (hr * w).astype(hr_ref.dtype)
        hi_ref[li] = jnp.where(lane == 0, 0.0, hi * w).astype(hi_ref.dtype)
        hn_ref[li] = hn * w[:, 0:1]


def _lag_spectra(hf2, hb2, cmat, smat, bin_w, L, T):
    R = hf2.shape[0]
    assert T % 2 == 0 and L % T == 0
    nl = 2 * (L // T) - 1
    rb = _blk(R, 256)
    full = lambda shape: pl.BlockSpec(shape, lambda r: (0, 0))
    return _call(
        functools.partial(_lag_spectrum_body, L=L, T=T),
        grid=(R // rb,),
        in_specs=[pl.BlockSpec((rb, L), lambda r: (r, 0)), pl.BlockSpec((rb, L), lambda r: (r, 0)),
                  full((T, T)), full((T, T)), full((1, T))],
        out_specs=[pl.BlockSpec((nl, rb, T), lambda r: (0, r, 0)),
                   pl.BlockSpec((nl, rb, T), lambda r: (0, r, 0)),
                   pl.BlockSpec((nl, rb, 1), lambda r: (0, r, 0))],
        out_shape=[jax.ShapeDtypeStruct((nl, R, T), BF16), jax.ShapeDtypeStruct((nl, R, T), BF16),
                   jax.ShapeDtypeStruct((nl, R, 1), F32)],
        name="hy_lag_spectra",
    )(hf2, hb2, cmat, smat, bin_w)


def _hy_conv_body(z_ref, gate_ref, skip_ref, c_ref, s_ref, st_ref, hr_ref, hi_ref, hn_ref, o_ref,
                  zr_ref, zi_ref, yr_ref, yi_ref, *, T, chunk):
    R, L = z_ref.shape
    nb = L // T
    for j in range(nb):
        zj = z_ref[:, j * T:(j + 1) * T]
        zr_ref[j] = _dot(zj, c_ref[...]).astype(zr_ref.dtype)
        zi_ref[j] = _dot(zj, s_ref[...]).astype(zi_ref.dtype)

    bin0 = lax.broadcasted_iota(jnp.int32, (chunk, T), 1) == 0

    def combine(r, carry):
        rows = pl.ds(pl.multiple_of(r * chunk, chunk), chunk)
        zr = [zr_ref[j, rows, :] for j in range(nb)]
        zi = [zi_ref[j, rows, :] for j in range(nb)]
        zn = [z[:, 0:1].astype(F32) for z in zi]
        for i in range(nb):
            yr = yi = yn = None
            for j in range(nb):
                lag = i - j + nb - 1
                hr, hi = hr_ref[lag, rows, :], hi_ref[lag, rows, :]
                tr = zr[j] * hr - zi[j] * hi
                ti = zr[j] * hi + zi[j] * hr
                tn = zn[j] * hn_ref[lag, rows, :]
                yr, yi, yn = (tr, ti, tn) if j == 0 else (yr + tr, yi + ti, yn + tn)
            yr_ref[i, rows, :] = yr
            yi_ref[i, rows, :] = jnp.where(bin0, yn.astype(yi.dtype), yi)
        return carry

    lax.fori_loop(0, R // chunk, combine, 0, unroll=2)

    skip = skip_ref[...]
    for i in range(nb):
        sl = slice(i * T, (i + 1) * T)
        zc = _dot(yr_ref[i], c_ref[...]) + _dot(yi_ref[i], st_ref[...])
        o_ref[:, sl] = (gate_ref[:, sl] * (zc + skip * z_ref[:, sl])).astype(o_ref.dtype)


def _hy_conv(z2d, z_row0, z_rows_per_b, gates2d, g_row0, g_rows_per_b, skip_col, tables,
             hr, hi, hn, h_row0, B, W, L, T):
    R = _blk(W, 512)
    ncb = W // R
    nb, nl = L // T, hr.shape[0]
    zb, z0 = z_rows_per_b // R, z_row0 // R
    gb, g0 = g_rows_per_b // R, g_row0 // R
    h0 = h_row0 // R
    table = pl.BlockSpec((T, T), lambda c, b: (0, 0))
    spectrum = lambda last: pl.BlockSpec((nl, R, last), lambda c, b: (0, h0 + c, 0))
    return _call(
        functools.partial(_hy_conv_body, T=T, chunk=16),
        grid=(ncb, B),
        in_specs=[pl.BlockSpec((R, L), lambda c, b: (b * zb + z0 + c, 0)),
                  pl.BlockSpec((R, L), lambda c, b: (b * gb + g0 + c, 0)),
                  pl.BlockSpec((R, 1), lambda c, b: (h0 + c, 0)),
                  table, table, table, spectrum(T), spectrum(T), spectrum(1)],
        out_specs=pl.BlockSpec((R, L), lambda c, b: (b * ncb + c, 0)),
        out_shape=jax.ShapeDtypeStruct((B * W, L), BF16),
        scratch_shapes=[pltpu.VMEM((nb, R, T), BF16)] * 4,
        name="hy_conv",
    )(z2d, gates2d, skip_col, *tables, hr, hi, hn)


def _merge_body(yg_ref, yh_ref, wg_ref, wh_ref, g0_ref, g1_ref, o_ref):
    tg = _dot(yg_ref[...], wg_ref[...])
    th = _dot(yh_ref[...], wh_ref[...], TN)
    o_ref[...] = (g0_ref[...] * tg + g1_ref[...] * th).astype(o_ref.dtype)


def _branch_merge(y_gla, y_hy_t, w_gla_o, w_hy_o, gates, B, L, D):
    VAL, W = y_gla.shape[1], w_hy_o.shape[0]
    bt = _blk(L, 512)
    nt = L // bt
    resident = lambda shape: pl.BlockSpec(shape, lambda b, t: (0, 0), pipeline_mode=pl.Buffered(1))
    return _call(
        _merge_body,
        grid=(B, nt),
        in_specs=[pl.BlockSpec((bt, VAL), lambda b, t: (b * nt + t, 0)),
                  pl.BlockSpec((W, bt), lambda b, t: (b, t)),
                  resident((VAL, D)), resident((W, D)),
                  pl.BlockSpec((bt, D), lambda b, t: (b * nt + t, 0)),
                  pl.BlockSpec((bt, D), lambda b, t: (b * nt + t, 1))],
        out_specs=pl.BlockSpec((bt, D), lambda b, t: (b * nt + t, 0)),
        out_shape=jax.ShapeDtypeStruct((B * L, D), BF16),
        name="branch_merge",
    )(y_gla, y_hy_t, w_gla_o, w_hy_o, gates, gates)


def _proj_ln_body(a_ref, w_ref, x_ref, g_ref, b_ref, o_ref, ob_ref, *, alpha):
    y = alpha * x_ref[...] + _dot(a_ref[...], w_ref[...])
    h = _layernorm_rows(y, g_ref[...], b_ref[...])
    o_ref[...] = h
    ob_ref[...] = h.astype(ob_ref.dtype)


def _proj_ln(a, w, x2d, ln_g, ln_b, alpha):
    M, K = a.shape
    D = w.shape[1]
    bm = _blk(M, 512)
    row = lambda i: (i, 0)
    full = lambda shape: pl.BlockSpec(shape, lambda i: (0, 0))
    return _call(
        functools.partial(_proj_ln_body, alpha=alpha),
        grid=(M // bm,),
        in_specs=[pl.BlockSpec((bm, K), row), full((K, D)), pl.BlockSpec((bm, D), row),
                  full((1, D)), full((1, D))],
        out_specs=[pl.BlockSpec((bm, D), row)] * 2,
        out_shape=[jax.ShapeDtypeStruct((M, D), F32), jax.ShapeDtypeStruct((M, D), BF16)],
        name="out_proj_ln1",
    )(a, w, x2d, ln_g.reshape(1, D), ln_b.reshape(1, D))


def _ff2_ln_body(a_ref, w_ref, h_ref, g_ref, b_ref, o_ref, *, alpha, ln_rows):
    kk = pl.program_id(1)

    @pl.when(kk == 0)
    def _():
        o_ref[...] = jnp.zeros_like(o_ref)

    o_ref[...] += _dot(a_ref[...], w_ref[...])

    @pl.when(kk == pl.num_programs(1) - 1)
    def _():
        g, b = g_ref[...], b_ref[...]

        def norm_rows(r, carry):
            rows = pl.ds(pl.multiple_of(r * ln_rows, ln_rows), ln_rows)
            y = alpha * h_ref[rows, :] + o_ref[rows, :]
            o_ref[rows, :] = _layernorm_rows(y, g, b)
            return carry

        lax.fori_loop(0, o_ref.shape[0] // ln_rows, norm_rows, 0)


def _ff2_ln(a, w, h, ln_g, ln_b, alpha):
    M, K = a.shape
    D = w.shape[1]
    bm, bk = _blk(M, 1024), _blk(K, 1024)
    return _call(
        functools.partial(_ff2_ln_body, alpha=alpha, ln_rows=_blk(bm, 128)),
        grid=(M // bm, K // bk),
        in_specs=[pl.BlockSpec((bm, bk), lambda i, k: (i, k)),
                  pl.BlockSpec((bk, D), lambda i, k: (k, 0)),
                  pl.BlockSpec((bm, D), lambda i, k: (i, 0)),
                  pl.BlockSpec((1, D), lambda i, k: (0, 0)),
                  pl.BlockSpec((1, D), lambda i, k: (0, 0))],
        out_specs=pl.BlockSpec((bm, D), lambda i, k: (i, 0)),
        out_shape=jax.ShapeDtypeStruct((M, D), F32),
        name="ff2_ln2",
    )(a, w, h, ln_g.reshape(1, D), ln_b.reshape(1, D))


def _dft_tables(L):
    LO = _blk(L, 32)
    k = jnp.arange(L, dtype=jnp.int32)
    ang = lambda m: ((m[:, None] * k[None, :]) % (2 * L)).astype(F32) * (math.pi / L)
    a_hi = ang(jnp.arange(L // LO, dtype=jnp.int32) * LO)[:, None, :]
    a_lo = ang(jnp.arange(LO, dtype=jnp.int32))[None, :, :]
    cos_t = (jnp.cos(a_hi) * jnp.cos(a_lo) - jnp.sin(a_hi) * jnp.sin(a_lo)).reshape(L, L)
    sin_t = (jnp.sin(a_hi) * jnp.cos(a_lo) + jnp.cos(a_hi) * jnp.sin(a_lo)).reshape(L, L)
    nyq = jnp.where(k % 2 == 0, 1.0, -1.0).astype(F32)
    smat = jnp.where(k[None, :] == 0, nyq[:, None], sin_t)
    smat_t = jnp.where(k[:, None] == 0, nyq[None, :], sin_t)
    return cos_t.astype(BF16), smat.astype(BF16), smat_t.astype(BF16)


def _position_features(L, emb_dim):
    t = jnp.linspace(0.0, 1.0, L, dtype=F32)[:, None]
    bands = (emb_dim - 1) // 2
    f = jnp.linspace(1e-4, bands - 1, bands, dtype=F32)
    wpos = 2.0 * math.pi * jnp.arange(L, dtype=F32) / L
    ang = wpos[:, None] * f[None, :]
    return jnp.concatenate([t, jnp.cos(ang), -jnp.sin(ang)], axis=-1).T


def _layer(h, w_in, gla_wa2_f, gla_ba_f, gla_wa2_b, gla_ba_b, gla_norm_g, w_gla_o,
           hy_conv_w, hy_conv_b, hy_w1, hy_b1, hy_w2, hy_b2, hy_w3, hy_b3, hy_w4, hy_b4,
           hy_freq, hy_skip, w_hy_o, w_out, ln1_g, ln1_b, w_ff1, w_ff2, ln2_g, ln2_b, alpha):
    B, L, D = h.shape
    M = B * L
    rank, KEY = gla_wa2_f.shape
    VAL = gla_norm_g.shape[0]
    n_ord, W = hy_skip.shape

    sizes = (KEY, KEY, VAL, VAL, rank, rank, (n_ord + 1) * W, 2 * D)
    offs = [int(v) for v in np.concatenate([[0], np.cumsum(sizes)])]
    w_in_t = w_in.T
    w_hy_t = w_in_t[offs[6]:offs[7]].astype(BF16)

    x2d = h.reshape(M, D)
    gla_block = _blk(L, GLA_BLOCK)
    cum_f, cum_b, xb = _gla_decays(x2d, w_in_t, offs[4], gla_wa2_f, gla_ba_f, gla_wa2_b, gla_ba_b, gla_block)

    p_qkvr = _matmul_f32w(xb, w_in_t, offs[4], out_dtype=BF16, name="in_proj_qkvr", bm=2048, transposed=True)
    gates = _matmul_f32w(xb, w_in_t, sizes[7], out_dtype=BF16, act="sigmoid", name="in_proj_gates", bm=2048,
                         transposed=True, row0=offs[7])
    conv_params = jnp.concatenate([hy_conv_w.T, hy_conv_b[:, None]], axis=1)
    u_t = _hy_proj(w_hy_t, xb, conv_params, B, L)

    y_gla = _gla(p_qkvr, cum_f, cum_b, gla_norm_g, B, L, KEY, VAL, gla_block)

    emb_t = _position_features(L, hy_w1.shape[0])
    min_decay = math.log(HY_DECAY_TARGET) / HY_SLOW_DECAY
    max_decay = math.log(HY_DECAY_TARGET) / HY_FAST_DECAY
    deltas = jnp.abs(jnp.linspace(min_decay, max_decay, W, dtype=F32)).reshape(W, 1)
    hf2, hb2 = _hyena_filters(emb_t, hy_w1, hy_b1, hy_w2, hy_b2, hy_w3, hy_b3,
                              hy_w4, hy_b4, hy_freq, deltas, W)
    T = _blk(L // 2, HY_BLOCK)
    tables = _dft_tables(T)
    bin_w = jnp.full((1, T), 1.0 / T, F32).at[0, 0].set(0.5 / T)
    h_re, h_im, h_ny = _lag_spectra(hf2, hb2, tables[0], tables[1], bin_w, L, T)

    skip_col = hy_skip.reshape(n_ord * W, 1)
    z, z_row0, z_rows = u_t, 0, (n_ord + 1) * W
    for o in range(n_ord):
        z = _hy_conv(z, z_row0, z_rows, u_t, (o + 1) * W, (n_ord + 1) * W, skip_col, tables,
                     h_re, h_im, h_ny, o * W, B, W, L, T)
        z_row0, z_rows = 0, W
    y_hy_t = z

    merged = _branch_merge(y_gla, y_hy_t, w_gla_o.astype(BF16), w_hy_o.astype(BF16), gates, B, L, D)
    h1, h1b = _proj_ln(merged, w_out.astype(BF16), x2d, ln1_g, ln1_b, alpha)
    act = _matmul_f32w(h1b, w_ff1, w_ff1.shape[1], out_dtype=BF16, act="relu2", name="ff1_relu2", bm=2048)
    out = _ff2_ln(act, w_ff2.astype(BF16), h1, ln2_g, ln2_b, alpha)
    return out.reshape(B, L, D)


def kernel(x, w_in, gla_wa2_f, gla_ba_f, gla_wa2_b, gla_ba_b, gla_norm_g, w_gla_o, hy_conv_w, hy_conv_b, hy_w1, hy_b1, hy_w2, hy_b2, hy_w3, hy_b3, hy_w4, hy_b4, hy_freq, hy_skip, w_hy_o, w_out, ln1_g, ln1_b, w_ff1, w_ff2, ln2_g, ln2_b):
    depth = w_in.shape[0]
    alpha = (2 * depth) ** 0.25
    params = (w_in, gla_wa2_f, gla_ba_f, gla_wa2_b, gla_ba_b, gla_norm_g, w_gla_o, hy_conv_w, hy_conv_b,
              hy_w1, hy_b1, hy_w2, hy_b2, hy_w3, hy_b3, hy_w4, hy_b4, hy_freq, hy_skip, w_hy_o, w_out,
              ln1_g, ln1_b, w_ff1, w_ff2, ln2_g, ln2_b)
    h = x
    for l in range(depth):
        h = _layer(h, *(p[l] for p in params), alpha)
    return h
```

```python
import functools
import math

import jax
import jax.numpy as jnp
import numpy as np
from jax import lax
from jax.experimental import pallas as pl
from jax.experimental.pallas import tpu as pltpu

F32 = jnp.float32
BF16 = jnp.bfloat16

GLA_HEADS = 4
GLA_TAU = 16.0
GLA_CHUNK = 64
GLA_BLOCK = 256
HY_BLOCK = 512
HY_FAST_DECAY = 0.3
HY_SLOW_DECAY = 1.5
HY_DECAY_TARGET = 1e-2
LN_EPS = 1e-5

V7X_VMEM_LIMIT_BYTES = 56 * 1024 * 1024
V7X_LANES = 128
V7X_SUBLANES = 8

NN = (((1,), (0,)), ((), ()))
NT = (((1,), (1,)), ((), ()))
TN = (((0,), (0,)), ((), ()))


def _dot(a, b, dims=NN):
    return lax.dot_general(a.astype(BF16), b.astype(BF16), dims, preferred_element_type=F32)


def _split2(a):
    h1 = a.astype(BF16)
    h2 = (a - h1.astype(F32)).astype(BF16)
    return h1, h2


def _dot_f32(a, b, dims=NN):
    a1, a2 = _split2(a)
    b1, b2 = _split2(b)
    d = lambda x, y: lax.dot_general(x, y, dims, preferred_element_type=F32)
    return (d(a2, b1) + d(a1, b2)) + d(a1, b1)


def _dot_exact_lhs(a_bf16, b, dims=NN):
    b1, b2 = _split2(b)
    d = lambda y: lax.dot_general(a_bf16, y, dims, preferred_element_type=F32)
    return d(b2) + d(b1)


def _call(body, *, grid, in_specs, out_specs, out_shape, name, scratch_shapes=()):
    return pl.pallas_call(
        body,
        grid=grid,
        in_specs=in_specs,
        out_specs=out_specs,
        out_shape=out_shape,
        scratch_shapes=scratch_shapes,
        compiler_params=pltpu.CompilerParams(
            dimension_semantics=("arbitrary",) * len(grid),
            vmem_limit_bytes=V7X_VMEM_LIMIT_BYTES,
        ),
        name=name,
    )


def _blk(n, want):
    b = min(n, want)
    while n % b:
        b //= 2
    return b


def _layernorm_rows(y, g, b):
    mu = jnp.mean(y, axis=-1, keepdims=True)
    d = y - mu
    var = jnp.mean(d * d, axis=-1, keepdims=True)
    return d * lax.rsqrt(var + LN_EPS) * g + b


def _sigmoid(x):
    return 0.5 * jnp.tanh(0.5 * x) + 0.5


def _act(acc, act):
    if act == "sigmoid":
        return _sigmoid(acc)
    if act == "relu2":
        return jnp.square(jnp.maximum(acc, 0.0))
    return acc


def _weight_spec(K, bn, transposed, index, row0=0):
    if transposed and row0:
        assert row0 % V7X_SUBLANES == 0
        return pl.BlockSpec((pl.Element(bn), pl.Element(K)),
                            lambda *g: (pl.multiple_of(row0 + index(*g) * bn, V7X_SUBLANES), 0))
    if transposed:
        return pl.BlockSpec((bn, K), lambda *g: (index(*g), 0))
    assert row0 == 0
    return pl.BlockSpec((K, bn), lambda *g: (0, index(*g)))


def _mm_f32w_body(a_ref, w_ref, o_ref, wb_ref, *, act, dims):
    @pl.when(pl.program_id(1) == 0)
    def _():
        wb_ref[...] = w_ref[...].astype(wb_ref.dtype)

    o_ref[...] = _act(_dot(a_ref[...], wb_ref[...], dims), act).astype(o_ref.dtype)


def _matmul_f32w(a, w, n_out, *, out_dtype, act=None, name, bm=1024, bn=1024, transposed=False, row0=0):
    M, K = a.shape
    bm, bn = _blk(M, bm), _blk(n_out, bn)
    return _call(
        functools.partial(_mm_f32w_body, act=act, dims=NT if transposed else NN),
        grid=(n_out // bn, M // bm),
        in_specs=[pl.BlockSpec((bm, K), lambda j, i: (i, 0)),
                  _weight_spec(K, bn, transposed, lambda j, i: j, row0)],
        out_specs=pl.BlockSpec((bm, bn), lambda j, i: (i, j)),
        out_shape=jax.ShapeDtypeStruct((M, n_out), out_dtype),
        scratch_shapes=[pltpu.VMEM((bn, K) if transposed else (K, bn), BF16)],
        name=name,
    )(a, w)


def _hy_proj_body(w_ref, x_ref, cp_ref, o_ref):
    u = _dot(w_ref[...], x_ref[...], NT)
    L = u.shape[1]
    t = lax.broadcasted_iota(jnp.int32, u.shape, 1)
    prev = jnp.where(t == 0, 0.0, pltpu.roll(u, 1, axis=1))
    nxt = jnp.where(t == L - 1, 0.0, pltpu.roll(u, L - 1, axis=1))
    cp = cp_ref[...]
    o_ref[...] = (cp[:, 0:1] * prev + cp[:, 1:2] * u + cp[:, 2:3] * nxt + cp[:, 3:4]).astype(o_ref.dtype)


def _hy_proj(w_t, xb, conv_params, B, L):
    C3, D = w_t.shape
    bc = _blk(C3, 1024)
    nc = C3 // bc
    return _call(
        _hy_proj_body,
        grid=(B, nc),
        in_specs=[pl.BlockSpec((bc, D), lambda b, c: (c, 0)),
                  pl.BlockSpec((L, D), lambda b, c: (b, 0)),
                  pl.BlockSpec((bc, 4), lambda b, c: (c, 0))],
        out_specs=pl.BlockSpec((bc, L), lambda b, c: (b * nc + c, 0)),
        out_shape=jax.ShapeDtypeStruct((B * C3, L), BF16),
        name="hy_proj_conv",
    )(w_t, xb, conv_params)


def _log_sigmoid(z):
    return -(jnp.maximum(-z, 0.0) + jnp.log(1.0 + jnp.exp(-jnp.abs(z))))


def _decay_body(x_ref, wab_ref, w2f_ref, bf_ref, w2b_ref, bb_ref, cf_ref, cb_ref, xb_ref, *, block):
    xb = x_ref[...].astype(xb_ref.dtype)
    xb_ref[...] = xb
    ab = _dot(xb, wab_ref[...], NT)
    zf = _dot(ab, w2f_ref[...]) + bf_ref[...]
    zb = _dot(ab, w2b_ref[...]) + bb_ref[...]
    laf = _log_sigmoid(zf) * (1.0 / GLA_TAU)
    lab = _log_sigmoid(zb) * (1.0 / GLA_TAU)
    T = block
    row = lax.broadcasted_iota(jnp.int32, (T, T), 0)
    col = lax.broadcasted_iota(jnp.int32, (T, T), 1)
    tri_f = (col <= row).astype(BF16)
    tri_b = (col >= row).astype(BF16)
    for s in range(xb.shape[0] // T):
        cf_ref[s * T:(s + 1) * T, :] = _dot_exact_lhs(tri_f, laf[s * T:(s + 1) * T, :])
        cb_ref[s * T:(s + 1) * T, :] = _dot_exact_lhs(tri_b, lab[s * T:(s + 1) * T, :])


def _gla_decays(x2d, w_in_t, ab_col, wa2_f, ba_f, wa2_b, ba_b, block):
    M, D = x2d.shape
    rank, KEY = wa2_f.shape
    P = V7X_LANES
    assert 2 * rank <= P and ab_col % P == 0
    w2f = jnp.pad(wa2_f, ((0, P - rank), (0, 0)))
    w2b = jnp.pad(wa2_b, ((rank, P - 2 * rank), (0, 0)))
    bm = max(_blk(M, 512), block)
    full = lambda shape: pl.BlockSpec(shape, lambda i: (0, 0))
    out = jax.ShapeDtypeStruct((M, KEY), F32)
    return _call(
        functools.partial(_decay_body, block=block),
        grid=(M // bm,),
        in_specs=[pl.BlockSpec((bm, D), lambda i: (i, 0)), pl.BlockSpec((P, D), lambda i: (ab_col // P, 0)),
                  full((P, KEY)), full((1, KEY)), full((P, KEY)), full((1, KEY))],
        out_specs=[pl.BlockSpec((bm, KEY), lambda i: (i, 0))] * 2 + [pl.BlockSpec((bm, D), lambda i: (i, 0))],
        out_shape=[out, out, jax.ShapeDtypeStruct((M, D), BF16)],
        name="gla_decays",
    )(x2d, w_in_t, w2f, ba_f.reshape(1, KEY), w2b, ba_b.reshape(1, KEY))


def _gla_body(q_ref, k_ref, v_ref, r_ref, cf_ref, cb_ref, g_ref, y_ref, o_acc,
              stf_ref, sf_ref, qbf_ref, kef_ref, klf_ref,
              stb_ref, sb_ref, qbb_ref, keb_ref, klb_ref, *, block, sub):
    L, DK = q_ref.shape
    T, C = block, sub
    n, ns = L // T, T // C
    scale = DK ** -0.5
    row = lax.broadcasted_iota(jnp.int32, (C, C), 0)
    col = lax.broadcasted_iota(jnp.int32, (C, C), 1)

    def boundary_row(cum_ref, start, pick):
        return cum_ref[pl.ds(pl.multiple_of(start, 8), 8), :][pick:pick + 1, :]

    def block_step(blk, cum_ref, st_ref, s_ref, qb_ref, ke_ref, kl_ref, forward, first_visit):
        r0 = pl.multiple_of(blk * T, T)
        tot = boundary_row(cum_ref, r0 + T - 8, 7) if forward else boundary_row(cum_ref, r0, 0)
        mask = (col <= row) if forward else (col > row)
        refs = {}
        for I in (range(ns) if forward else range(ns - 1, -1, -1)):
            sl = slice(I * C, (I + 1) * C)
            rows = pl.ds(r0 + I * C, C)
            if forward:
                ref = boundary_row(cum_ref, r0 + I * C - 8, 7) if I > 0 else jnp.zeros_like(tot)
            else:
                ref = boundary_row(cum_ref, r0 + (I + 1) * C, 0) if I < ns - 1 else jnp.zeros_like(tot)
            d = cum_ref[rows, :] - ref
            q_loc = q_ref[rows, :] * (scale * jnp.exp(d))
            k_loc = k_ref[rows, :] * jnp.exp(-d)
            qb_ref[sl, :] = (q_loc * jnp.exp(ref)).astype(BF16)
            ke_ref[sl, :] = (k_loc * jnp.exp(tot - ref)).astype(BF16)
            k_loc = k_loc.astype(BF16)
            kl_ref[sl, :] = k_loc
            s_ref[sl, sl] = jnp.where(mask, _dot(q_loc, k_loc, NT), 0.0).astype(BF16)
            for J, ref_j in refs.items():
                sj = slice(J * C, (J + 1) * C)
                s_ref[sl, sj] = _dot(q_loc * jnp.exp(ref - ref_j), kl_ref[sj, :], NT).astype(BF16)
            refs[I] = ref
        rows = pl.ds(r0, T)
        vb = v_ref[rows, :].astype(BF16)
        st = st_ref[...]
        o = _dot(s_ref[...], vb) + _dot(qb_ref[...], st, NT)
        st_ref[...] = st * jnp.exp(tot) + _dot(vb, ke_ref[...], TN)
        if first_visit:
            o_acc[rows, :] = o
        else:
            o = o_acc[rows, :] + o
            o = o * lax.rsqrt(jnp.mean(o * o, axis=-1, keepdims=True) + LN_EPS) * g_ref[...]
            r = r_ref[rows, :].astype(F32)
            y_ref[rows, :] = (o * (r * _sigmoid(r))).astype(y_ref.dtype)

    fwd = functools.partial(block_step, cum_ref=cf_ref, st_ref=stf_ref, s_ref=sf_ref, qb_ref=qbf_ref,
                            ke_ref=kef_ref, kl_ref=klf_ref, forward=True)
    bwd = functools.partial(block_step, cum_ref=cb_ref, st_ref=stb_ref, s_ref=sb_ref, qb_ref=qbb_ref,
                            ke_ref=keb_ref, kl_ref=klb_ref, forward=False)

    for ref in (stf_ref, stb_ref, sf_ref, sb_ref):
        ref[...] = jnp.zeros_like(ref)

    def sweep(lo, hi, step, unroll=1):
        def body(i, carry):
            step(i)
            return carry
        lax.fori_loop(lo, hi, body, 0, unroll=unroll)

    if n % 2 == 0:
        def first_half(i):
            fwd(i, first_visit=True)
            bwd(n - 1 - i, first_visit=True)

        def second_half(i):
            fwd(i, first_visit=False)
            bwd(n - 1 - i, first_visit=False)

        sweep(0, n // 2, first_half, unroll=True)
        sweep(n // 2, n, second_half, unroll=True)
    else:
        sweep(0, n, lambda i: fwd(i, first_visit=True))
        sweep(0, n, lambda i: bwd(n - 1 - i, first_visit=False))


def _gla(p_qkvr, cum_f, cum_b, norm_g, B, L, KEY, VAL, block):
    H = GLA_HEADS
    DK, DV = KEY // H, VAL // H
    M = B * L
    kq = KEY // DK
    vq = 2 * KEY // DV
    rq = (2 * KEY + VAL) // DV
    T = block
    direction_scratch = [pltpu.VMEM((DV, DK), F32), pltpu.VMEM((T, T), BF16), pltpu.VMEM((T, DK), BF16),
                         pltpu.VMEM((T, DK), BF16), pltpu.VMEM((T, DK), BF16)]
    return _call(
        functools.partial(_gla_body, block=T, sub=_blk(T, GLA_CHUNK)),
        grid=(B, H),
        in_specs=[pl.BlockSpec((L, DK), lambda b, h: (b, h)),
                  pl.BlockSpec((L, DK), lambda b, h: (b, kq + h)),
                  pl.BlockSpec((L, DV), lambda b, h: (b, vq + h)),
                  pl.BlockSpec((L, DV), lambda b, h: (b, rq + h)),
                  pl.BlockSpec((L, DK), lambda b, h: (b, h)),
                  pl.BlockSpec((L, DK), lambda b, h: (b, h)),
                  pl.BlockSpec((1, DV), lambda b, h: (0, h))],
        out_specs=pl.BlockSpec((L, DV), lambda b, h: (b, h)),
        out_shape=jax.ShapeDtypeStruct((M, VAL), BF16),
        scratch_shapes=[pltpu.VMEM((L, DV), F32)] + direction_scratch * 2,
        name="gla_bidir",
    )(p_qkvr, p_qkvr, p_qkvr, p_qkvr, cum_f, cum_b, norm_g.reshape(1, VAL))


def _filter_body(emb_ref, w1_ref, b1_ref, w2_ref, b2_ref, w3_ref, b3_ref, fr_ref,
                 w4f_ref, b4f_ref, w4b_ref, b4b_ref, dl_ref, hf_ref, hb_ref, h_ref):
    @pl.when((pl.program_id(0) == 0) & (pl.program_id(1) == 0))
    def _():
        fr = fr_ref[...]
        h = jnp.sin(fr * (_dot_f32(w1_ref[...], emb_ref[...]) + b1_ref[...]))
        h = jnp.sin(fr * (_dot_f32(w2_ref[...], h) + b2_ref[...]))
        h_ref[...] = jnp.sin(fr * (_dot_f32(w3_ref[...], h) + b3_ref[...]))

    h = h_ref[...]
    t_lin = emb_ref[0:1, :]
    decay = jnp.exp(-t_lin * dl_ref[...])
    hf_ref[...] = (_dot_f32(w4f_ref[...], h) + b4f_ref[...]) * decay
    hb_ref[...] = (_dot_f32(w4b_ref[...], h) + b4b_ref[...]) * decay


def _lag_spectrum_body(hf_ref, hb_ref, c_ref, s_ref, w_ref, hr_ref, hi_ref, hn_ref, *, L, T):
    nb = L // T
    rows = hf_ref.shape[0]
    lane = lax.broadcasted_iota(jnp.int32, (rows, T), 1)
    alt = jnp.where((lane & 1) == 0, 1.0, -1.0)
    w = w_ref[...]

    def transforms(x_ref):
        blocks = [x_ref[:, n * T:(n + 1) * T] for n in range(nb)]
        return ([_dot(x, c_ref[...]) for x in blocks], [_dot(x, s_ref[...]) for x in blocks],
                [x[:, 0:1] for x in blocks])

    fc, fs, f0 = transforms(hf_ref)
    bc, bs, b0 = transforms(hb_ref)
    ny = lambda s: s[:, 0:1]
    for d in range(-(nb - 1), nb):
        if d >= 1:
            hr = fc[d] + alt * (fc[d - 1] - f0[d - 1])
            hi = fs[d] + alt * fs[d - 1]
            hn = ny(fs[d]) + ny(fs[d - 1]) - f0[d - 1]
        elif d == 0:
            hr = fc[0] + bc[0] - b0[0]
            hi = fs[0] - bs[0]
            hn = ny(fs[0]) + ny(bs[0]) - b0[0]
        else:
            e = -d
            hr = bc[e] + alt * (bc[e - 1] - b0[e - 1])
            hi = -(bs[e] + alt * bs[e - 1])
            hn = ny(bs[e]) + ny(bs[e - 1]) - b0[e - 1]
        li = d + nb - 1
        hr_ref[li] = (hr * w).astype(hr_ref.dtype)
        hi_ref[li] = jnp.where(lane == 0, 0.0, hi * w).astype(hi_ref.dtype)
        hn_ref[li] = hn * w[:, 0:1]


def _filter_spectra_body(*refs, L, T):
    filter_in, (c_ref, s_ref, w_ref, hr_ref, hi_ref, hn_ref, h_ref, hf_ref, hb_ref) = refs[:13], refs[13:]
    _filter_body(*filter_in, hf_ref, hb_ref, h_ref)
    _lag_spectrum_body(hf_ref, hb_ref, c_ref, s_ref, w_ref, hr_ref, hi_ref, hn_ref, L=L, T=T)


def _hyena_lag_spectra(emb_t, w1, b1, w2, b2, w3, b3, w4, b4, freq, deltas, cmat, smat, bin_w, W, T):
    L = emb_t.shape[1]
    HID = w1.shape[1]
    n_ord = w4.shape[1] // (2 * W)
    assert T % 2 == 0 and L % T == 0
    nl = 2 * (L // T) - 1
    EMB = -(-emb_t.shape[0] // V7X_LANES) * V7X_LANES
    w1 = jnp.pad(w1, ((0, EMB - w1.shape[0]), (0, 0)))
    emb_t = jnp.pad(emb_t, ((0, EMB - emb_t.shape[0]), (0, 0)))
    cb = _blk(W, 256)
    ncb = W // cb
    w4t = w4.T
    b4c = b4.reshape(-1, 1)
    colv = lambda v: v.reshape(-1, 1)
    full = lambda shape: pl.BlockSpec(shape, lambda o, c: (0, 0))
    rows = n_ord * W
    spectrum = lambda last: pl.BlockSpec((nl, cb, last), lambda o, c: (0, o * ncb + c, 0))
    return _call(
        functools.partial(_filter_spectra_body, L=L, T=T),
        grid=(n_ord, ncb),
        in_specs=[full((EMB, L)), full((HID, EMB)), full((HID, 1)), full((HID, HID)), full((HID, 1)),
                  full((HID, HID)), full((HID, 1)), full((HID, 1)),
                  pl.BlockSpec((cb, HID), lambda o, c: (o * 2 * ncb + c, 0)),
                  pl.BlockSpec((cb, 1), lambda o, c: (o * 2 * ncb + c, 0)),
                  pl.BlockSpec((cb, HID), lambda o, c: (o * 2 * ncb + ncb + c, 0)),
                  pl.BlockSpec((cb, 1), lambda o, c: (o * 2 * ncb + ncb + c, 0)),
                  pl.BlockSpec((cb, 1), lambda o, c: (c, 0)),
                  full((T, T)), full((T, T)), full((1, T))],
        out_specs=[spectrum(T), spectrum(T), spectrum(1)],
        out_shape=[jax.ShapeDtypeStruct((nl, rows, T), BF16), jax.ShapeDtypeStruct((nl, rows, T), BF16),
                   jax.ShapeDtypeStruct((nl, rows, 1), F32)],
        scratch_shapes=[pltpu.VMEM((HID, L), F32), pltpu.VMEM((cb, L), F32), pltpu.VMEM((cb, L), F32)],
        name="hy_filter_lag_spectra",
    )(emb_t, w1.T, colv(b1), w2.T, colv(b2), w3.T, colv(b3), colv(freq), w4t, b4c, w4t, b4c, deltas,
      cmat, smat, bin_w)


def _hy_conv_body(z_ref, gate_ref, skip_ref, c_ref, s_ref, st_ref, hr_ref, hi_ref, hn_ref, o_ref,
                  zr_ref, zi_ref, yr_ref, yi_ref, *, T, chunk):
    R, L = z_ref.shape
    nb = L // T
    for j in range(nb):
        zj = z_ref[:, j * T:(j + 1) * T]
        zr_ref[j] = _dot(zj, c_ref[...]).astype(zr_ref.dtype)
        zi_ref[j] = _dot(zj, s_ref[...]).astype(zi_ref.dtype)

    bin0 = lax.broadcasted_iota(jnp.int32, (chunk, T), 1) == 0

    def combine(r, carry):
        rows = pl.ds(pl.multiple_of(r * chunk, chunk), chunk)
        zr = [zr_ref[j, rows, :] for j in range(nb)]
        zi = [zi_ref[j, rows, :] for j in range(nb)]
        zn = [z[:, 0:1].astype(F32) for z in zi]
        for i in range(nb):
            yr = yi = yn = None
            for j in range(nb):
                lag = i - j + nb - 1
                hr, hi = hr_ref[lag, rows, :], hi_ref[lag, rows, :]
                tr = zr[j] * hr - zi[j] * hi
                ti = zr[j] * hi + zi[j] * hr
                tn = zn[j] * hn_ref[lag, rows, :]
                yr, yi, yn = (tr, ti, tn) if j == 0 else (yr + tr, yi + ti, yn + tn)
            yr_ref[i, rows, :] = yr
            yi_ref[i, rows, :] = jnp.where(bin0, yn.astype(yi.dtype), yi)
        return carry

    lax.fori_loop(0, R // chunk, combine, 0, unroll=2)

    skip = skip_ref[...]
    for i in range(nb):
        sl = slice(i * T, (i + 1) * T)
        zc = _dot(yr_ref[i], c_ref[...]) + _dot(yi_ref[i], st_ref[...])
        o_ref[:, sl] = (gate_ref[:, sl] * (zc + skip * z_ref[:, sl])).astype(o_ref.dtype)


def _hy_conv(z2d, z_row0, z_rows_per_b, gates2d, g_row0, g_rows_per_b, skip_col, tables,
             hr, hi, hn, h_row0, B, W, L, T):
    R = _blk(W, 512)
    ncb = W // R
    nb, nl = L // T, hr.shape[0]
    zb, z0 = z_rows_per_b // R, z_row0 // R
    gb, g0 = g_rows_per_b // R, g_row0 // R
    h0 = h_row0 // R
    table = pl.BlockSpec((T, T), lambda c, b: (0, 0))
    spectrum = lambda last: pl.BlockSpec((nl, R, last), lambda c, b: (0, h0 + c, 0))
    return _call(
        functools.partial(_hy_conv_body, T=T, chunk=16),
        grid=(ncb, B),
        in_specs=[pl.BlockSpec((R, L), lambda c, b: (b * zb + z0 + c, 0)),
                  pl.BlockSpec((R, L), lambda c, b: (b * gb + g0 + c, 0)),
                  pl.BlockSpec((R, 1), lambda c, b: (h0 + c, 0)),
                  table, table, table, spectrum(T), spectrum(T), spectrum(1)],
        out_specs=pl.BlockSpec((R, L), lambda c, b: (b * ncb + c, 0)),
        out_shape=jax.ShapeDtypeStruct((B * W, L), BF16),
        scratch_shapes=[pltpu.VMEM((nb, R, T), BF16)] * 4,
        name="hy_conv",
    )(z2d, gates2d, skip_col, *tables, hr, hi, hn)


def _merge_body(yg_ref, yh_ref, wg_ref, wh_ref, g0_ref, g1_ref, o_ref):
    tg = _dot(yg_ref[...], wg_ref[...])
    th = _dot(yh_ref[...], wh_ref[...], TN)
    o_ref[...] = (g0_ref[...] * tg + g1_ref[...] * th).astype(o_ref.dtype)


def _branch_merge(y_gla, y_hy_t, w_gla_o, w_hy_o, gates, B, L, D):
    VAL, W = y_gla.shape[1], w_hy_o.shape[0]
    bt = _blk(L, 512)
    nt = L // bt
    resident = lambda shape: pl.BlockSpec(shape, lambda b, t: (0, 0), pipeline_mode=pl.Buffered(1))
    return _call(
        _merge_body,
        grid=(B, nt),
        in_specs=[pl.BlockSpec((bt, VAL), lambda b, t: (b * nt + t, 0)),
                  pl.BlockSpec((W, bt), lambda b, t: (b, t)),
                  resident((VAL, D)), resident((W, D)),
                  pl.BlockSpec((bt, D), lambda b, t: (b * nt + t, 0)),
                  pl.BlockSpec((bt, D), lambda b, t: (b * nt + t, 1))],
        out_specs=pl.BlockSpec((bt, D), lambda b, t: (b * nt + t, 0)),
        out_shape=jax.ShapeDtypeStruct((B * L, D), BF16),
        name="branch_merge",
    )(y_gla, y_hy_t, w_gla_o, w_hy_o, gates, gates)


def _proj_ln_body(a_ref, w_ref, x_ref, g_ref, b_ref, o_ref, ob_ref, *, alpha):
    y = alpha * x_ref[...] + _dot(a_ref[...], w_ref[...])
    h = _layernorm_rows(y, g_ref[...], b_ref[...])
    o_ref[...] = h
    ob_ref[...] = h.astype(ob_ref.dtype)


def _proj_ln(a, w, x2d, ln_g, ln_b, alpha):
    M, K = a.shape
    D = w.shape[1]
    bm = _blk(M, 512)
    row = lambda i: (i, 0)
    full = lambda shape: pl.BlockSpec(shape, lambda i: (0, 0))
    return _call(
        functools.partial(_proj_ln_body, alpha=alpha),
        grid=(M // bm,),
        in_specs=[pl.BlockSpec((bm, K), row), full((K, D)), pl.BlockSpec((bm, D), row),
                  full((1, D)), full((1, D))],
        out_specs=[pl.BlockSpec((bm, D), row)] * 2,
        out_shape=[jax.ShapeDtypeStruct((M, D), F32), jax.ShapeDtypeStruct((M, D), BF16)],
        name="out_proj_ln1",
    )(a, w, x2d, ln_g.reshape(1, D), ln_b.reshape(1, D))


def _ff2_ln_body(a_ref, w_ref, h_ref, g_ref, b_ref, o_ref, *, alpha, ln_rows):
    kk = pl.program_id(1)

    @pl.when(kk == 0)
    def _():
        o_ref[...] = jnp.zeros_like(o_ref)

    o_ref[...] += _dot(a_ref[...], w_ref[...])

    @pl.when(kk == pl.num_programs(1) - 1)
    def _():
        g, b = g_ref[...], b_ref[...]

        def norm_rows(r, carry):
            rows = pl.ds(pl.multiple_of(r * ln_rows, ln_rows), ln_rows)
            y = alpha * h_ref[rows, :] + o_ref[rows, :]
            o_ref[rows, :] = _layernorm_rows(y, g, b)
            return carry

        lax.fori_loop(0, o_ref.shape[0] // ln_rows, norm_rows, 0)


def _ff2_ln(a, w, h, ln_g, ln_b, alpha):
    M, K = a.shape
    D = w.shape[1]
    bm, bk = _blk(M, 1024), _blk(K, 1024)
    return _call(
        functools.partial(_ff2_ln_body, alpha=alpha, ln_rows=_blk(bm, 128)),
        grid=(M // bm, K // bk),
        in_specs=[pl.BlockSpec((bm, bk), lambda i, k: (i, k)),
                  pl.BlockSpec((bk, D), lambda i, k: (k, 0)),
                  pl.BlockSpec((bm, D), lambda i, k: (i, 0)),
                  pl.BlockSpec((1, D), lambda i, k: (0, 0)),
                  pl.BlockSpec((1, D), lambda i, k: (0, 0))],
        out_specs=pl.BlockSpec((bm, D), lambda i, k: (i, 0)),
        out_shape=jax.ShapeDtypeStruct((M, D), F32),
        name="ff2_ln2",
    )(a, w, h, ln_g.reshape(1, D), ln_b.reshape(1, D))


def _dft_tables(L):
    LO = _blk(L, 32)
    k = jnp.arange(L, dtype=jnp.int32)
    ang = lambda m: ((m[:, None] * k[None, :]) % (2 * L)).astype(F32) * (math.pi / L)
    a_hi = ang(jnp.arange(L // LO, dtype=jnp.int32) * LO)[:, None, :]
    a_lo = ang(jnp.arange(LO, dtype=jnp.int32))[None, :, :]
    cos_t = (jnp.cos(a_hi) * jnp.cos(a_lo) - jnp.sin(a_hi) * jnp.sin(a_lo)).reshape(L, L)
    sin_t = (jnp.sin(a_hi) * jnp.cos(a_lo) + jnp.cos(a_hi) * jnp.sin(a_lo)).reshape(L, L)
    nyq = jnp.where(k % 2 == 0, 1.0, -1.0).astype(F32)
    smat = jnp.where(k[None, :] == 0, nyq[:, None], sin_t)
    smat_t = jnp.where(k[:, None] == 0, nyq[None, :], sin_t)
    return cos_t.astype(BF16), smat.astype(BF16), smat_t.astype(BF16)


def _position_features(L, emb_dim):
    t = jnp.linspace(0.0, 1.0, L, dtype=F32)[:, None]
    bands = (emb_dim - 1) // 2
    f = jnp.linspace(1e-4, bands - 1, bands, dtype=F32)
    wpos = 2.0 * math.pi * jnp.arange(L, dtype=F32) / L
    ang = wpos[:, None] * f[None, :]
    return jnp.concatenate([t, jnp.cos(ang), -jnp.sin(ang)], axis=-1).T


def _layer(h, w_in, gla_wa2_f, gla_ba_f, gla_wa2_b, gla_ba_b, gla_norm_g, w_gla_o,
           hy_conv_w, hy_conv_b, hy_w1, hy_b1, hy_w2, hy_b2, hy_w3, hy_b3, hy_w4, hy_b4,
           hy_freq, hy_skip, w_hy_o, w_out, ln1_g, ln1_b, w_ff1, w_ff2, ln2_g, ln2_b, alpha):
    B, L, D = h.shape
    M = B * L
    rank, KEY = gla_wa2_f.shape
    VAL = gla_norm_g.shape[0]
    n_ord, W = hy_skip.shape

    sizes = (KEY, KEY, VAL, VAL, rank, rank, (n_ord + 1) * W, 2 * D)
    offs = [int(v) for v in np.concatenate([[0], np.cumsum(sizes)])]
    w_in_t = w_in.T
    w_hy_t = w_in_t[offs[6]:offs[7]].astype(BF16)

    x2d = h.reshape(M, D)
    gla_block = _blk(L, GLA_BLOCK)
    cum_f, cum_b, xb = _gla_decays(x2d, w_in_t, offs[4], gla_wa2_f, gla_ba_f, gla_wa2_b, gla_ba_b, gla_block)

    p_qkvr = _matmul_f32w(xb, w_in_t, offs[4], out_dtype=BF16, name="in_proj_qkvr", bm=2048, transposed=True)
    gates = _matmul_f32w(xb, w_in_t, sizes[7], out_dtype=BF16, act="sigmoid", name="in_proj_gates", bm=2048,
                         transposed=True, row0=offs[7])
    conv_params = jnp.concatenate([hy_conv_w.T, hy_conv_b[:, None]], axis=1)
    u_t = _hy_proj(w_hy_t, xb, conv_params, B, L)

    y_gla = _gla(p_qkvr, cum_f, cum_b, gla_norm_g, B, L, KEY, VAL, gla_block)

    emb_t = _position_features(L, hy_w1.shape[0])
    min_decay = math.log(HY_DECAY_TARGET) / HY_SLOW_DECAY
    max_decay = math.log(HY_DECAY_TARGET) / HY_FAST_DECAY
    deltas = jnp.abs(jnp.linspace(min_decay, max_decay, W, dtype=F32)).reshape(W, 1)
    T = _blk(L // 2, HY_BLOCK)
    tables = _dft_tables(T)
    bin_w = jnp.full((1, T), 1.0 / T, F32).at[0, 0].set(0.5 / T)
    h_re, h_im, h_ny = _hyena_lag_spectra(emb_t, hy_w1, hy_b1, hy_w2, hy_b2, hy_w3, hy_b3, hy_w4, hy_b4,
                                          hy_freq, deltas, tables[0], tables[1], bin_w, W, T)

    skip_col = hy_skip.reshape(n_ord * W, 1)
    z, z_row0, z_rows = u_t, 0, (n_ord + 1) * W
    for o in range(n_ord):
        z = _hy_conv(z, z_row0, z_rows, u_t, (o + 1) * W, (n_ord + 1) * W, skip_col, tables,
                     h_re, h_im, h_ny, o * W, B, W, L, T)
        z_row0, z_rows = 0, W
    y_hy_t = z

    merged = _branch_merge(y_gla, y_hy_t, w_gla_o.astype(BF16), w_hy_o.astype(BF16), gates, B, L, D)
    h1, h1b = _proj_ln(merged, w_out.astype(BF16), x2d, ln1_g, ln1_b, alpha)
    act = _matmul_f32w(h1b, w_ff1, w_ff1.shape[1], out_dtype=BF16, act="relu2", name="ff1_relu2", bm=2048)
    out = _ff2_ln(act, w_ff2.astype(BF16), h1, ln2_g, ln2_b, alpha)
    return out.reshape(B, L, D)


def kernel(x, w_in, gla_wa2_f, gla_ba_f, gla_wa2_b, gla_ba_b, gla_norm_g, w_gla_o, hy_conv_w, hy_conv_b, hy_w1, hy_b1, hy_w2, hy_b2, hy_w3, hy_b3, hy_w4, hy_b4, hy_freq, hy_skip, w_hy_o, w_out, ln1_g, ln1_b, w_ff1, w_ff2, ln2_g, ln2_b):
    depth = w_in.shape[0]
    alpha = (2 * depth) ** 0.25
    params = (w_in, gla_wa2_f, gla_ba_f, gla_wa2_b, gla_ba_b, gla_norm_g, w_gla_o, hy_conv_w, hy_conv_b,
              hy_w1, hy_b1, hy_w2, hy_b2, hy_w3, hy_b3, hy_w4, hy_b4, hy_freq, hy_skip, w_hy_o, w_out,
              ln1_g, ln1_b, w_ff1, w_ff2, ln2_g, ln2_b)
    h = x
    for l in range(depth):
        h = _layer(h, *(p[l] for p in params), alpha)
    return h
```
